```python
import math
import jax
import jax.numpy as jnp
from jax import lax
import numpy as np

D_MODEL = 2048
BATCH = 2
SEQ = 4096
DEPTH = 4

GRID_W = 64
CTX_LEN = 256
N_MIXERS = 3
N_MOD = 9
NORM_EPS = 1e-6
ROPE_THETA = 10000.0
NEG_INF = -1e30
D_FF = 5504
A_HEAD_DIM = 128
A_HEADS = D_MODEL // A_HEAD_DIM
A_KV_HEADS = 4
A_GROUP = A_HEADS // A_KV_HEADS
A_Q_WIDTH = A_HEADS * A_HEAD_DIM
A_KV_WIDTH = A_KV_HEADS * A_HEAD_DIM
A_WINDOW = 128
A_BLOCK = 128
B_CHUNK = 128
B_WIDTH = 3 * D_MODEL
B_GROUPS = 8
B_GROUP_W = B_WIDTH // B_GROUPS
C_HEAD_DIM = 128
C_HEADS = D_MODEL // (2 * C_HEAD_DIM)
C_WIDTH = C_HEADS * 2 * C_HEAD_DIM
C_BLOCK = 128

N_A = len(range(0, DEPTH, N_MIXERS))
N_B = len(range(1, DEPTH, N_MIXERS))
N_C = len(range(2, DEPTH, N_MIXERS))

kernel_name = 'hybrid_diffusion_block'


def rms_norm(x, g, eps=NORM_EPS):
    xf = x.astype(jnp.float32)
    y = xf * lax.rsqrt(jnp.mean(xf * xf, axis=-1, keepdims=True) + eps)
    return (y * g.astype(jnp.float32)).astype(x.dtype)


def layer_norm(x, g, b, eps=NORM_EPS):
    xf = x.astype(jnp.float32)
    mu = jnp.mean(xf, axis=-1, keepdims=True)
    xc = xf - mu
    y = xc * lax.rsqrt(jnp.mean(xc * xc, axis=-1, keepdims=True) + eps)
    return (y * g.astype(jnp.float32) + b.astype(jnp.float32)).astype(x.dtype)


def axial_rope(n_tokens, head_dim):
    rows = n_tokens // GRID_W
    row = jnp.repeat(jnp.arange(rows, dtype=jnp.float32), GRID_W)
    col = jnp.tile(jnp.arange(GRID_W, dtype=jnp.float32), rows)
    axis_dim = head_dim // 2
    inv = ROPE_THETA ** (-jnp.arange(0, axis_dim, 2, dtype=jnp.float32) / axis_dim)
    ang = jnp.concatenate([row[:, None] * inv, col[:, None] * inv], axis=-1)
    return jnp.cos(ang), jnp.sin(ang)


def apply_rope(x, cos, sin):
    xr = x.reshape(*x.shape[:-1], -1, 2)
    x0, x1 = xr[..., 0], xr[..., 1]
    c = cos[:, None, :].astype(x.dtype)
    s = sin[:, None, :].astype(x.dtype)
    return jnp.stack([x0 * c - x1 * s, x0 * s + x1 * c], axis=-1).reshape(x.shape)


def swiglu(h, w_in, w_out):
    g, u = jnp.split(h @ w_in, 2, axis=-1)
    return (jax.nn.silu(g) * u) @ w_out


def _pre(t, g, m, s):
    return rms_norm(t, g[2 * s]) * (1 + m[:, :, 3 * s + 1]) + m[:, :, 3 * s]


def _post(y, g, m, s):
    return m[:, :, 3 * s + 2] * rms_norm(y, g[2 * s + 1])


def _band(t):
    b, s = t.shape[:2]
    nb = s // A_BLOCK
    tp = jnp.pad(t, ((0, 0), (A_BLOCK, A_BLOCK), (0, 0), (0, 0)))
    tp = tp.reshape(b, nb + 2, A_BLOCK, *t.shape[2:])
    return jnp.concatenate([tp[:, :-2], tp[:, 1:-1], tp[:, 2:]], axis=2)


def window_gqa(h, hc, w_in, w_out, sink, cos, sin, need_ctx):
    b, s, _ = h.shape
    L = hc.shape[1]
    nb = s // A_BLOCK
    scale = A_HEAD_DIM ** -0.5

    def proj(t):
        T = t.shape[1]
        q, k, v = jnp.split(t @ w_in, [A_Q_WIDTH, A_Q_WIDTH + A_KV_WIDTH], axis=-1)
        return (q.reshape(b, T, A_HEADS, A_HEAD_DIM),
                k.reshape(b, T, A_KV_HEADS, A_HEAD_DIM),
                v.reshape(b, T, A_KV_HEADS, A_HEAD_DIM))

    q, k, v = proj(h)
    qc, kc, vc = proj(hc)
    q = apply_rope(q, cos, sin)
    k = apply_rope(k, cos, sin)
    qb = q.reshape(b, nb, A_BLOCK, A_KV_HEADS, A_GROUP, A_HEAD_DIM)
    kb, vb = _band(k), _band(v)
    qpos = jnp.arange(nb)[:, None, None] * A_BLOCK + jnp.arange(A_BLOCK)[None, :, None]
    kpos = (jnp.arange(nb)[:, None, None] - 1) * A_BLOCK + jnp.arange(3 * A_BLOCK)[None, None, :]
    valid = (jnp.abs(kpos - qpos) <= A_WINDOW) & (kpos >= 0) & (kpos < s)
    s_band = jnp.einsum('bnqkgd,bnmkd->bnkgqm', qb, kb).astype(jnp.float32) * scale
    s_band = jnp.where(valid[None, :, None, None], s_band, NEG_INF)
    s_ctx = jnp.einsum('bnqkgd,blkd->bnkgql', qb, kc).astype(jnp.float32) * scale
    sink_l = jnp.broadcast_to(sink.reshape(A_KV_HEADS, A_GROUP, 1, 1).astype(jnp.float32), s_ctx.shape[:-1] + (1,))
    p = jax.nn.softmax(jnp.concatenate([sink_l, s_ctx, s_band], axis=-1), axis=-1).astype(v.dtype)
    o = (jnp.einsum('bnkgql,blkd->bnqkgd', p[..., 1:1 + L], vc)
         + jnp.einsum('bnkgqm,bnmkd->bnqkgd', p[..., 1 + L:], vb))
    y = o.reshape(b, s, A_Q_WIDTH) @ w_out
    yc = None
    if need_ctx:
        qg = qc.reshape(b, L, A_KV_HEADS, A_GROUP, A_HEAD_DIM)
        sc = jnp.einsum('blkgd,bmkd->bkglm', qg, kc).astype(jnp.float32) * scale
        sink_c = jnp.broadcast_to(sink.reshape(A_KV_HEADS, A_GROUP, 1, 1).astype(jnp.float32), sc.shape[:-1] + (1,))
        pc = jax.nn.softmax(jnp.concatenate([sink_c, sc], axis=-1), axis=-1)[..., 1:].astype(vc.dtype)
        oc = jnp.einsum('bkglm,bmkd->blkgd', pc, vc)
        yc = oc.reshape(b, L, A_Q_WIDTH) @ w_out
    return y, yc


def chunk_mlp(t, w_in, vn_g, vn_b, w_s, b_s, w_out):
    b, T, _ = t.shape
    u, v = jnp.split(jax.nn.gelu(t @ w_in), 2, axis=-1)
    v = layer_norm(v, vn_g, vn_b)
    v = v.reshape(b, T // B_CHUNK, B_CHUNK, B_GROUPS, B_GROUP_W)
    v = jnp.einsum('gpq,bnqgc->bnpgc', w_s, v) + b_s.T[:, :, None]
    return (u * v.reshape(b, T, B_WIDTH)) @ w_out


def diff_attn(h, hc, w_in, w_out, lq1, lk1, lq2, lk2, subln_g, lam_init, cos, sin, need_ctx):
    b, s, _ = h.shape
    L = hc.shape[1]
    nb = s // C_BLOCK
    scale = C_HEAD_DIM ** -0.5
    lam = (jnp.exp(jnp.sum(lq1.astype(jnp.float32) * lk1.astype(jnp.float32)))
           - jnp.exp(jnp.sum(lq2.astype(jnp.float32) * lk2.astype(jnp.float32))) + lam_init)

    def proj(t):
        T = t.shape[1]
        q, k, v = jnp.split(t @ w_in, 3, axis=-1)
        return (q.reshape(b, T, C_HEADS, 2, C_HEAD_DIM),
                k.reshape(b, T, C_HEADS, 2, C_HEAD_DIM),
                v.reshape(b, T, C_HEADS, 2 * C_HEAD_DIM))

    def rope2(t):
        return apply_rope(t.reshape(b, s, 2 * C_HEADS, C_HEAD_DIM), cos, sin).reshape(t.shape)

    def finish(o):
        o = rms_norm(o, subln_g) * (1 - lam_init)
        return o.reshape(o.shape[0], o.shape[1], C_WIDTH) @ w_out

    q, k, v = proj(h)
    qc, kc, vc = proj(hc)
    q, k = rope2(q), rope2(k)
    qblocks = jnp.moveaxis(q.reshape(b, nb, C_BLOCK, C_HEADS, 2, C_HEAD_DIM), 1, 0)

    def block(qb):
        s_ctx = jnp.einsum('bqhcd,bkhcd->bhcqk', qb, kc).astype(jnp.float32) * scale
        s_lat = jnp.einsum('bqhcd,bkhcd->bhcqk', qb, k).astype(jnp.float32) * scale
        p = jax.nn.softmax(jnp.concatenate([s_ctx, s_lat], axis=-1), axis=-1)
        a = (p[:, :, 0] - lam * p[:, :, 1]).astype(v.dtype)
        return (jnp.einsum('bhqk,bkhe->bqhe', a[..., :L], vc)
                + jnp.einsum('bhqk,bkhe->bqhe', a[..., L:], v))

    o = lax.map(block, qblocks)
    o = jnp.moveaxis(o, 0, 1).reshape(b, s, C_HEADS, 2 * C_HEAD_DIM)
    y = finish(o)
    yc = None
    if need_ctx:
        sc = jnp.einsum('bqhcd,bkhcd->bhcqk', qc, kc).astype(jnp.float32) * scale
        pc = jax.nn.softmax(sc, axis=-1)
        ac = (pc[:, :, 0] - lam * pc[:, :, 1]).astype(vc.dtype)
        yc = finish(jnp.einsum('bhqk,bkhe->bqhe', ac, vc))
    return y, yc


def setup_inputs(seed: int = 0) -> dict:
    key = jax.random.key(seed)
    ks = jax.random.split(key, 32)
    D = D_MODEL

    def nrm(k, shape, s):
        return jax.random.normal(k, shape, jnp.float32) * s

    return {
        'x': nrm(ks[0], (BATCH, SEQ, D), 1.0),
        'c': nrm(ks[1], (BATCH, D), 1.0),
        'ctx': nrm(ks[2], (BATCH, CTX_LEN, D), 1.0),
        'c_ctx': nrm(ks[3], (D,), 1.0),
        'ada_w': nrm(ks[4], (DEPTH, D, N_MOD * D), 0.5 * D ** -0.5),
        'ada_b': nrm(ks[5], (DEPTH, N_MOD * D), 0.01),
        'norm_g': 1.0 + nrm(ks[6], (DEPTH, 6, D), 0.02),
        'ffn_w_in': nrm(ks[7], (DEPTH, 2, D, 2 * D_FF), D ** -0.5),
        'ffn_w_out': nrm(ks[8], (DEPTH, 2, D_FF, D), D_FF ** -0.5),
        'a_w_in': nrm(ks[9], (N_A, D, A_Q_WIDTH + 2 * A_KV_WIDTH), D ** -0.5),
        'a_w_out': nrm(ks[10], (N_A, A_Q_WIDTH, D), A_Q_WIDTH ** -0.5),
        'a_sink': nrm(ks[11], (N_A, A_HEADS), 1.0),
        'b_w_in': nrm(ks[12], (N_B, D, 2 * B_WIDTH), D ** -0.5),
        'b_vnorm_g': 1.0 + nrm(ks[13], (N_B, B_WIDTH), 0.02),
        'b_vnorm_b': nrm(ks[14], (N_B, B_WIDTH), 0.02),
        'b_ws': nrm(ks[15], (N_B, B_GROUPS, B_CHUNK, B_CHUNK), B_CHUNK ** -0.5),
        'b_bs': 1.0 + nrm(ks[16], (N_B, B_GROUPS, B_CHUNK), 0.1),
        'b_w_out': nrm(ks[17], (N_B, B_WIDTH, D), B_WIDTH ** -0.5),
        'c_w_in': nrm(ks[18], (N_C, D, 3 * C_WIDTH), D ** -0.5),
        'c_w_out': nrm(ks[19], (N_C, C_WIDTH, D), C_WIDTH ** -0.5),
        'c_lq1': nrm(ks[20], (N_C, C_HEAD_DIM), 0.1),
        'c_lk1': nrm(ks[21], (N_C, C_HEAD_DIM), 0.1),
        'c_lq2': nrm(ks[22], (N_C, C_HEAD_DIM), 0.1),
        'c_lk2': nrm(ks[23], (N_C, C_HEAD_DIM), 0.1),
        'c_subln_g': 1.0 + nrm(ks[24], (N_C, 2 * C_HEAD_DIM), 0.02),
    }


def reference(x, c, ctx, c_ctx, ada_w, ada_b, norm_g, ffn_w_in, ffn_w_out,
              a_w_in, a_w_out, a_sink,
              b_w_in, b_vnorm_g, b_vnorm_b, b_ws, b_bs, b_w_out,
              c_w_in, c_w_out, c_lq1, c_lk1, c_lq2, c_lk2, c_subln_g):
    b, s, d = x.shape
    cos, sin = axial_rope(s, A_HEAD_DIM)
    for i in range(DEPTH):
        kind, j = i % N_MIXERS, i // N_MIXERS
        ctx_live = i < DEPTH - 1
        ctx_read = ctx_live or kind != 1
        g = norm_g[i]
        m = (jax.nn.silu(c) @ ada_w[i] + ada_b[i]).reshape(b, 1, N_MOD, d)
        x = x + 0.5 * _post(swiglu(_pre(x, g, m, 0), ffn_w_in[i, 0], ffn_w_out[i, 0]), g, m, 0)
        hc = None
        if ctx_read:
            mc = (jax.nn.silu(c_ctx) @ ada_w[i] + ada_b[i]).reshape(1, 1, N_MOD, d)
            ctx = ctx + 0.5 * _post(swiglu(_pre(ctx, g, mc, 0), ffn_w_in[i, 0], ffn_w_out[i, 0]), g, mc, 0)
            hc = _pre(ctx, g, mc, 1)
        h = _pre(x, g, m, 1)
        if kind == 0:
            y, yc = window_gqa(h, hc, a_w_in[j], a_w_out[j], a_sink[j], cos, sin, ctx_live)
        elif kind == 1:
            y = chunk_mlp(h, b_w_in[j], b_vnorm_g[j], b_vnorm_b[j], b_ws[j], b_bs[j], b_w_out[j])
            yc = chunk_mlp(hc, b_w_in[j], b_vnorm_g[j], b_vnorm_b[j], b_ws[j], b_bs[j], b_w_out[j]) if ctx_live else None
        else:
            lam_init = 0.8 - 0.6 * math.exp(-0.3 * i)
            y, yc = diff_attn(h, hc, c_w_in[j], c_w_out[j], c_lq1[j], c_lk1[j], c_lq2[j], c_lk2[j],
                              c_subln_g[j], lam_init, cos, sin, ctx_live)
        x = x + _post(y, g, m, 1)
        x = x + 0.5 * _post(swiglu(_pre(x, g, m, 2), ffn_w_in[i, 1], ffn_w_out[i, 1]), g, m, 2)
        if ctx_live:
            ctx = ctx + _post(yc, g, mc, 1)
            ctx = ctx + 0.5 * _post(swiglu(_pre(ctx, g, mc, 2), ffn_w_in[i, 1], ffn_w_out[i, 1]), g, mc, 2)
    return x
```

```python
import functools
import math

import jax
import jax.numpy as jnp
from jax import lax
from jax.experimental import pallas as pl
from jax.experimental.pallas import tpu as pltpu

D_MODEL = 2048
BATCH = 2
SEQ = 4096
DEPTH = 4
GRID_W = 64
CTX_LEN = 256
N_MIXERS = 3
N_MOD = 9
NORM_EPS = 1e-6
ROPE_THETA = 10000.0
NEG_INF = -1e30
D_FF = 5504
HEAD_DIM = 128
A_HEADS = 16
A_KV_HEADS = 4
A_GROUP = 4
A_BLOCK = 128
B_CHUNK = 128
B_WIDTH = 3 * D_MODEL
B_GROUPS = 8
B_GROUP_W = B_WIDTH // B_GROUPS
C_HEADS = 8

LAT_ROWS = BATCH * SEQ
CTX_ROWS = BATCH * CTX_LEN
ROWS = LAT_ROWS + CTX_ROWS
N_GROUPS = 3
LANES = 128
FF_TILE = 512
D_FF_PAD = -(-D_FF // FF_TILE) * FF_TILE
VMEM_LIMIT = 56 * 1024 * 1024

BF16 = jnp.bfloat16
F32 = jnp.float32

P_POST_G, P_GATE, P_PRE_G, P_SCALE, P_SHIFT = 0, 1, 2, 3, 4


def _params(**kw):
    return pltpu.CompilerParams(vmem_limit_bytes=VMEM_LIMIT, **kw)


def _group_of_block(i, tm):
    return jnp.minimum(i // (SEQ // tm), N_GROUPS - 1)


def _rms(x):
    return x * lax.rsqrt(jnp.mean(x * x, axis=-1, keepdims=True) + NORM_EPS)


def _ada_kernel(c_ref, w_ref, b_ref, o_ref):
    c = c_ref[...]
    a = (c * jax.nn.sigmoid(c)).astype(BF16)
    o_ref[...] = jnp.dot(a, w_ref[...].astype(BF16), preferred_element_type=F32) + b_ref[...]


def _ada_mods(cvec, ada_w, ada_b):
    tn = 1024
    n = N_MOD * D_MODEL
    return pl.pallas_call(
        _ada_kernel,
        grid=(DEPTH, n // tn),
        in_specs=[
            pl.BlockSpec((8, D_MODEL), lambda l, j: (0, 0)),
            pl.BlockSpec((None, D_MODEL, tn), lambda l, j: (l, 0, j)),
            pl.BlockSpec((None, 1, tn), lambda l, j: (l, 0, j)),
        ],
        out_specs=pl.BlockSpec((None, 8, tn), lambda l, j: (l, 0, j)),
        out_shape=jax.ShapeDtypeStruct((DEPTH, 8, n), F32),
        compiler_params=_params(dimension_semantics=("arbitrary", "arbitrary")),
        name="ada_mods",
    )(cvec, ada_w, ada_b.reshape(DEPTH, 1, n))


def _pre(x, p_ref):
    return (_rms(x) * p_ref[P_PRE_G:P_PRE_G + 1, :]) * (1.0 + p_ref[P_SCALE:P_SCALE + 1, :]) \
        + p_ref[P_SHIFT:P_SHIFT + 1, :]


def _prenorm_kernel(x_ref, p_ref, h_ref):
    h_ref[...] = _pre(x_ref[...], p_ref).astype(BF16)


def _prenorm(x, pp, s, tm=512):
    rows = x.shape[0]
    return pl.pallas_call(
        _prenorm_kernel,
        grid=(rows // tm,),
        in_specs=[
            pl.BlockSpec((tm, D_MODEL), lambda i: (i, 0)),
            pl.BlockSpec((None, None, 8, D_MODEL), lambda i: (s, _group_of_block(i, tm), 0, 0)),
        ],
        out_specs=pl.BlockSpec((tm, D_MODEL), lambda i: (i, 0)),
        out_shape=jax.ShapeDtypeStruct((rows, D_MODEL), BF16),
        compiler_params=_params(dimension_semantics=("arbitrary",)),
        name="prenorm",
    )(x, pp)


def _swiglu_kernel(h_ref, wg_ref, wu_ref, o_ref):
    h = h_ref[...]
    g = jnp.dot(h, wg_ref[...], preferred_element_type=F32)
    u = jnp.dot(h, wu_ref[...], preferred_element_type=F32)
    o_ref[...] = (g * jax.nn.sigmoid(g) * u).astype(BF16)


def _ffn_in(h, wgu, tm=512, tn=FF_TILE):
    rows = h.shape[0]
    return pl.pallas_call(
        _swiglu_kernel,
        grid=(D_FF_PAD // tn, rows // tm),
        in_specs=[
            pl.BlockSpec((tm, D_MODEL), lambda j, i: (i, 0)),
            pl.BlockSpec((None, D_MODEL, tn), lambda j, i: (0, 0, j)),
            pl.BlockSpec((None, D_MODEL, tn), lambda j, i: (1, 0, j)),
        ],
        out_specs=pl.BlockSpec((tm, tn), lambda j, i: (i, j)),
        out_shape=jax.ShapeDtypeStruct((rows, D_FF_PAD), BF16),
        compiler_params=_params(dimension_semantics=("arbitrary", "arbitrary")),
        name="ffn_in",
    )(h, wgu, wgu)


def _gelu_kernel(h_ref, w_ref, o_ref):
    y = jnp.dot(h_ref[...], w_ref[...], preferred_element_type=F32)
    o_ref[...] = jax.nn.gelu(y).astype(BF16)


def _gmlp_in(h, w, tm=512, tn=1024):
    rows = h.shape[0]
    n = w.shape[1]
    return pl.pallas_call(
        _gelu_kernel,
        grid=(n // tn, rows // tm),
        in_specs=[
            pl.BlockSpec((tm, D_MODEL), lambda j, i: (i, 0)),
            pl.BlockSpec((D_MODEL, tn), lambda j, i: (0, j)),
        ],
        out_specs=pl.BlockSpec((tm, tn), lambda j, i: (i, j)),
        out_shape=jax.ShapeDtypeStruct((rows, n), BF16),
        compiler_params=_params(dimension_semantics=("arbitrary", "arbitrary")),
        name="gmlp_in",
    )(h, w)


def _qkv_rope_kernel(h_ref, w_ref, cos_ref, sin_ref, o_ref, *, n_rope_tiles, tm, tn):
    j = pl.program_id(0)
    y = jnp.dot(h_ref[...], w_ref[...], preferred_element_type=F32)

    @pl.when(j < n_rope_tiles)
    def _():
        c = cos_ref[...]
        s = sin_ref[...]
        lane = lax.broadcasted_iota(jnp.int32, (tm, LANES), 1)
        even = (lane & 1) == 0
        for q in range(tn // LANES):
            xh = y[:, q * LANES:(q + 1) * LANES]
            partner = jnp.where(even, pltpu.roll(xh, LANES - 1, 1), pltpu.roll(xh, 1, 1))
            o_ref[:, q * LANES:(q + 1) * LANES] = (xh * c + partner * s).astype(BF16)

    @pl.when(j >= n_rope_tiles)
    def _():
        o_ref[...] = y.astype(BF16)


def _qkv_rope(h, w, cos_t, sin_t, rope_width, tm=512, tn=512):
    rows = h.shape[0]
    n = w.shape[1]
    kern = functools.partial(_qkv_rope_kernel, n_rope_tiles=rope_width // tn, tm=tm, tn=tn)
    return pl.pallas_call(
        kern,
        grid=(n // tn, rows // tm),
        in_specs=[
            pl.BlockSpec((tm, D_MODEL), lambda j, i: (i, 0)),
            pl.BlockSpec((D_MODEL, tn), lambda j, i: (0, j)),
            pl.BlockSpec((tm, LANES), lambda j, i: (i, 0)),
            pl.BlockSpec((tm, LANES), lambda j, i: (i, 0)),
        ],
        out_specs=pl.BlockSpec((tm, tn), lambda j, i: (i, j)),
        out_shape=jax.ShapeDtypeStruct((rows, n), BF16),
        compiler_params=_params(dimension_semantics=("arbitrary", "arbitrary")),
        name="qkv_rope",
    )(h, w, cos_t, sin_t)


def _mm_out_kernel(a_ref, w_ref, x_ref, p_ref, xo_ref, *maybe_h_ref, coef):
    y = jnp.dot(a_ref[...], w_ref[...], preferred_element_type=F32)
    r = _rms(y) * p_ref[P_POST_G:P_POST_G + 1, :]
    xn = x_ref[...] + coef * (p_ref[P_GATE:P_GATE + 1, :] * r)
    xo_ref[...] = xn
    if maybe_h_ref:
        maybe_h_ref[0][...] = _pre(xn, p_ref).astype(BF16)


def _mm_out(a, w, x, pp, s, *, rows, coef, want_h, tm=256):
    k = a.shape[1]
    out_shape = [jax.ShapeDtypeStruct((rows, D_MODEL), F32)]
    out_specs = [pl.BlockSpec((tm, D_MODEL), lambda i: (i, 0))]
    if want_h:
        out_shape.append(jax.ShapeDtypeStruct((rows, D_MODEL), BF16))
        out_specs.append(pl.BlockSpec((tm, D_MODEL), lambda i: (i, 0)))
    res = pl.pallas_call(
        functools.partial(_mm_out_kernel, coef=coef),
        grid=(rows // tm,),
        in_specs=[
            pl.BlockSpec((tm, k), lambda i: (i, 0)),
            pl.BlockSpec((k, D_MODEL), lambda i: (0, 0), pipeline_mode=pl.Buffered(1)),
            pl.BlockSpec((tm, D_MODEL), lambda i: (i, 0)),
            pl.BlockSpec((None, None, 8, D_MODEL), lambda i: (s, _group_of_block(i, tm), 0, 0)),
        ],
        out_specs=out_specs,
        out_shape=out_shape,
        compiler_params=_params(dimension_semantics=("arbitrary",)),
        name="mm_out",
    )(a, w, x, pp)
    return (res[0], res[1]) if want_h else (res[0], None)


def _dot_t(a, b):
    return lax.dot_general(a, b, (((1,), (1,)), ((), ())), preferred_element_type=F32)


def _win_attn_kernel(sink_ref, q_ref, kl_ref, km_ref, kr_ref, vl_ref, vm_ref, vr_ref, kc_ref, vc_ref, o_ref):
    n = pl.program_id(1)
    n_lat = SEQ // A_BLOCK
    scale = HEAD_DIM ** -0.5
    gq = A_GROUP * A_BLOCK
    row = lax.broadcasted_iota(jnp.int32, (gq, 1), 0)
    qi = lax.broadcasted_iota(jnp.int32, (gq, A_BLOCK), 0) & (A_BLOCK - 1)
    mi = lax.broadcasted_iota(jnp.int32, (gq, A_BLOCK), 1)

    def heads(kk):
        q4 = jnp.concatenate(
            [q_ref[:, (kk * A_GROUP + g) * HEAD_DIM:(kk * A_GROUP + g + 1) * HEAD_DIM] for g in range(A_GROUP)],
            axis=0)
        sink = jnp.full((gq, 1), sink_ref[kk * A_GROUP], F32)
        for g in range(1, A_GROUP):
            sink = jnp.where(row >= g * A_BLOCK, sink_ref[kk * A_GROUP + g], sink)
        return q4, sink, slice(kk * HEAD_DIM, (kk + 1) * HEAD_DIM)

    def store(kk, o):
        o = o.astype(BF16)
        for g in range(A_GROUP):
            h = kk * A_GROUP + g
            o_ref[:, h * HEAD_DIM:(h + 1) * HEAD_DIM] = o[g * A_BLOCK:(g + 1) * A_BLOCK, :]

    @pl.when(n < n_lat)
    def _():
        off_l = jnp.where(n > 0, 0, 2 * A_BLOCK)
        off_r = jnp.where(n < n_lat - 1, 0, 2 * A_BLOCK)
        ok_l = mi >= qi + off_l
        ok_r = mi <= qi - off_r
        for kk in range(A_KV_HEADS):
            q4, sink, cs = heads(kk)
            s_c = _dot_t(q4, kc_ref[:, cs]) * scale
            s_l = jnp.where(ok_l, _dot_t(q4, kl_ref[:, cs]) * scale, NEG_INF)
            s_m = _dot_t(q4, km_ref[:, cs]) * scale
            s_r = jnp.where(ok_r, _dot_t(q4, kr_ref[:, cs]) * scale, NEG_INF)
            m = jnp.maximum(jnp.maximum(jnp.max(s_c, axis=-1, keepdims=True), jnp.max(s_l, axis=-1, keepdims=True)),
                            jnp.maximum(jnp.max(s_m, axis=-1, keepdims=True), jnp.max(s_r, axis=-1, keepdims=True)))
            m = jnp.maximum(m, sink)
            p_c = jnp.exp(s_c - m)
            p_l = jnp.exp(s_l - m)
            p_m = jnp.exp(s_m - m)
            p_r = jnp.exp(s_r - m)
            den = (jnp.exp(sink - m) + jnp.sum(p_c, axis=-1, keepdims=True) + jnp.sum(p_l, axis=-1, keepdims=True)
                   + jnp.sum(p_m, axis=-1, keepdims=True) + jnp.sum(p_r, axis=-1, keepdims=True))
            o = (jnp.dot(p_c.astype(BF16), vc_ref[:, cs], preferred_element_type=F32)
                 + jnp.dot(p_l.astype(BF16), vl_ref[:, cs], preferred_element_type=F32)
                 + jnp.dot(p_m.astype(BF16), vm_ref[:, cs], preferred_element_type=F32)
                 + jnp.dot(p_r.astype(BF16), vr_ref[:, cs], preferred_element_type=F32))
            store(kk, o / den)

    @pl.when(n >= n_lat)
    def _():
        for kk in range(A_KV_HEADS):
            q4, sink, cs = heads(kk)
            s_c = _dot_t(q4, kc_ref[:, cs]) * scale
            m = jnp.maximum(jnp.max(s_c, axis=-1, keepdims=True), sink)
            p_c = jnp.exp(s_c - m)
            den = jnp.exp(sink - m) + jnp.sum(p_c, axis=-1, keepdims=True)
            o = jnp.dot(p_c.astype(BF16), vc_ref[:, cs], preferred_element_type=F32)
            store(kk, o / den)


def _win_attn(qkv, sink, with_ctx):
    n_lat = SEQ // A_BLOCK
    n_ctx = CTX_LEN // A_BLOCK
    steps = n_lat + (n_ctx if with_ctx else 0)
    kvw = A_KV_HEADS * HEAD_DIM
    kcol = A_HEADS * HEAD_DIM // kvw
    ctx_blk0 = LAT_ROWS // A_BLOCK

    def qrow(b, n):
        return jnp.where(n < n_lat, b * n_lat + n, ctx_blk0 + b * n_ctx + (n - n_lat))

    def nb(b, n, d):
        return b * n_lat + jnp.clip(n + d, 0, n_lat - 1)

    kv_spec = lambda d, c: pl.BlockSpec((A_BLOCK, kvw), lambda b, n: (nb(b, n, d), c))
    ctx_spec = lambda c: pl.BlockSpec((CTX_LEN, kvw), lambda b, n: (LAT_ROWS // CTX_LEN + b, c))
    rows = LAT_ROWS + (CTX_ROWS if with_ctx else 0)
    return pl.pallas_call(
        _win_attn_kernel,
        grid=(BATCH, steps),
        in_specs=[
            pl.BlockSpec(memory_space=pltpu.SMEM),
            pl.BlockSpec((A_BLOCK, A_HEADS * HEAD_DIM), lambda b, n: (qrow(b, n), 0)),
            kv_spec(-1, kcol), kv_spec(0, kcol), kv_spec(1, kcol),
            kv_spec(-1, kcol + 1), kv_spec(0, kcol + 1), kv_spec(1, kcol + 1),
            ctx_spec(kcol), ctx_spec(kcol + 1),
        ],
        out_specs=pl.BlockSpec((A_BLOCK, D_MODEL), lambda b, n: (qrow(b, n), 0)),
        out_shape=jax.ShapeDtypeStruct((rows, D_MODEL), BF16),
        compiler_params=_params(dimension_semantics=("arbitrary", "arbitrary")),
        name="win_attn",
    )(sink, qkv, qkv, qkv, qkv, qkv, qkv, qkv, qkv, qkv)


def _gmlp_mid_kernel(u_ref, v_ref, g_ref, b_ref, ws_ref, bs_ref, o_ref, *, tm):
    v = v_ref[...].astype(F32)
    mu = jnp.mean(v, axis=-1, keepdims=True)
    vc = v - mu
    vn = vc * lax.rsqrt(jnp.mean(vc * vc, axis=-1, keepdims=True) + NORM_EPS)
    vn = (vn * g_ref[...] + b_ref[...]).astype(BF16)
    for c in range(tm // B_CHUNK):
        rs = slice(c * B_CHUNK, (c + 1) * B_CHUNK)
        for g in range(B_GROUPS):
            cs = slice(g * B_GROUP_W, (g + 1) * B_GROUP_W)
            mixed = jnp.dot(ws_ref[g], vn[rs, cs], preferred_element_type=F32) + bs_ref[:, g:g + 1]
            o_ref[rs, cs] = (u_ref[rs, cs].astype(F32) * mixed).astype(BF16)


def _gmlp_mid(z, vn_g, vn_b, ws, bs_t, tm=256):
    rows = z.shape[0]
    return pl.pallas_call(
        functools.partial(_gmlp_mid_kernel, tm=tm),
        grid=(rows // tm,),
        in_specs=[
            pl.BlockSpec((tm, B_WIDTH), lambda i: (i, 0)),
            pl.BlockSpec((tm, B_WIDTH), lambda i: (i, 1)),
            pl.BlockSpec((1, B_WIDTH), lambda i: (0, 0)),
            pl.BlockSpec((1, B_WIDTH), lambda i: (0, 0)),
            pl.BlockSpec((B_GROUPS, B_CHUNK, B_CHUNK), lambda i: (0, 0, 0)),
            pl.BlockSpec((B_CHUNK, B_GROUPS), lambda i: (0, 0)),
        ],
        out_specs=pl.BlockSpec((tm, B_WIDTH), lambda i: (i, 0)),
        out_shape=jax.ShapeDtypeStruct((rows, B_WIDTH), BF16),
        compiler_params=_params(dimension_semantics=("arbitrary",)),
        name="gmlp_mid",
    )(z, z, vn_g, vn_b, ws, bs_t)


def _diff_attn_kernel(lq1_ref, lk1_ref, lq2_ref, lk2_ref, sg_ref, q_ref, k_ref, v_ref, kc_ref, vc_ref, o_ref,
                      *, lam_init, tq, tk):
    qi = pl.program_id(2)
    n_lat = SEQ // tq
    scale = HEAD_DIM ** -0.5
    lam = (jnp.exp(jnp.sum(lq1_ref[...] * lk1_ref[...], axis=-1, keepdims=True))
           - jnp.exp(jnp.sum(lq2_ref[...] * lk2_ref[...], axis=-1, keepdims=True)) + lam_init)

    def run(chunks):
        outs = []
        for c in range(2):
            q = q_ref[:, c * HEAD_DIM:(c + 1) * HEAD_DIM]
            m = l = acc = None
            for kref, vref, lo, size in chunks:
                s = _dot_t(q, kref[lo:lo + size, c * HEAD_DIM:(c + 1) * HEAD_DIM]) * scale
                m_new = jnp.max(s, axis=-1, keepdims=True)
                if m is not None:
                    m_new = jnp.maximum(m, m_new)
                p = jnp.exp(s - m_new)
                pv = jnp.dot(p.astype(BF16), vref[lo:lo + size, :], preferred_element_type=F32)
                if m is None:
                    l = jnp.sum(p, axis=-1, keepdims=True)
                    acc = pv
                else:
                    alpha = jnp.exp(m - m_new)
                    l = alpha * l + jnp.sum(p, axis=-1, keepdims=True)
                    acc = alpha * acc + pv
                m = m_new
            outs.append(acc / l)
        o = outs[0] - lam * outs[1]
        o = _rms(o) * sg_ref[...] * (1.0 - lam_init)
        o_ref[...] = o.astype(BF16)

    @pl.when(qi < n_lat)
    def _():
        run([(kc_ref, vc_ref, 0, CTX_LEN)] + [(k_ref, v_ref, t * tk, tk) for t in range(SEQ // tk)])

    @pl.when(qi >= n_lat)
    def _():
        run([(kc_ref, vc_ref, 0, CTX_LEN)])


def _diff_attn(qkv, lq1, lk1, lq2, lk2, subln_g, lam_init, tq=256, tk=512):
    hw = 2 * HEAD_DIM
    n_lat = SEQ // tq
    n_ctx = CTX_LEN // tq
    kcol = D_MODEL // hw
    ctx_blk0 = LAT_ROWS // tq

    def qrow(b, qi):
        return jnp.where(qi < n_lat, b * n_lat + qi, ctx_blk0 + b * n_ctx + (qi - n_lat))

    vec = pl.BlockSpec((1, HEAD_DIM), lambda b, h, qi: (0, 0))
    kern = functools.partial(_diff_attn_kernel, lam_init=lam_init, tq=tq, tk=tk)
    return pl.pallas_call(
        kern,
        grid=(BATCH, C_HEADS, n_lat + n_ctx),
        in_specs=[
            vec, vec, vec, vec,
            pl.BlockSpec((1, hw), lambda b, h, qi: (0, 0)),
            pl.BlockSpec((tq, hw), lambda b, h, qi: (qrow(b, qi), h)),
            pl.BlockSpec((SEQ, hw), lambda b, h, qi: (b, kcol + h)),
            pl.BlockSpec((SEQ, hw), lambda b, h, qi: (b, 2 * kcol + h)),
            pl.BlockSpec((CTX_LEN, hw), lambda b, h, qi: (LAT_ROWS // CTX_LEN + b, kcol + h)),
            pl.BlockSpec((CTX_LEN, hw), lambda b, h, qi: (LAT_ROWS // CTX_LEN + b, 2 * kcol + h)),
        ],
        out_specs=pl.BlockSpec((tq, hw), lambda b, h, qi: (qrow(b, qi), h)),
        out_shape=jax.ShapeDtypeStruct((ROWS, D_MODEL), BF16),
        compiler_params=_params(dimension_semantics=("arbitrary", "arbitrary", "arbitrary")),
        name="diff_attn",
    )(lq1, lk1, lq2, lk2, subln_g, qkv, qkv, qkv, qkv, qkv)


def _rope_tables():
    rows = SEQ // GRID_W
    row = jnp.repeat(jnp.arange(rows, dtype=F32), GRID_W)
    col = jnp.tile(jnp.arange(GRID_W, dtype=F32), rows)
    axis_dim = HEAD_DIM // 2
    inv = ROPE_THETA ** (-jnp.arange(0, axis_dim, 2, dtype=F32) / axis_dim)
    ang = jnp.concatenate([row[:, None] * inv, col[:, None] * inv], axis=-1)
    cos = jnp.repeat(jnp.cos(ang), 2, axis=-1)
    sin = jnp.repeat(jnp.sin(ang), 2, axis=-1) * jnp.tile(jnp.array([-1.0, 1.0], F32), HEAD_DIM // 2)
    cos_t = jnp.concatenate([jnp.tile(cos, (BATCH, 1)), jnp.ones((CTX_ROWS, HEAD_DIM), F32)], axis=0)
    sin_t = jnp.concatenate([jnp.tile(sin, (BATCH, 1)), jnp.zeros((CTX_ROWS, HEAD_DIM), F32)], axis=0)
    return cos_t, sin_t


def _param_tiles(mods, norm_g):
    m = mods[:, :N_GROUPS].reshape(DEPTH, N_GROUPS, N_MOD, D_MODEL)
    zero = jnp.zeros((N_GROUPS, D_MODEL), F32)
    bcast = lambda v: jnp.broadcast_to(v, (N_GROUPS, D_MODEL))
    tiles = []
    for s in range(-1, 3 * DEPTH):
        rows = [zero] * 8
        if s >= 0:
            i, slot = divmod(s, 3)
            rows[P_POST_G] = bcast(norm_g[i, 2 * slot + 1])
            rows[P_GATE] = m[i, :, 3 * slot + 2]
        if s + 1 < 3 * DEPTH:
            i, slot = divmod(s + 1, 3)
            rows[P_PRE_G] = bcast(norm_g[i, 2 * slot])
            rows[P_SCALE] = m[i, :, 3 * slot + 1]
            rows[P_SHIFT] = m[i, :, 3 * slot]
        tiles.append(jnp.stack(rows, axis=1))
    return jnp.stack(tiles, axis=0)


def kernel(x, c, ctx, c_ctx, ada_w, ada_b, norm_g, ffn_w_in, ffn_w_out, a_w_in, a_w_out, a_sink, b_w_in, b_vnorm_g,
           b_vnorm_b, b_ws, b_bs, b_w_out, c_w_in, c_w_out, c_lq1, c_lk1, c_lq2, c_lk2, c_subln_g):
    xs = jnp.concatenate([x.reshape(LAT_ROWS, D_MODEL), ctx.reshape(CTX_ROWS, D_MODEL)], axis=0)
    cvec = jnp.concatenate([c, c_ctx[None, :], jnp.zeros((8 - BATCH - 1, D_MODEL), F32)], axis=0)
    pp = _param_tiles(_ada_mods(cvec, ada_w, ada_b), norm_g)
    cos_t, sin_t = _rope_tables()

    wgu = ffn_w_in.reshape(DEPTH, 2, D_MODEL, 2, D_FF)
    wgu = jnp.pad(wgu, ((0, 0),) * 4 + ((0, D_FF_PAD - D_FF),)).astype(BF16).transpose(0, 1, 3, 2, 4)
    wdn = jnp.pad(ffn_w_out, ((0, 0), (0, 0), (0, D_FF_PAD - D_FF), (0, 0))).astype(BF16)

    h = _prenorm(xs, pp, 0)
    for i in range(DEPTH):
        kind, j = i % N_MIXERS, i // N_MIXERS
        ctx_live = i < DEPTH - 1
        rows_out = ROWS if ctx_live else LAT_ROWS
        s0 = 3 * i + 1

        a = _ffn_in(h, wgu[i, 0])
        xs, h = _mm_out(a, wdn[i, 0], xs, pp, s0, rows=ROWS, coef=0.5, want_h=True)

        if kind == 0:
            qkv = _qkv_rope(h, a_w_in[j].astype(BF16), cos_t, sin_t, (A_HEADS + A_KV_HEADS) * HEAD_DIM)
            o = _win_attn(qkv, a_sink[j], ctx_live)
            w_o = a_w_out[j].astype(BF16)
        elif kind == 1:
            z = _gmlp_in(h, b_w_in[j].astype(BF16))
            o = _gmlp_mid(z, b_vnorm_g[j][None, :], b_vnorm_b[j][None, :], b_ws[j].astype(BF16), b_bs[j].T)
            w_o = b_w_out[j].astype(BF16)
        else:
            lam_init = 0.8 - 0.6 * math.exp(-0.3 * i)
            qkv = _qkv_rope(h, c_w_in[j].astype(BF16), cos_t, sin_t, 2 * D_MODEL)
            o = _diff_attn(qkv, c_lq1[j][None, :], c_lk1[j][None, :], c_lq2[j][None, :], c_lk2[j][None, :],
                           c_subln_g[j][None, :], lam_init)
            w_o = c_w_out[j].astype(BF16)
        xs, h = _mm_out(o, w_o, xs, pp, s0 + 1, rows=rows_out, coef=1.0, want_h=True)

        a = _ffn_in(h, wgu[i, 1])
        xs, h = _mm_out(a, wdn[i, 1], xs, pp, s0 + 2, rows=rows_out, coef=0.5, want_h=i + 1 < DEPTH)
    return xs.reshape(BATCH, SEQ, D_MODEL)
```

```python
import functools
import math
from typing import Callable, NamedTuple

import jax
import jax.numpy as jnp
from jax import lax
from jax.experimental import pallas as pl
from jax.experimental.pallas import tpu as pltpu

D_MODEL = 2048
BATCH = 2
SEQ = 4096
DEPTH = 4
GRID_W = 64
CTX_LEN = 256
N_MIXERS = 3
N_MOD = 9
NORM_EPS = 1e-6
ROPE_THETA = 10000.0
NEG_INF = -1e30
D_FF = 5504
HEAD_DIM = 128
A_HEADS = 16
A_KV_HEADS = 4
A_GROUP = 4
A_BLOCK = 128
B_CHUNK = 128
B_WIDTH = 3 * D_MODEL
B_GROUPS = 8
B_GROUP_W = B_WIDTH // B_GROUPS
C_HEADS = 8

LAT_ROWS = BATCH * SEQ
CTX_ROWS = BATCH * CTX_LEN
ROWS = LAT_ROWS + CTX_ROWS
N_GROUPS = 3
LANES = 128
BF16_SUBLANES = 16
FF_TILE = 512
D_FF_PAD = -(-D_FF // FF_TILE) * FF_TILE
VMEM_LIMIT = 56 * 1024 * 1024

BF16 = jnp.bfloat16
F32 = jnp.float32

P_POST_G, P_GATE, P_PRE_G, P_SCALE, P_SHIFT = 0, 1, 2, 3, 4


def _params(**kw):
    return pltpu.CompilerParams(vmem_limit_bytes=VMEM_LIMIT, **kw)


def _group_of_block(i, tm):
    return jnp.minimum(i // (SEQ // tm), N_GROUPS - 1)


def _rms(x):
    return x * lax.rsqrt(jnp.mean(x * x, axis=-1, keepdims=True) + NORM_EPS)


def _ada_kernel(c_ref, w_ref, b_ref, o_ref):
    @pl.when(pl.program_id(1) == 0)
    def _():
        o_ref[...] = jnp.broadcast_to(b_ref[...], o_ref.shape)

    c = c_ref[...]
    a = (c * jax.nn.sigmoid(c)).astype(BF16)
    o_ref[...] += jnp.dot(a, w_ref[...].astype(BF16), preferred_element_type=F32)


def _ada_mods(cvec, ada_w, ada_b):
    kb = LANES
    n = N_MOD * D_MODEL
    return pl.pallas_call(
        _ada_kernel,
        grid=(DEPTH, D_MODEL // kb),
        in_specs=[
            pl.BlockSpec((8, kb), lambda l, k: (0, k)),
            pl.BlockSpec((None, kb, n), lambda l, k: (l, k, 0)),
            pl.BlockSpec((None, 1, n), lambda l, k: (l, 0, 0)),
        ],
        out_specs=pl.BlockSpec((None, 8, n), lambda l, k: (l, 0, 0)),
        out_shape=jax.ShapeDtypeStruct((DEPTH, 8, n), F32),
        compiler_params=_params(dimension_semantics=("arbitrary", "arbitrary")),
        name="ada_mods",
    )(cvec, ada_w, ada_b.reshape(DEPTH, 1, n))


def _pre(x, p_ref):
    return (_rms(x) * p_ref[P_PRE_G:P_PRE_G + 1, :]) * (1.0 + p_ref[P_SCALE:P_SCALE + 1, :]) \
        + p_ref[P_SHIFT:P_SHIFT + 1, :]


def _prenorm_kernel(x_ref, p_ref, h_ref):
    h_ref[...] = _pre(x_ref[...], p_ref).astype(BF16)


def _prenorm(x, pp, s, tm=512):
    rows = x.shape[0]
    return pl.pallas_call(
        _prenorm_kernel,
        grid=(rows // tm,),
        in_specs=[
            pl.BlockSpec((tm, D_MODEL), lambda i: (i, 0)),
            pl.BlockSpec((None, None, 8, D_MODEL), lambda i: (s, _group_of_block(i, tm), 0, 0)),
        ],
        out_specs=pl.BlockSpec((tm, D_MODEL), lambda i: (i, 0)),
        out_shape=jax.ShapeDtypeStruct((rows, D_MODEL), BF16),
        compiler_params=_params(dimension_semantics=("arbitrary",)),
        name="prenorm",
    )(x, pp)


class _Job(NamedTuple):
    src: jax.Array
    in_spec: pl.BlockSpec
    out_spec: pl.BlockSpec
    out_shape: jax.ShapeDtypeStruct
    body: Callable
    n_blocks: int


def _row_block(rows, rows_out, steps):
    g = math.gcd(rows, rows_out)
    for rb in range(BF16_SUBLANES, g + 1, BF16_SUBLANES):
        if g % rb == 0 and rows_out // rb <= steps:
            return rb
    raise ValueError(f"no row block for {rows}->{rows_out} rows in {steps} steps")


def _cast_job(src, lead, rows_out=None):
    rows, cols = src.shape[-2:]
    rows_out = rows_out or rows

    def make(steps, lin):
        rb = _row_block(rows, rows_out, steps)
        nb_in, nb_out = rows // rb, rows_out // rb

        def body(src_ref, dst_ref, blk):
            @pl.when(blk < nb_in)
            def _():
                dst_ref[...] = src_ref[...].astype(BF16)

            if nb_out > nb_in:
                @pl.when(blk >= nb_in)
                def _():
                    dst_ref[...] = jnp.zeros((rb, cols), BF16)

        return _Job(
            src,
            pl.BlockSpec((None,) * len(lead) + (rb, cols),
                         lambda *g: lead + (jnp.minimum(lin(*g), nb_in - 1), 0)),
            pl.BlockSpec((rb, cols), lambda *g: (jnp.minimum(lin(*g), nb_out - 1), 0)),
            jax.ShapeDtypeStruct((rows_out, cols), BF16), body, nb_out)

    return make


def _ffn_split_job(ffn_w_in, lead):
    def make(steps, lin):
        rb = _row_block(D_MODEL, D_MODEL, steps)
        nb = D_MODEL // rb

        def body(src_ref, dst_ref, blk):
            del blk
            for part in range(2):
                dst_ref[part, :, :D_FF] = src_ref[:, part * D_FF:(part + 1) * D_FF].astype(BF16)
                dst_ref[part, :, D_FF:] = jnp.zeros((rb, D_FF_PAD - D_FF), BF16)

        return _Job(
            ffn_w_in,
            pl.BlockSpec((None,) * len(lead) + (rb, 2 * D_FF), lambda *g: lead + (jnp.minimum(lin(*g), nb - 1), 0)),
            pl.BlockSpec((2, rb, D_FF_PAD), lambda *g: (0, jnp.minimum(lin(*g), nb - 1), 0)),
            jax.ShapeDtypeStruct((2, D_MODEL, D_FF_PAD), BF16), body, nb)

    return make


def _hosted_call(main, main_args, main_in_specs, out_spec, out_shape, grid, job_makers, name):
    steps = math.prod(grid)
    strides = [math.prod(grid[d + 1:]) for d in range(len(grid))]
    lin = lambda *g: sum(gi * st for gi, st in zip(g, strides))
    jobs = [mk(steps, lin) for mk in job_makers]
    n_in, n_jobs = len(main_args), len(jobs)

    def kern(*refs):
        if main is not None:
            main(*refs[:n_in], refs[n_in + n_jobs])
        t = lin(*[pl.program_id(d) for d in range(len(grid))])
        n_main_out = 0 if main is None else 1
        for q, jb in enumerate(jobs):
            jb.body(refs[n_in + q], refs[n_in + n_jobs + n_main_out + q], jnp.minimum(t, jb.n_blocks - 1))

    main_out = [] if main is None else [(out_spec, out_shape)]
    res = pl.pallas_call(
        kern,
        grid=grid,
        in_specs=list(main_in_specs) + [jb.in_spec for jb in jobs],
        out_specs=[s for s, _ in main_out] + [jb.out_spec for jb in jobs],
        out_shape=[s for _, s in main_out] + [jb.out_shape for jb in jobs],
        compiler_params=_params(dimension_semantics=("arbitrary",) * len(grid)),
        name=name,
    )(*main_args, *[jb.src for jb in jobs])
    return (None, list(res)) if main is None else (res[0], list(res[1:]))


def _convert(job_maker, steps=32):
    return _hosted_call(None, (), (), None, None, (steps,), [job_maker], "convert")[1][0]


def _swiglu_kernel(h_ref, wg_ref, wu_ref, o_ref):
    h = h_ref[...]
    g = jnp.dot(h, wg_ref[...], preferred_element_type=F32)
    u = jnp.dot(h, wu_ref[...], preferred_element_type=F32)
    o_ref[...] = (g * jax.nn.sigmoid(g) * u).astype(BF16)


def _ffn_in(h, wgu, jobs, tm=512, tn=FF_TILE):
    rows = h.shape[0]
    return _hosted_call(
        _swiglu_kernel, (h, wgu, wgu),
        [pl.BlockSpec((tm, D_MODEL), lambda j, i: (i, 0)),
         pl.BlockSpec((None, D_MODEL, tn), lambda j, i: (0, 0, j)),
         pl.BlockSpec((None, D_MODEL, tn), lambda j, i: (1, 0, j))],
        pl.BlockSpec((tm, tn), lambda j, i: (i, j)),
        jax.ShapeDtypeStruct((rows, D_FF_PAD), BF16),
        (D_FF_PAD // tn, rows // tm), jobs, "ffn_in")


def _gelu_kernel(h_ref, w_ref, o_ref):
    y = jnp.dot(h_ref[...], w_ref[...], preferred_element_type=F32)
    o_ref[...] = jax.nn.gelu(y).astype(BF16)


def _gmlp_in(h, w, jobs, tm=512, tn=1024):
    rows = h.shape[0]
    n = w.shape[1]
    return _hosted_call(
        _gelu_kernel, (h, w),
        [pl.BlockSpec((tm, D_MODEL), lambda j, i: (i, 0)),
         pl.BlockSpec((D_MODEL, tn), lambda j, i: (0, j))],
        pl.BlockSpec((tm, tn), lambda j, i: (i, j)),
        jax.ShapeDtypeStruct((rows, n), BF16),
        (n // tn, rows // tm), jobs, "gmlp_in")


def _qkv_rope_kernel(h_ref, w_ref, cos_ref, sin_ref, o_ref, *, n_rope_tiles, tm, tn):
    j = pl.program_id(0)
    y = jnp.dot(h_ref[...], w_ref[...], preferred_element_type=F32)

    @pl.when(j < n_rope_tiles)
    def _():
        c = cos_ref[...]
        s = sin_ref[...]
        lane = lax.broadcasted_iota(jnp.int32, (tm, LANES), 1)
        even = (lane & 1) == 0
        for q in range(tn // LANES):
            xh = y[:, q * LANES:(q + 1) * LANES]
            partner = jnp.where(even, pltpu.roll(xh, LANES - 1, 1), pltpu.roll(xh, 1, 1))
            o_ref[:, q * LANES:(q + 1) * LANES] = (xh * c + partner * s).astype(BF16)

    @pl.when(j >= n_rope_tiles)
    def _():
        o_ref[...] = y.astype(BF16)


def _qkv_rope(h, w, cos_t, sin_t, rope_width, jobs, tm=512, tn=512):
    rows = h.shape[0]
    n = w.shape[1]
    kern = functools.partial(_qkv_rope_kernel, n_rope_tiles=rope_width // tn, tm=tm, tn=tn)
    return _hosted_call(
        kern, (h, w, cos_t, sin_t),
        [pl.BlockSpec((tm, D_MODEL), lambda j, i: (i, 0)),
         pl.BlockSpec((D_MODEL, tn), lambda j, i: (0, j)),
         pl.BlockSpec((tm, LANES), lambda j, i: (i, 0)),
         pl.BlockSpec((tm, LANES), lambda j, i: (i, 0))],
        pl.BlockSpec((tm, tn), lambda j, i: (i, j)),
        jax.ShapeDtypeStruct((rows, n), BF16),
        (n // tn, rows // tm), jobs, "qkv_rope")


def _mm_out_kernel(a_ref, w_ref, x_ref, p_ref, xo_ref, *maybe_h_ref, coef):
    y = jnp.dot(a_ref[...], w_ref[...], preferred_element_type=F32)
    r = _rms(y) * p_ref[P_POST_G:P_POST_G + 1, :]
    xn = x_ref[...] + coef * (p_ref[P_GATE:P_GATE + 1, :] * r)
    xo_ref[...] = xn
    if maybe_h_ref:
        maybe_h_ref[0][...] = _pre(xn, p_ref).astype(BF16)


def _mm_out(a, w, x, pp, s, *, rows, coef, want_h):
    k = a.shape[1]
    tm = 512 if k <= D_MODEL else 256
    out_shape = [jax.ShapeDtypeStruct((rows, D_MODEL), F32)]
    out_specs = [pl.BlockSpec((tm, D_MODEL), lambda i: (i, 0))]
    if want_h:
        out_shape.append(jax.ShapeDtypeStruct((rows, D_MODEL), BF16))
        out_specs.append(pl.BlockSpec((tm, D_MODEL), lambda i: (i, 0)))
    res = pl.pallas_call(
        functools.partial(_mm_out_kernel, coef=coef),
        grid=(rows // tm,),
        in_specs=[
            pl.BlockSpec((tm, k), lambda i: (i, 0)),
            pl.BlockSpec((k, D_MODEL), lambda i: (0, 0), pipeline_mode=pl.Buffered(1)),
            pl.BlockSpec((tm, D_MODEL), lambda i: (i, 0)),
            pl.BlockSpec((None, None, 8, D_MODEL), lambda i: (s, _group_of_block(i, tm), 0, 0)),
        ],
        out_specs=out_specs,
        out_shape=out_shape,
        compiler_params=_params(dimension_semantics=("arbitrary",)),
        name="mm_out",
    )(a, w, x, pp)
    return (res[0], res[1]) if want_h else (res[0], None)


def _dot_t(a, b):
    return lax.dot_general(a, b, (((1,), (1,)), ((), ())), preferred_element_type=F32)


def _win_attn_kernel(sink_ref, q_ref, kl_ref, km_ref, kr_ref, vl_ref, vm_ref, vr_ref, kc_ref, vc_ref, o_ref):
    n = pl.program_id(1)
    n_lat = SEQ // A_BLOCK
    scale = HEAD_DIM ** -0.5
    gq = A_GROUP * A_BLOCK
    row = lax.broadcasted_iota(jnp.int32, (gq, 1), 0)
    qi = lax.broadcasted_iota(jnp.int32, (gq, A_BLOCK), 0) & (A_BLOCK - 1)
    mi = lax.broadcasted_iota(jnp.int32, (gq, A_BLOCK), 1)

    def heads(kk):
        q4 = jnp.concatenate(
            [q_ref[:, (kk * A_GROUP + g) * HEAD_DIM:(kk * A_GROUP + g + 1) * HEAD_DIM] for g in range(A_GROUP)],
            axis=0)
        sink = jnp.full((gq, 1), sink_ref[kk * A_GROUP], F32)
        for g in range(1, A_GROUP):
            sink = jnp.where(row >= g * A_BLOCK, sink_ref[kk * A_GROUP + g], sink)
        return q4, sink, slice(kk * HEAD_DIM, (kk + 1) * HEAD_DIM)

    def store(kk, o):
        o = o.astype(BF16)
        for g in range(A_GROUP):
            h = kk * A_GROUP + g
            o_ref[:, h * HEAD_DIM:(h + 1) * HEAD_DIM] = o[g * A_BLOCK:(g + 1) * A_BLOCK, :]

    @pl.when(n < n_lat)
    def _():
        off_l = jnp.where(n > 0, 0, 2 * A_BLOCK)
        off_r = jnp.where(n < n_lat - 1, 0, 2 * A_BLOCK)
        ok_l = mi >= qi + off_l
        ok_r = mi <= qi - off_r
        for kk in range(A_KV_HEADS):
            q4, sink, cs = heads(kk)
            s_c = _dot_t(q4, kc_ref[:, cs]) * scale
            s_l = jnp.where(ok_l, _dot_t(q4, kl_ref[:, cs]) * scale, NEG_INF)
            s_m = _dot_t(q4, km_ref[:, cs]) * scale
            s_r = jnp.where(ok_r, _dot_t(q4, kr_ref[:, cs]) * scale, NEG_INF)
            m_lanes = jnp.maximum(jnp.maximum(s_c[:, :A_BLOCK], s_c[:, A_BLOCK:]),
                                  jnp.maximum(jnp.maximum(s_l, s_m), s_r))
            m = jnp.maximum(jnp.max(m_lanes, axis=-1, keepdims=True), sink)
            p_c = jnp.exp(s_c - m)
            p_l = jnp.exp(s_l - m)
            p_m = jnp.exp(s_m - m)
            p_r = jnp.exp(s_r - m)
            p_lanes = (p_c[:, :A_BLOCK] + p_c[:, A_BLOCK:]) + (p_l + p_m + p_r)
            den = jnp.exp(sink - m) + jnp.sum(p_lanes, axis=-1, keepdims=True)
            o = (jnp.dot(p_c.astype(BF16), vc_ref[:, cs], preferred_element_type=F32)
                 + jnp.dot(p_l.astype(BF16), vl_ref[:, cs], preferred_element_type=F32)
                 + jnp.dot(p_m.astype(BF16), vm_ref[:, cs], preferred_element_type=F32)
                 + jnp.dot(p_r.astype(BF16), vr_ref[:, cs], preferred_element_type=F32))
            store(kk, o / den)

    @pl.when(n >= n_lat)
    def _():
        for kk in range(A_KV_HEADS):
            q4, sink, cs = heads(kk)
            s_c = _dot_t(q4, kc_ref[:, cs]) * scale
            m = jnp.maximum(jnp.max(s_c, axis=-1, keepdims=True), sink)
            p_c = jnp.exp(s_c - m)
            den = jnp.exp(sink - m) + jnp.sum(p_c, axis=-1, keepdims=True)
            o = jnp.dot(p_c.astype(BF16), vc_ref[:, cs], preferred_element_type=F32)
            store(kk, o / den)


def _win_attn(qkv, sink, with_ctx):
    n_lat = SEQ // A_BLOCK
    n_ctx = CTX_LEN // A_BLOCK
    steps = n_lat + (n_ctx if with_ctx else 0)
    kvw = A_KV_HEADS * HEAD_DIM
    kcol = A_HEADS * HEAD_DIM // kvw
    ctx_blk0 = LAT_ROWS // A_BLOCK

    def qrow(b, n):
        return jnp.where(n < n_lat, b * n_lat + n, ctx_blk0 + b * n_ctx + (n - n_lat))

    def nb(b, n, d):
        return b * n_lat + jnp.clip(n + d, 0, n_lat - 1)

    kv_spec = lambda d, c: pl.BlockSpec((A_BLOCK, kvw), lambda b, n: (nb(b, n, d), c))
    ctx_spec = lambda c: pl.BlockSpec((CTX_LEN, kvw), lambda b, n: (LAT_ROWS // CTX_LEN + b, c))
    rows = LAT_ROWS + (CTX_ROWS if with_ctx else 0)
    return pl.pallas_call(
        _win_attn_kernel,
        grid=(BATCH, steps),
        in_specs=[
            pl.BlockSpec(memory_space=pltpu.SMEM),
            pl.BlockSpec((A_BLOCK, A_HEADS * HEAD_DIM), lambda b, n: (qrow(b, n), 0)),
            kv_spec(-1, kcol), kv_spec(0, kcol), kv_spec(1, kcol),
            kv_spec(-1, kcol + 1), kv_spec(0, kcol + 1), kv_spec(1, kcol + 1),
            ctx_spec(kcol), ctx_spec(kcol + 1),
        ],
        out_specs=pl.BlockSpec((A_BLOCK, D_MODEL), lambda b, n: (qrow(b, n), 0)),
        out_shape=jax.ShapeDtypeStruct((rows, D_MODEL), BF16),
        compiler_params=_params(dimension_semantics=("arbitrary", "arbitrary")),
        name="win_attn",
    )(sink, qkv, qkv, qkv, qkv, qkv, qkv, qkv, qkv, qkv)


def _gmlp_mid_kernel(u_ref, v_ref, g_ref, b_ref, ws_ref, bs_ref, o_ref, *, tm):
    v = v_ref[...].astype(F32)
    mu = jnp.mean(v, axis=-1, keepdims=True)
    vc = v - mu
    vn = vc * lax.rsqrt(jnp.mean(vc * vc, axis=-1, keepdims=True) + NORM_EPS)
    vn = (vn * g_ref[...] + b_ref[...]).astype(BF16)
    for g in range(B_GROUPS):
        cs = slice(g * B_GROUP_W, (g + 1) * B_GROUP_W)
        ws = ws_ref[g].astype(BF16)
        for c in range(tm // B_CHUNK):
            rs = slice(c * B_CHUNK, (c + 1) * B_CHUNK)
            mixed = jnp.dot(ws, vn[rs, cs], preferred_element_type=F32) + bs_ref[:, g:g + 1]
            o_ref[rs, cs] = (u_ref[rs, cs].astype(F32) * mixed).astype(BF16)


def _gmlp_mid(z, vn_g, vn_b, ws, bs_t, tm=256):
    rows = z.shape[0]
    return pl.pallas_call(
        functools.partial(_gmlp_mid_kernel, tm=tm),
        grid=(rows // tm,),
        in_specs=[
            pl.BlockSpec((tm, B_WIDTH), lambda i: (i, 0)),
            pl.BlockSpec((tm, B_WIDTH), lambda i: (i, 1)),
            pl.BlockSpec((1, B_WIDTH), lambda i: (0, 0)),
            pl.BlockSpec((1, B_WIDTH), lambda i: (0, 0)),
            pl.BlockSpec((B_GROUPS, B_CHUNK, B_CHUNK), lambda i: (0, 0, 0)),
            pl.BlockSpec((B_CHUNK, B_GROUPS), lambda i: (0, 0)),
        ],
        out_specs=pl.BlockSpec((tm, B_WIDTH), lambda i: (i, 0)),
        out_shape=jax.ShapeDtypeStruct((rows, B_WIDTH), BF16),
        compiler_params=_params(dimension_semantics=("arbitrary",)),
        name="gmlp_mid",
    )(z, z, vn_g, vn_b, ws, bs_t)


def _diff_attn_kernel(lq1_ref, lk1_ref, lq2_ref, lk2_ref, sg_ref, q_ref, k_ref, v_ref, kc_ref, vc_ref, o_ref,
                      *, lam_init, tq, tk):
    qi = pl.program_id(2)
    n_lat = SEQ // tq
    scale = HEAD_DIM ** -0.5
    lam = (jnp.exp(jnp.sum(lq1_ref[...] * lk1_ref[...], axis=-1, keepdims=True))
           - jnp.exp(jnp.sum(lq2_ref[...] * lk2_ref[...], axis=-1, keepdims=True)) + lam_init)

    def run(chunks):
        outs = []
        for c in range(2):
            q = q_ref[:, c * HEAD_DIM:(c + 1) * HEAD_DIM]
            m = l = acc = None
            for kref, vref, lo, size in chunks:
                s = _dot_t(q, kref[lo:lo + size, c * HEAD_DIM:(c + 1) * HEAD_DIM]) * scale
                m_new = jnp.max(s, axis=-1, keepdims=True)
                if m is not None:
                    m_new = jnp.maximum(m, m_new)
                p = jnp.exp(s - m_new)
                pv = jnp.dot(p.astype(BF16), vref[lo:lo + size, :], preferred_element_type=F32)
                if m is None:
                    l = jnp.sum(p, axis=-1, keepdims=True)
                    acc = pv
                else:
                    alpha = jnp.exp(m - m_new)
                    l = alpha * l + jnp.sum(p, axis=-1, keepdims=True)
                    acc = alpha * acc + pv
                m = m_new
            outs.append(acc / l)
        o = outs[0] - lam * outs[1]
        o = _rms(o) * sg_ref[...] * (1.0 - lam_init)
        o_ref[...] = o.astype(BF16)

    @pl.when(qi < n_lat)
    def _():
        run([(kc_ref, vc_ref, 0, CTX_LEN)] + [(k_ref, v_ref, t * tk, tk) for t in range(SEQ // tk)])

    @pl.when(qi >= n_lat)
    def _():
        run([(kc_ref, vc_ref, 0, CTX_LEN)])


def _diff_attn(qkv, lq1, lk1, lq2, lk2, subln_g, lam_init, tq=256, tk=512):
    hw = 2 * HEAD_DIM
    n_lat = SEQ // tq
    n_ctx = CTX_LEN // tq
    kcol = D_MODEL // hw
    ctx_blk0 = LAT_ROWS // tq

    def qrow(b, qi):
        return jnp.where(qi < n_lat, b * n_lat + qi, ctx_blk0 + b * n_ctx + (qi - n_lat))

    vec = pl.BlockSpec((1, HEAD_DIM), lambda b, h, qi: (0, 0))
    kern = functools.partial(_diff_attn_kernel, lam_init=lam_init, tq=tq, tk=tk)
    return pl.pallas_call(
        kern,
        grid=(BATCH, C_HEADS, n_lat + n_ctx),
        in_specs=[
            vec, vec, vec, vec,
            pl.BlockSpec((1, hw), lambda b, h, qi: (0, 0)),
            pl.BlockSpec((tq, hw), lambda b, h, qi: (qrow(b, qi), h)),
            pl.BlockSpec((SEQ, hw), lambda b, h, qi: (b, kcol + h)),
            pl.BlockSpec((SEQ, hw), lambda b, h, qi: (b, 2 * kcol + h)),
            pl.BlockSpec((CTX_LEN, hw), lambda b, h, qi: (LAT_ROWS // CTX_LEN + b, kcol + h)),
            pl.BlockSpec((CTX_LEN, hw), lambda b, h, qi: (LAT_ROWS // CTX_LEN + b, 2 * kcol + h)),
        ],
        out_specs=pl.BlockSpec((tq, hw), lambda b, h, qi: (qrow(b, qi), h)),
        out_shape=jax.ShapeDtypeStruct((ROWS, D_MODEL), BF16),
        compiler_params=_params(dimension_semantics=("arbitrary", "arbitrary", "arbitrary")),
        name="diff_attn",
    )(lq1, lk1, lq2, lk2, subln_g, qkv, qkv, qkv, qkv, qkv)


def _rope_tables():
    rows = SEQ // GRID_W
    row = jnp.repeat(jnp.arange(rows, dtype=F32), GRID_W)
    col = jnp.tile(jnp.arange(GRID_W, dtype=F32), rows)
    axis_dim = HEAD_DIM // 2
    inv = ROPE_THETA ** (-jnp.arange(0, axis_dim, 2, dtype=F32) / axis_dim)
    ang = jnp.concatenate([row[:, None] * inv, col[:, None] * inv], axis=-1)
    cos = jnp.repeat(jnp.cos(ang), 2, axis=-1)
    sin = jnp.repeat(jnp.sin(ang), 2, axis=-1) * jnp.tile(jnp.array([-1.0, 1.0], F32), HEAD_DIM // 2)
    cos_t = jnp.concatenate([jnp.tile(cos, (BATCH, 1)), jnp.ones((CTX_ROWS, HEAD_DIM), F32)], axis=0)
    sin_t = jnp.concatenate([jnp.tile(sin, (BATCH, 1)), jnp.zeros((CTX_ROWS, HEAD_DIM), F32)], axis=0)
    return cos_t, sin_t


def _param_tiles(mods, norm_g):
    m = mods[:, :N_GROUPS].reshape(DEPTH, N_GROUPS, N_MOD, D_MODEL)
    zero = jnp.zeros((N_GROUPS, D_MODEL), F32)
    bcast = lambda v: jnp.broadcast_to(v, (N_GROUPS, D_MODEL))
    tiles = []
    for s in range(-1, 3 * DEPTH):
        rows = [zero] * 8
        if s >= 0:
            i, slot = divmod(s, 3)
            rows[P_POST_G] = bcast(norm_g[i, 2 * slot + 1])
            rows[P_GATE] = m[i, :, 3 * slot + 2]
        if s + 1 < 3 * DEPTH:
            i, slot = divmod(s + 1, 3)
            rows[P_PRE_G] = bcast(norm_g[i, 2 * slot])
            rows[P_SCALE] = m[i, :, 3 * slot + 1]
            rows[P_SHIFT] = m[i, :, 3 * slot]
        tiles.append(jnp.stack(rows, axis=1))
    return jnp.stack(tiles, axis=0)


def kernel(x, c, ctx, c_ctx, ada_w, ada_b, norm_g, ffn_w_in, ffn_w_out, a_w_in, a_w_out, a_sink, b_w_in, b_vnorm_g,
           b_vnorm_b, b_ws, b_bs, b_w_out, c_w_in, c_w_out, c_lq1, c_lk1, c_lq2, c_lk2, c_subln_g):
    xs = jnp.concatenate([x.reshape(LAT_ROWS, D_MODEL), ctx.reshape(CTX_ROWS, D_MODEL)], axis=0)
    cvec = jnp.concatenate([c, c_ctx[None, :], jnp.zeros((8 - BATCH - 1, D_MODEL), F32)], axis=0)
    pp = _param_tiles(_ada_mods(cvec, ada_w, ada_b), norm_g)
    cos_t, sin_t = _rope_tables()

    mixer_w = ((a_w_in, a_w_out), (b_w_in, b_w_out), (c_w_in, c_w_out))
    wgu = _convert(_ffn_split_job(ffn_w_in, (0, 0)))
    h = _prenorm(xs, pp, 0)
    for i in range(DEPTH):
        kind, j = i % N_MIXERS, i // N_MIXERS
        ctx_live = i < DEPTH - 1
        rows_out = ROWS if ctx_live else LAT_ROWS
        s0 = 3 * i + 1
        w_mix_in, w_mix_out = mixer_w[kind]

        a, (wdn, w_i) = _ffn_in(h, wgu, [_cast_job(ffn_w_out, (i, 0), D_FF_PAD), _cast_job(w_mix_in, (j,))])
        xs, h = _mm_out(a, wdn, xs, pp, s0, rows=ROWS, coef=0.5, want_h=True)

        jobs = [_cast_job(w_mix_out, (j,)), _ffn_split_job(ffn_w_in, (i, 1))]
        if kind == 0:
            qkv, (w_o, wgu) = _qkv_rope(h, w_i, cos_t, sin_t, (A_HEADS + A_KV_HEADS) * HEAD_DIM, jobs)
            o = _win_attn(qkv, a_sink[j], ctx_live)
        elif kind == 1:
            z, (w_o, wgu) = _gmlp_in(h, w_i, jobs)
            o = _gmlp_mid(z, b_vnorm_g[j][None, :], b_vnorm_b[j][None, :], b_ws[j], b_bs[j].T)
        else:
            lam_init = 0.8 - 0.6 * math.exp(-0.3 * i)
            qkv, (w_o, wgu) = _qkv_rope(h, w_i, cos_t, sin_t, 2 * D_MODEL, jobs)
            o = _diff_attn(qkv, c_lq1[j][None, :], c_lk1[j][None, :], c_lq2[j][None, :], c_lk2[j][None, :],
                           c_subln_g[j][None, :], lam_init)
        xs, h = _mm_out(o, w_o, xs, pp, s0 + 1, rows=rows_out, coef=1.0, want_h=True)

        jobs = [_cast_job(ffn_w_out, (i, 1), D_FF_PAD)]
        if i + 1 < DEPTH:
            jobs.append(_ffn_split_job(ffn_w_in, (i + 1, 0)))
        a, conv = _ffn_in(h, wgu, jobs)
        xs, h = _mm_out(a, conv[0], xs, pp, s0 + 2, rows=rows_out, coef=0.5, want_h=i + 1 < DEPTH)
        wgu = conv[1] if i + 1 < DEPTH else None
    return xs.reshape(BATCH, SEQ, D_MODEL)
```

```python
import functools
import math
from typing import Callable, NamedTuple

import jax
import jax.numpy as jnp
from jax import lax
from jax.experimental import pallas as pl
from jax.experimental.pallas import tpu as pltpu

D_MODEL = 2048
BATCH = 2
SEQ = 4096
DEPTH = 4
GRID_W = 64
CTX_LEN = 256
N_MIXERS = 3
N_MOD = 9
NORM_EPS = 1e-6
ROPE_THETA = 10000.0
NEG_INF = -1e30
D_FF = 5504
HEAD_DIM = 128
A_HEADS = 16
A_KV_HEADS = 4
A_GROUP = 4
A_BLOCK = 128
B_CHUNK = 128
B_WIDTH = 3 * D_MODEL
B_GROUPS = 8
B_GROUP_W = B_WIDTH // B_GROUPS
C_HEADS = 8

LAT_ROWS = BATCH * SEQ
CTX_ROWS = BATCH * CTX_LEN
ROWS = LAT_ROWS + CTX_ROWS
N_GROUPS = 3
LANES = 128
BF16_SUBLANES = 16
MXU_WIDTH = 256
IN_ROW_TILES = 8
FF_TILE = 512
D_FF_PAD = -(-D_FF // FF_TILE) * FF_TILE
VMEM_LIMIT = 56 * 1024 * 1024

BF16 = jnp.bfloat16
F32 = jnp.float32

P_POST_G, P_GATE, P_PRE_G, P_SCALE, P_SHIFT = 0, 1, 2, 3, 4


def _params(**kw):
    return pltpu.CompilerParams(vmem_limit_bytes=VMEM_LIMIT, **kw)


def _group_of_block(i, tm):
    return jnp.minimum(i // (SEQ // tm), N_GROUPS - 1)


def _rms(x):
    return x * lax.rsqrt(jnp.mean(x * x, axis=-1, keepdims=True) + NORM_EPS)


def _ada_kernel(c_ref, w_ref, b_ref, o_ref):
    @pl.when(pl.program_id(1) == 0)
    def _():
        o_ref[...] = jnp.broadcast_to(b_ref[...], o_ref.shape)

    c = c_ref[...]
    a = (c * jax.nn.sigmoid(c)).astype(BF16)
    o_ref[...] += jnp.dot(a, w_ref[...].astype(BF16), preferred_element_type=F32)


def _ada_mods(cvec, ada_w, ada_b):
    kb = LANES
    n = N_MOD * D_MODEL
    return pl.pallas_call(
        _ada_kernel,
        grid=(DEPTH, D_MODEL // kb),
        in_specs=[
            pl.BlockSpec((8, kb), lambda l, k: (0, k)),
            pl.BlockSpec((None, kb, n), lambda l, k: (l, k, 0)),
            pl.BlockSpec((None, 1, n), lambda l, k: (l, 0, 0)),
        ],
        out_specs=pl.BlockSpec((None, 8, n), lambda l, k: (l, 0, 0)),
        out_shape=jax.ShapeDtypeStruct((DEPTH, 8, n), F32),
        compiler_params=_params(dimension_semantics=("arbitrary", "arbitrary")),
        name="ada_mods",
    )(cvec, ada_w, ada_b.reshape(DEPTH, 1, n))


def _pre(x, p_ref):
    gain = p_ref[P_PRE_G:P_PRE_G + 1, :] * (1.0 + p_ref[P_SCALE:P_SCALE + 1, :])
    return _rms(x) * gain + p_ref[P_SHIFT:P_SHIFT + 1, :]


def _prenorm_kernel(x_ref, p_ref, h_ref):
    h_ref[...] = _pre(x_ref[...], p_ref).astype(BF16)


def _prenorm(x, pp, s, tm=512):
    rows = x.shape[0]
    return pl.pallas_call(
        _prenorm_kernel,
        grid=(rows // tm,),
        in_specs=[
            pl.BlockSpec((tm, D_MODEL), lambda i: (i, 0)),
            pl.BlockSpec((None, None, 8, D_MODEL), lambda i: (s, _group_of_block(i, tm), 0, 0)),
        ],
        out_specs=pl.BlockSpec((tm, D_MODEL), lambda i: (i, 0)),
        out_shape=jax.ShapeDtypeStruct((rows, D_MODEL), BF16),
        compiler_params=_params(dimension_semantics=("arbitrary",)),
        name="prenorm",
    )(x, pp)


class _Job(NamedTuple):
    src: jax.Array
    in_spec: pl.BlockSpec
    out_spec: pl.BlockSpec
    out_shape: jax.ShapeDtypeStruct
    body: Callable
    n_blocks: int


def _row_block(rows, rows_out, steps):
    g = math.gcd(rows, rows_out)
    for rb in range(BF16_SUBLANES, g + 1, BF16_SUBLANES):
        if g % rb == 0 and rows_out // rb <= steps:
            return rb
    raise ValueError(f"no row block for {rows}->{rows_out} rows in {steps} steps")


def _cast_job(src, lead, rows_out=None):
    rows, cols = src.shape[-2:]
    rows_out = rows_out or rows

    def make(steps, lin):
        rb = _row_block(rows, rows_out, steps)
        nb_in, nb_out = rows // rb, rows_out // rb

        def body(src_ref, dst_ref, blk):
            v = src_ref[...].astype(BF16)
            if nb_out > nb_in:
                v = jnp.where(blk < nb_in, v, jnp.zeros_like(v))
            dst_ref[...] = v

        return _Job(
            src,
            pl.BlockSpec((None,) * len(lead) + (rb, cols),
                         lambda *g: lead + (jnp.minimum(lin(*g), nb_in - 1), 0)),
            pl.BlockSpec((rb, cols), lambda *g: (jnp.minimum(lin(*g), nb_out - 1), 0)),
            jax.ShapeDtypeStruct((rows_out, cols), BF16), body, nb_out)

    return make


def _ffn_split_job(ffn_w_in, lead):
    def make(steps, lin):
        rb = _row_block(D_MODEL, D_MODEL, steps)
        nb = D_MODEL // rb

        def body(src_ref, dst_ref, blk):
            del blk
            for part in range(2):
                dst_ref[part, :, :D_FF] = src_ref[:, part * D_FF:(part + 1) * D_FF].astype(BF16)
                dst_ref[part, :, D_FF:] = jnp.zeros((rb, D_FF_PAD - D_FF), BF16)

        return _Job(
            ffn_w_in,
            pl.BlockSpec((None,) * len(lead) + (rb, 2 * D_FF), lambda *g: lead + (jnp.minimum(lin(*g), nb - 1), 0)),
            pl.BlockSpec((2, rb, D_FF_PAD), lambda *g: (0, jnp.minimum(lin(*g), nb - 1), 0)),
            jax.ShapeDtypeStruct((2, D_MODEL, D_FF_PAD), BF16), body, nb)

    return make


def _hosted_call(main, main_args, main_in_specs, out_spec, out_shape, grid, job_makers, name):
    steps = math.prod(grid)
    strides = [math.prod(grid[d + 1:]) for d in range(len(grid))]
    lin = lambda *g: sum(gi * st for gi, st in zip(g, strides))
    jobs = [mk(steps, lin) for mk in job_makers]
    n_in, n_jobs = len(main_args), len(jobs)

    def kern(*refs):
        if main is not None:
            main(*refs[:n_in], refs[n_in + n_jobs])
        t = lin(*[pl.program_id(d) for d in range(len(grid))])
        n_main_out = 0 if main is None else 1
        for q, jb in enumerate(jobs):
            jb.body(refs[n_in + q], refs[n_in + n_jobs + n_main_out + q], jnp.minimum(t, jb.n_blocks - 1))

    main_out = [] if main is None else [(out_spec, out_shape)]
    res = pl.pallas_call(
        kern,
        grid=grid,
        in_specs=list(main_in_specs) + [jb.in_spec for jb in jobs],
        out_specs=[s for s, _ in main_out] + [jb.out_spec for jb in jobs],
        out_shape=[s for _, s in main_out] + [jb.out_shape for jb in jobs],
        compiler_params=_params(dimension_semantics=("arbitrary",) * len(grid)),
        name=name,
    )(*main_args, *[jb.src for jb in jobs])
    return (None, list(res)) if main is None else (res[0], list(res[1:]))


def _convert(job_maker, steps=32):
    return _hosted_call(None, (), (), None, None, (steps,), [job_maker], "convert")[1][0]


def _swiglu_kernel(h_ref, wg_ref, wu_ref, o_ref):
    h = h_ref[...]
    g = jnp.dot(h, wg_ref[...], preferred_element_type=F32)
    u = jnp.dot(h, wu_ref[...], preferred_element_type=F32)
    o_ref[...] = (g * jax.nn.sigmoid(g) * u).astype(BF16)


def _ffn_in(h, wgu, jobs, tn=FF_TILE):
    rows = h.shape[0]
    tm = rows // IN_ROW_TILES
    return _hosted_call(
        _swiglu_kernel, (h, wgu, wgu),
        [pl.BlockSpec((tm, D_MODEL), lambda j, i: (i, 0)),
         pl.BlockSpec((None, D_MODEL, tn), lambda j, i: (0, 0, j)),
         pl.BlockSpec((None, D_MODEL, tn), lambda j, i: (1, 0, j))],
        pl.BlockSpec((tm, tn), lambda j, i: (i, j)),
        jax.ShapeDtypeStruct((rows, D_FF_PAD), BF16),
        (D_FF_PAD // tn, rows // tm), jobs, "ffn_in")


def _gelu_kernel(h_ref, w_ref, o_ref):
    y = jnp.dot(h_ref[...], w_ref[...], preferred_element_type=F32)
    o_ref[...] = jax.nn.gelu(y).astype(BF16)


def _gmlp_in(h, w, jobs, tn=1024):
    rows = h.shape[0]
    tm = rows // IN_ROW_TILES
    n = w.shape[1]
    return _hosted_call(
        _gelu_kernel, (h, w),
        [pl.BlockSpec((tm, D_MODEL), lambda j, i: (i, 0)),
         pl.BlockSpec((D_MODEL, tn), lambda j, i: (0, j))],
        pl.BlockSpec((tm, tn), lambda j, i: (i, j)),
        jax.ShapeDtypeStruct((rows, n), BF16),
        (n // tn, rows // tm), jobs, "gmlp_in")


def _qkv_rope_kernel(h_ref, w_ref, cos_ref, sin_ref, o_ref, *, n_rope_tiles, tm, tn):
    j = pl.program_id(0)

    @pl.when(j < n_rope_tiles)
    def _():
        h = h_ref[...]
        c = cos_ref[...]
        s = sin_ref[...]
        lane = lax.broadcasted_iota(jnp.int32, (tm, LANES), 1)
        even = (lane & 1) == 0
        for q0 in range(0, tn, MXU_WIDTH):
            y = jnp.dot(h, w_ref[:, q0:q0 + MXU_WIDTH], preferred_element_type=F32)
            for q in range(0, MXU_WIDTH, LANES):
                xh = y[:, q:q + LANES]
                partner = jnp.where(even, pltpu.roll(xh, LANES - 1, 1), pltpu.roll(xh, 1, 1))
                o_ref[:, q0 + q:q0 + q + LANES] = (xh * c + partner * s).astype(BF16)

    @pl.when(j >= n_rope_tiles)
    def _():
        o_ref[...] = jnp.dot(h_ref[...], w_ref[...], preferred_element_type=F32).astype(BF16)


def _qkv_rope(h, w, cos_t, sin_t, rope_width, jobs, tn=512):
    rows = h.shape[0]
    tm = rows // IN_ROW_TILES
    n = w.shape[1]
    kern = functools.partial(_qkv_rope_kernel, n_rope_tiles=rope_width // tn, tm=tm, tn=tn)
    return _hosted_call(
        kern, (h, w, cos_t, sin_t),
        [pl.BlockSpec((tm, D_MODEL), lambda j, i: (i, 0)),
         pl.BlockSpec((D_MODEL, tn), lambda j, i: (0, j)),
         pl.BlockSpec((tm, LANES), lambda j, i: (i, 0)),
         pl.BlockSpec((tm, LANES), lambda j, i: (i, 0))],
        pl.BlockSpec((tm, tn), lambda j, i: (i, j)),
        jax.ShapeDtypeStruct((rows, n), BF16),
        (n // tn, rows // tm), jobs, "qkv_rope")


def _mm_out_kernel(a_ref, w_ref, x_ref, p_ref, xo_ref, *maybe_h_ref, coef):
    y = jnp.dot(a_ref[...], w_ref[...], preferred_element_type=F32)
    post = coef * (p_ref[P_GATE:P_GATE + 1, :] * p_ref[P_POST_G:P_POST_G + 1, :])
    xn = x_ref[...] + _rms(y) * post
    xo_ref[...] = xn
    if maybe_h_ref:
        maybe_h_ref[0][...] = _pre(xn, p_ref).astype(BF16)


def _mm_out(a, w, x, pp, s, *, rows, coef, want_h):
    k = a.shape[1]
    tm = 512 if k <= D_MODEL else 256
    out_shape = [jax.ShapeDtypeStruct((rows, D_MODEL), F32)]
    out_specs = [pl.BlockSpec((tm, D_MODEL), lambda i: (i, 0))]
    if want_h:
        out_shape.append(jax.ShapeDtypeStruct((rows, D_MODEL), BF16))
        out_specs.append(pl.BlockSpec((tm, D_MODEL), lambda i: (i, 0)))
    res = pl.pallas_call(
        functools.partial(_mm_out_kernel, coef=coef),
        grid=(rows // tm,),
        in_specs=[
            pl.BlockSpec((tm, k), lambda i: (i, 0)),
            pl.BlockSpec((k, D_MODEL), lambda i: (0, 0), pipeline_mode=pl.Buffered(1)),
            pl.BlockSpec((tm, D_MODEL), lambda i: (i, 0)),
            pl.BlockSpec((None, None, 8, D_MODEL), lambda i: (s, _group_of_block(i, tm), 0, 0)),
        ],
        out_specs=out_specs,
        out_shape=out_shape,
        compiler_params=_params(dimension_semantics=("arbitrary",)),
        name="mm_out",
    )(a, w, x, pp)
    return (res[0], res[1]) if want_h else (res[0], None)


def _dot_t(a, b):
    return lax.dot_general(a, b, (((1,), (1,)), ((), ())), preferred_element_type=F32)


def _win_attn_kernel(sink_ref, q_ref, kl_ref, km_ref, kr_ref, vl_ref, vm_ref, vr_ref, kc_ref, vc_ref, o_ref):
    n = pl.program_id(1)
    n_lat = SEQ // A_BLOCK
    scale = HEAD_DIM ** -0.5
    gq = A_GROUP * A_BLOCK
    row = lax.broadcasted_iota(jnp.int32, (gq, 1), 0)
    qi = lax.broadcasted_iota(jnp.int32, (gq, A_BLOCK), 0) & (A_BLOCK - 1)
    mi = lax.broadcasted_iota(jnp.int32, (gq, A_BLOCK), 1)

    def heads(kk):
        q4 = jnp.concatenate(
            [q_ref[:, (kk * A_GROUP + g) * HEAD_DIM:(kk * A_GROUP + g + 1) * HEAD_DIM] for g in range(A_GROUP)],
            axis=0)
        sink = jnp.full((gq, 1), sink_ref[kk * A_GROUP], F32)
        for g in range(1, A_GROUP):
            sink = jnp.where(row >= g * A_BLOCK, sink_ref[kk * A_GROUP + g], sink)
        return q4, sink, slice(kk * HEAD_DIM, (kk + 1) * HEAD_DIM)

    def store(kk, o):
        o = o.astype(BF16)
        for g in range(A_GROUP):
            h = kk * A_GROUP + g
            o_ref[:, h * HEAD_DIM:(h + 1) * HEAD_DIM] = o[g * A_BLOCK:(g + 1) * A_BLOCK, :]

    @pl.when(n < n_lat)
    def _():
        off_l = jnp.where(n > 0, 0, 2 * A_BLOCK)
        off_r = jnp.where(n < n_lat - 1, 0, 2 * A_BLOCK)
        ok_l = mi >= qi + off_l
        ok_r = mi <= qi - off_r
        for kk in range(A_KV_HEADS):
            q4, sink, cs = heads(kk)
            s_c = _dot_t(q4, kc_ref[:, cs]) * scale
            s_l = jnp.where(ok_l, _dot_t(q4, kl_ref[:, cs]) * scale, NEG_INF)
            s_m = _dot_t(q4, km_ref[:, cs]) * scale
            s_r = jnp.where(ok_r, _dot_t(q4, kr_ref[:, cs]) * scale, NEG_INF)
            m_lanes = jnp.maximum(jnp.maximum(s_c[:, :A_BLOCK], s_c[:, A_BLOCK:]),
                                  jnp.maximum(jnp.maximum(s_l, s_m), s_r))
            m = jnp.maximum(jnp.max(m_lanes, axis=-1, keepdims=True), sink)
            p_c = jnp.exp(s_c - m)
            p_l = jnp.exp(s_l - m)
            p_m = jnp.exp(s_m - m)
            p_r = jnp.exp(s_r - m)
            p_lanes = (p_c[:, :A_BLOCK] + p_c[:, A_BLOCK:]) + (p_l + p_m + p_r)
            den = jnp.exp(sink - m) + jnp.sum(p_lanes, axis=-1, keepdims=True)
            o = (jnp.dot(p_c.astype(BF16), vc_ref[:, cs], preferred_element_type=F32)
                 + jnp.dot(p_l.astype(BF16), vl_ref[:, cs], preferred_element_type=F32)
                 + jnp.dot(p_m.astype(BF16), vm_ref[:, cs], preferred_element_type=F32)
                 + jnp.dot(p_r.astype(BF16), vr_ref[:, cs], preferred_element_type=F32))
            store(kk, o / den)

    @pl.when(n >= n_lat)
    def _():
        for kk in range(A_KV_HEADS):
            q4, sink, cs = heads(kk)
            s_c = _dot_t(q4, kc_ref[:, cs]) * scale
            m = jnp.maximum(jnp.max(s_c, axis=-1, keepdims=True), sink)
            p_c = jnp.exp(s_c - m)
            den = jnp.exp(sink - m) + jnp.sum(p_c, axis=-1, keepdims=True)
            o = jnp.dot(p_c.astype(BF16), vc_ref[:, cs], preferred_element_type=F32)
            store(kk, o / den)


def _win_attn(qkv, sink, with_ctx):
    n_lat = SEQ // A_BLOCK
    n_ctx = CTX_LEN // A_BLOCK
    steps = n_lat + (n_ctx if with_ctx else 0)
    kvw = A_KV_HEADS * HEAD_DIM
    kcol = A_HEADS * HEAD_DIM // kvw
    ctx_blk0 = LAT_ROWS // A_BLOCK

    def qrow(b, n):
        return jnp.where(n < n_lat, b * n_lat + n, ctx_blk0 + b * n_ctx + (n - n_lat))

    def nb(b, n, d):
        return b * n_lat + jnp.clip(n + d, 0, n_lat - 1)

    kv_spec = lambda d, c: pl.BlockSpec((A_BLOCK, kvw), lambda b, n: (nb(b, n, d), c))
    ctx_spec = lambda c: pl.BlockSpec((CTX_LEN, kvw), lambda b, n: (LAT_ROWS // CTX_LEN + b, c))
    rows = LAT_ROWS + (CTX_ROWS if with_ctx else 0)
    return pl.pallas_call(
        _win_attn_kernel,
        grid=(BATCH, steps),
        in_specs=[
            pl.BlockSpec(memory_space=pltpu.SMEM),
            pl.BlockSpec((A_BLOCK, A_HEADS * HEAD_DIM), lambda b, n: (qrow(b, n), 0)),
            kv_spec(-1, kcol), kv_spec(0, kcol), kv_spec(1, kcol),
            kv_spec(-1, kcol + 1), kv_spec(0, kcol + 1), kv_spec(1, kcol + 1),
            ctx_spec(kcol), ctx_spec(kcol + 1),
        ],
        out_specs=pl.BlockSpec((A_BLOCK, D_MODEL), lambda b, n: (qrow(b, n), 0)),
        out_shape=jax.ShapeDtypeStruct((rows, D_MODEL), BF16),
        compiler_params=_params(dimension_semantics=("arbitrary", "arbitrary")),
        name="win_attn",
    )(sink, qkv, qkv, qkv, qkv, qkv, qkv, qkv, qkv, qkv)


def _gmlp_mid_kernel(u_ref, v_ref, g_ref, b_ref, ws_ref, bs_ref, o_ref, *, tm):
    v = v_ref[...].astype(F32)
    mu = jnp.mean(v, axis=-1, keepdims=True)
    vc = v - mu
    vn = vc * lax.rsqrt(jnp.mean(vc * vc, axis=-1, keepdims=True) + NORM_EPS)
    vn = (vn * g_ref[...] + b_ref[...]).astype(BF16)
    for g in range(B_GROUPS):
        cs = slice(g * B_GROUP_W, (g + 1) * B_GROUP_W)
        ws = ws_ref[g].astype(BF16)
        for c in range(tm // B_CHUNK):
            rs = slice(c * B_CHUNK, (c + 1) * B_CHUNK)
            mixed = jnp.dot(ws, vn[rs, cs], preferred_element_type=F32) + bs_ref[:, g:g + 1]
            o_ref[rs, cs] = (u_ref[rs, cs].astype(F32) * mixed).astype(BF16)


def _gmlp_mid(z, vn_g, vn_b, ws, bs_t, tm=256):
    rows = z.shape[0]
    return pl.pallas_call(
        functools.partial(_gmlp_mid_kernel, tm=tm),
        grid=(rows // tm,),
        in_specs=[
            pl.BlockSpec((tm, B_WIDTH), lambda i: (i, 0)),
            pl.BlockSpec((tm, B_WIDTH), lambda i: (i, 1)),
            pl.BlockSpec((1, B_WIDTH), lambda i: (0, 0)),
            pl.BlockSpec((1, B_WIDTH), lambda i: (0, 0)),
            pl.BlockSpec((B_GROUPS, B_CHUNK, B_CHUNK), lambda i: (0, 0, 0)),
            pl.BlockSpec((B_CHUNK, B_GROUPS), lambda i: (0, 0)),
        ],
        out_specs=pl.BlockSpec((tm, B_WIDTH), lambda i: (i, 0)),
        out_shape=jax.ShapeDtypeStruct((rows, B_WIDTH), BF16),
        compiler_params=_params(dimension_semantics=("arbitrary",)),
        name="gmlp_mid",
    )(z, z, vn_g, vn_b, ws, bs_t)


def _diff_attn_kernel(lq1_ref, lk1_ref, lq2_ref, lk2_ref, sg_ref, q_ref, k_ref, v_ref, kc_ref, vc_ref, o_ref, vt_ref,
                      *, lam_init, tq, tk):
    qi = pl.program_id(2)
    n_lat = SEQ // tq
    expo = (HEAD_DIM ** -0.5) * math.log2(math.e)
    lam = (jnp.exp(jnp.sum(lq1_ref[...] * lk1_ref[...], axis=-1, keepdims=True))
           - jnp.exp(jnp.sum(lq2_ref[...] * lk2_ref[...], axis=-1, keepdims=True)) + lam_init)

    @pl.when(qi == 0)
    def _():
        vt_ref[:, :CTX_LEN] = vc_ref[...].T
        for t in range(SEQ // tk):
            vt_ref[:, CTX_LEN + t * tk:CTX_LEN + (t + 1) * tk] = v_ref[t * tk:(t + 1) * tk, :].T

    def run(chunks):
        qs = [q_ref[:, c * HEAD_DIM:(c + 1) * HEAD_DIM] for c in range(2)]

        def scores(chunk):
            kref, lo, size, _ = chunk
            return [_dot_t(kref[lo:lo + size, c * HEAD_DIM:(c + 1) * HEAD_DIM], qs[c]) for c in range(2)]

        m = [None, None]
        l = [None, None]
        acc = [None, None]
        s_next = scores(chunks[0])
        for t, (_, _, size, vlo) in enumerate(chunks):
            s_cur = s_next
            if t + 1 < len(chunks):
                s_next = scores(chunks[t + 1])
            for c in range(2):
                st = s_cur[c]
                m_new = jnp.max(st, axis=0, keepdims=True)
                if m[c] is not None:
                    m_new = jnp.maximum(m[c], m_new)
                p = jnp.exp2((st - m_new) * expo)
                pv = jnp.dot(vt_ref[:, vlo:vlo + size], p.astype(BF16), preferred_element_type=F32)
                if m[c] is None:
                    l[c] = jnp.sum(p, axis=0, keepdims=True)
                    acc[c] = pv
                else:
                    alpha = jnp.exp2((m[c] - m_new) * expo)
                    l[c] = alpha * l[c] + jnp.sum(p, axis=0, keepdims=True)
                    acc[c] = alpha * acc[c] + pv
                m[c] = m_new
        o = acc[0] / l[0] - lam * (acc[1] / l[1])
        o = o * lax.rsqrt(jnp.mean(o * o, axis=0, keepdims=True) + NORM_EPS) * (sg_ref[...] * (1.0 - lam_init))
        o_ref[...] = o.T.astype(BF16)

    @pl.when(qi < n_lat)
    def _():
        run([(kc_ref, 0, CTX_LEN, 0)] + [(k_ref, t * tk, tk, CTX_LEN + t * tk) for t in range(SEQ // tk)])

    @pl.when(qi >= n_lat)
    def _():
        run([(kc_ref, 0, CTX_LEN, 0)])


def _diff_attn(qkv, lq1, lk1, lq2, lk2, subln_g_col, lam_init, tq=256, tk=1024):
    hw = 2 * HEAD_DIM
    n_lat = SEQ // tq
    n_ctx = CTX_LEN // tq
    kcol = D_MODEL // hw
    ctx_blk0 = LAT_ROWS // tq

    def qrow(b, qi):
        return jnp.where(qi < n_lat, b * n_lat + qi, ctx_blk0 + b * n_ctx + (qi - n_lat))

    vec = pl.BlockSpec((1, HEAD_DIM), lambda b, h, qi: (0, 0))
    kern = functools.partial(_diff_attn_kernel, lam_init=lam_init, tq=tq, tk=tk)
    return pl.pallas_call(
        kern,
        grid=(BATCH, C_HEADS, n_lat + n_ctx),
        in_specs=[
            vec, vec, vec, vec,
            pl.BlockSpec((hw, 1), lambda b, h, qi: (0, 0)),
            pl.BlockSpec((tq, hw), lambda b, h, qi: (qrow(b, qi), h)),
            pl.BlockSpec((SEQ, hw), lambda b, h, qi: (b, kcol + h)),
            pl.BlockSpec((SEQ, hw), lambda b, h, qi: (b, 2 * kcol + h)),
            pl.BlockSpec((CTX_LEN, hw), lambda b, h, qi: (LAT_ROWS // CTX_LEN + b, kcol + h)),
            pl.BlockSpec((CTX_LEN, hw), lambda b, h, qi: (LAT_ROWS // CTX_LEN + b, 2 * kcol + h)),
        ],
        out_specs=pl.BlockSpec((tq, hw), lambda b, h, qi: (qrow(b, qi), h)),
        out_shape=jax.ShapeDtypeStruct((ROWS, D_MODEL), BF16),
        scratch_shapes=[pltpu.VMEM((hw, CTX_LEN + SEQ), BF16)],
        compiler_params=_params(dimension_semantics=("arbitrary", "arbitrary", "arbitrary")),
        name="diff_attn",
    )(lq1, lk1, lq2, lk2, subln_g_col, qkv, qkv, qkv, qkv, qkv)


def _rope_tables():
    rows = SEQ // GRID_W
    row = jnp.repeat(jnp.arange(rows, dtype=F32), GRID_W)
    col = jnp.tile(jnp.arange(GRID_W, dtype=F32), rows)
    axis_dim = HEAD_DIM // 2
    inv = ROPE_THETA ** (-jnp.arange(0, axis_dim, 2, dtype=F32) / axis_dim)
    ang = jnp.concatenate([row[:, None] * inv, col[:, None] * inv], axis=-1)
    cos = jnp.repeat(jnp.cos(ang), 2, axis=-1)
    sin = jnp.repeat(jnp.sin(ang), 2, axis=-1) * jnp.tile(jnp.array([-1.0, 1.0], F32), HEAD_DIM // 2)
    cos_t = jnp.concatenate([jnp.tile(cos, (BATCH, 1)), jnp.ones((CTX_ROWS, HEAD_DIM), F32)], axis=0)
    sin_t = jnp.concatenate([jnp.tile(sin, (BATCH, 1)), jnp.zeros((CTX_ROWS, HEAD_DIM), F32)], axis=0)
    return cos_t, sin_t


def _param_tiles(mods, norm_g):
    m = mods[:, :N_GROUPS].reshape(DEPTH, N_GROUPS, N_MOD, D_MODEL)
    zero = jnp.zeros((N_GROUPS, D_MODEL), F32)
    bcast = lambda v: jnp.broadcast_to(v, (N_GROUPS, D_MODEL))
    tiles = []
    for s in range(-1, 3 * DEPTH):
        rows = [zero] * 8
        if s >= 0:
            i, slot = divmod(s, 3)
            rows[P_POST_G] = bcast(norm_g[i, 2 * slot + 1])
            rows[P_GATE] = m[i, :, 3 * slot + 2]
        if s + 1 < 3 * DEPTH:
            i, slot = divmod(s + 1, 3)
            rows[P_PRE_G] = bcast(norm_g[i, 2 * slot])
            rows[P_SCALE] = m[i, :, 3 * slot + 1]
            rows[P_SHIFT] = m[i, :, 3 * slot]
        tiles.append(jnp.stack(rows, axis=1))
    return jnp.stack(tiles, axis=0)


def kernel(x, c, ctx, c_ctx, ada_w, ada_b, norm_g, ffn_w_in, ffn_w_out, a_w_in, a_w_out, a_sink, b_w_in, b_vnorm_g,
           b_vnorm_b, b_ws, b_bs, b_w_out, c_w_in, c_w_out, c_lq1, c_lk1, c_lq2, c_lk2, c_subln_g):
    xs = jnp.concatenate([x.reshape(LAT_ROWS, D_MODEL), ctx.reshape(CTX_ROWS, D_MODEL)], axis=0)
    cvec = jnp.concatenate([c, c_ctx[None, :], jnp.zeros((8 - BATCH - 1, D_MODEL), F32)], axis=0)
    pp = _param_tiles(_ada_mods(cvec, ada_w, ada_b), norm_g)
    cos_t, sin_t = _rope_tables()

    mixer_w = ((a_w_in, a_w_out), (b_w_in, b_w_out), (c_w_in, c_w_out))
    wgu = _convert(_ffn_split_job(ffn_w_in, (0, 0)))
    h = _prenorm(xs, pp, 0)
    for i in range(DEPTH):
        kind, j = i % N_MIXERS, i // N_MIXERS
        ctx_live = i < DEPTH - 1
        rows_out = ROWS if ctx_live else LAT_ROWS
        s0 = 3 * i + 1
        w_mix_in, w_mix_out = mixer_w[kind]

        a, (wdn, w_i) = _ffn_in(h, wgu, [_cast_job(ffn_w_out, (i, 0), D_FF_PAD), _cast_job(w_mix_in, (j,))])
        xs, h = _mm_out(a, wdn, xs, pp, s0, rows=ROWS, coef=0.5, want_h=True)

        jobs = [_cast_job(w_mix_out, (j,)), _ffn_split_job(ffn_w_in, (i, 1))]
        if kind == 0:
            qkv, (w_o, wgu) = _qkv_rope(h, w_i, cos_t, sin_t, (A_HEADS + A_KV_HEADS) * HEAD_DIM, jobs)
            o = _win_attn(qkv, a_sink[j], ctx_live)
        elif kind == 1:
            z, (w_o, wgu) = _gmlp_in(h, w_i, jobs)
            o = _gmlp_mid(z, b_vnorm_g[j][None, :], b_vnorm_b[j][None, :], b_ws[j], b_bs[j].T)
        else:
            lam_init = 0.8 - 0.6 * math.exp(-0.3 * i)
            qkv, (w_o, wgu) = _qkv_rope(h, w_i, cos_t, sin_t, 2 * D_MODEL, jobs)
            o = _diff_attn(qkv, c_lq1[j][None, :], c_lk1[j][None, :], c_lq2[j][None, :], c_lk2[j][None, :],
                           c_subln_g[j][:, None], lam_init)
        xs, h = _mm_out(o, w_o, xs, pp, s0 + 1, rows=rows_out, coef=1.0, want_h=True)

        jobs = [_cast_job(ffn_w_out, (i, 1), D_FF_PAD)]
        if i + 1 < DEPTH:
            jobs.append(_ffn_split_job(ffn_w_in, (i + 1, 0)))
        a, conv = _ffn_in(h, wgu, jobs)
        xs, h = _mm_out(a, conv[0], xs, pp, s0 + 2, rows=rows_out, coef=0.5, want_h=i + 1 < DEPTH)
        wgu = conv[1] if i + 1 < DEPTH else None
    return xs.reshape(BATCH, SEQ, D_MODEL)
```

```python
import functools
import math
from typing import Callable, NamedTuple

import jax
import jax.numpy as jnp
from jax import lax
from jax.experimental import pallas as pl
from jax.experimental.pallas import tpu as pltpu

D_MODEL = 2048
BATCH = 2
SEQ = 4096
DEPTH = 4
GRID_W = 64
CTX_LEN = 256
N_MIXERS = 3
N_MOD = 9
NORM_EPS = 1e-6
ROPE_THETA = 10000.0
NEG_INF = -1e30
D_FF = 5504
HEAD_DIM = 128
A_HEADS = 16
A_KV_HEADS = 4
A_GROUP = 4
A_BLOCK = 128
B_CHUNK = 128
B_WIDTH = 3 * D_MODEL
B_GROUPS = 8
B_GROUP_W = B_WIDTH // B_GROUPS
C_HEADS = 8

LAT_ROWS = BATCH * SEQ
CTX_ROWS = BATCH * CTX_LEN
ROWS = LAT_ROWS + CTX_ROWS
N_GROUPS = 3
LANES = 128
BF16_SUBLANES = 16
MXU_WIDTH = 256
IN_ROW_TILES = 8
FF_TILE = 512
D_FF_PAD = -(-D_FF // FF_TILE) * FF_TILE
VMEM_LIMIT = 56 * 1024 * 1024

BF16 = jnp.bfloat16
F32 = jnp.float32

P_POST_G, P_GATE, P_PRE_G, P_SCALE, P_SHIFT = 0, 1, 2, 3, 4


def _params(**kw):
    return pltpu.CompilerParams(vmem_limit_bytes=VMEM_LIMIT, **kw)


def _group_of_block(i, tm):
    return jnp.minimum(i // (SEQ // tm), N_GROUPS - 1)


def _rms(x):
    return x * lax.rsqrt(jnp.mean(x * x, axis=-1, keepdims=True) + NORM_EPS)


def _ada_kernel(c_ref, w_ref, b_ref, o_ref):
    @pl.when(pl.program_id(1) == 0)
    def _():
        o_ref[...] = jnp.broadcast_to(b_ref[...], o_ref.shape)

    c = c_ref[...]
    a = (c * jax.nn.sigmoid(c)).astype(BF16)
    o_ref[...] += jnp.dot(a, w_ref[...].astype(BF16), preferred_element_type=F32)


def _ada_mods(cvec, ada_w, ada_b):
    kb = LANES
    n = N_MOD * D_MODEL
    return pl.pallas_call(
        _ada_kernel,
        grid=(DEPTH, D_MODEL // kb),
        in_specs=[
            pl.BlockSpec((8, kb), lambda l, k: (0, k)),
            pl.BlockSpec((None, kb, n), lambda l, k: (l, k, 0)),
            pl.BlockSpec((None, 1, n), lambda l, k: (l, 0, 0)),
        ],
        out_specs=pl.BlockSpec((None, 8, n), lambda l, k: (l, 0, 0)),
        out_shape=jax.ShapeDtypeStruct((DEPTH, 8, n), F32),
        compiler_params=_params(dimension_semantics=("arbitrary", "arbitrary")),
        name="ada_mods",
    )(cvec, ada_w, ada_b.reshape(DEPTH, 1, n))


def _pre(x, p_ref):
    gain = p_ref[P_PRE_G:P_PRE_G + 1, :] * (1.0 + p_ref[P_SCALE:P_SCALE + 1, :])
    return _rms(x) * gain + p_ref[P_SHIFT:P_SHIFT + 1, :]


def _prenorm_kernel(x_ref, p_ref, h_ref):
    h_ref[...] = _pre(x_ref[...], p_ref).astype(BF16)


def _prenorm(x, pp, s, tm=512):
    rows = x.shape[0]
    return pl.pallas_call(
        _prenorm_kernel,
        grid=(rows // tm,),
        in_specs=[
            pl.BlockSpec((tm, D_MODEL), lambda i: (i, 0)),
            pl.BlockSpec((None, None, 8, D_MODEL), lambda i: (s, _group_of_block(i, tm), 0, 0)),
        ],
        out_specs=pl.BlockSpec((tm, D_MODEL), lambda i: (i, 0)),
        out_shape=jax.ShapeDtypeStruct((rows, D_MODEL), BF16),
        compiler_params=_params(dimension_semantics=("arbitrary",)),
        name="prenorm",
    )(x, pp)


class _Job(NamedTuple):
    src: jax.Array
    in_spec: pl.BlockSpec
    out_spec: pl.BlockSpec
    out_shape: jax.ShapeDtypeStruct
    body: Callable
    n_blocks: int


def _row_block(rows, rows_out, steps):
    g = math.gcd(rows, rows_out)
    for rb in range(BF16_SUBLANES, g + 1, BF16_SUBLANES):
        if g % rb == 0 and rows_out // rb <= steps:
            return rb
    raise ValueError(f"no row block for {rows}->{rows_out} rows in {steps} steps")


def _cast_job(src, lead, rows_out=None):
    rows, cols = src.shape[-2:]
    rows_out = rows_out or rows

    def make(steps, lin):
        rb = _row_block(rows, rows_out, steps)
        nb_in, nb_out = rows // rb, rows_out // rb

        def body(src_ref, dst_ref, blk):
            v = src_ref[...].astype(BF16)
            if nb_out > nb_in:
                v = jnp.where(blk < nb_in, v, jnp.zeros_like(v))
            dst_ref[...] = v

        return _Job(
            src,
            pl.BlockSpec((None,) * len(lead) + (rb, cols),
                         lambda *g: lead + (jnp.minimum(lin(*g), nb_in - 1), 0)),
            pl.BlockSpec((rb, cols), lambda *g: (jnp.minimum(lin(*g), nb_out - 1), 0)),
            jax.ShapeDtypeStruct((rows_out, cols), BF16), body, nb_out)

    return make


def _ffn_split_job(ffn_w_in, lead):
    def make(steps, lin):
        rb = _row_block(D_MODEL, D_MODEL, steps)
        nb = D_MODEL // rb

        def body(src_ref, dst_ref, blk):
            del blk
            for part in range(2):
                dst_ref[part, :, :D_FF] = src_ref[:, part * D_FF:(part + 1) * D_FF].astype(BF16)
                dst_ref[part, :, D_FF:] = jnp.zeros((rb, D_FF_PAD - D_FF), BF16)

        return _Job(
            ffn_w_in,
            pl.BlockSpec((None,) * len(lead) + (rb, 2 * D_FF), lambda *g: lead + (jnp.minimum(lin(*g), nb - 1), 0)),
            pl.BlockSpec((2, rb, D_FF_PAD), lambda *g: (0, jnp.minimum(lin(*g), nb - 1), 0)),
            jax.ShapeDtypeStruct((2, D_MODEL, D_FF_PAD), BF16), body, nb)

    return make


def _hosted_call(main, main_args, main_in_specs, out_spec, out_shape, grid, job_makers, name):
    steps = math.prod(grid)
    strides = [math.prod(grid[d + 1:]) for d in range(len(grid))]
    lin = lambda *g: sum(gi * st for gi, st in zip(g, strides))
    jobs = [mk(steps, lin) for mk in job_makers]
    n_in, n_jobs = len(main_args), len(jobs)

    def kern(*refs):
        if main is not None:
            main(*refs[:n_in], refs[n_in + n_jobs])
        t = lin(*[pl.program_id(d) for d in range(len(grid))])
        n_main_out = 0 if main is None else 1
        for q, jb in enumerate(jobs):
            jb.body(refs[n_in + q], refs[n_in + n_jobs + n_main_out + q], jnp.minimum(t, jb.n_blocks - 1))

    main_out = [] if main is None else [(out_spec, out_shape)]
    res = pl.pallas_call(
        kern,
        grid=grid,
        in_specs=list(main_in_specs) + [jb.in_spec for jb in jobs],
        out_specs=[s for s, _ in main_out] + [jb.out_spec for jb in jobs],
        out_shape=[s for _, s in main_out] + [jb.out_shape for jb in jobs],
        compiler_params=_params(dimension_semantics=("arbitrary",) * len(grid)),
        name=name,
    )(*main_args, *[jb.src for jb in jobs])
    return (None, list(res)) if main is None else (res[0], list(res[1:]))


def _convert(job_maker, steps=32):
    return _hosted_call(None, (), (), None, None, (steps,), [job_maker], "convert")[1][0]


def _swiglu_kernel(h_ref, wg_ref, wu_ref, o_ref):
    h = h_ref[...]
    g = jnp.dot(h, wg_ref[...], preferred_element_type=F32)
    u = jnp.dot(h, wu_ref[...], preferred_element_type=F32)
    o_ref[...] = (g * jax.nn.sigmoid(g) * u).astype(BF16)


def _ffn_in(h, wgu, jobs, tn=FF_TILE):
    rows = h.shape[0]
    tm = rows // IN_ROW_TILES
    return _hosted_call(
        _swiglu_kernel, (h, wgu, wgu),
        [pl.BlockSpec((tm, D_MODEL), lambda j, i: (i, 0)),
         pl.BlockSpec((None, D_MODEL, tn), lambda j, i: (0, 0, j)),
         pl.BlockSpec((None, D_MODEL, tn), lambda j, i: (1, 0, j))],
        pl.BlockSpec((tm, tn), lambda j, i: (i, j)),
        jax.ShapeDtypeStruct((rows, D_FF_PAD), BF16),
        (D_FF_PAD // tn, rows // tm), jobs, "ffn_in")


def _gelu_kernel(h_ref, w_ref, o_ref):
    y = jnp.dot(h_ref[...], w_ref[...], preferred_element_type=F32)
    o_ref[...] = jax.nn.gelu(y).astype(BF16)


def _gmlp_in(h, w, jobs, tn=1024):
    rows = h.shape[0]
    tm = rows // IN_ROW_TILES
    n = w.shape[1]
    return _hosted_call(
        _gelu_kernel, (h, w),
        [pl.BlockSpec((tm, D_MODEL), lambda j, i: (i, 0)),
         pl.BlockSpec((D_MODEL, tn), lambda j, i: (0, j))],
        pl.BlockSpec((tm, tn), lambda j, i: (i, j)),
        jax.ShapeDtypeStruct((rows, n), BF16),
        (n // tn, rows // tm), jobs, "gmlp_in")


def _qkv_rope_kernel(h_ref, w_ref, cos_ref, sin_ref, o_ref, *, n_rope_tiles, tm, tn):
    j = pl.program_id(0)

    @pl.when(j < n_rope_tiles)
    def _():
        h = h_ref[...]
        c = cos_ref[...]
        s = sin_ref[...]
        lane = lax.broadcasted_iota(jnp.int32, (tm, LANES), 1)
        even = (lane & 1) == 0
        for q0 in range(0, tn, MXU_WIDTH):
            y = jnp.dot(h, w_ref[:, q0:q0 + MXU_WIDTH], preferred_element_type=F32)
            for q in range(0, MXU_WIDTH, LANES):
                xh = y[:, q:q + LANES]
                partner = jnp.where(even, pltpu.roll(xh, LANES - 1, 1), pltpu.roll(xh, 1, 1))
                o_ref[:, q0 + q:q0 + q + LANES] = (xh * c + partner * s).astype(BF16)

    @pl.when(j >= n_rope_tiles)
    def _():
        o_ref[...] = jnp.dot(h_ref[...], w_ref[...], preferred_element_type=F32).astype(BF16)


def _qkv_rope(h, w, cos_t, sin_t, rope_width, jobs, tn=512):
    rows = h.shape[0]
    tm = rows // IN_ROW_TILES
    n = w.shape[1]
    kern = functools.partial(_qkv_rope_kernel, n_rope_tiles=rope_width // tn, tm=tm, tn=tn)
    return _hosted_call(
        kern, (h, w, cos_t, sin_t),
        [pl.BlockSpec((tm, D_MODEL), lambda j, i: (i, 0)),
         pl.BlockSpec((D_MODEL, tn), lambda j, i: (0, j)),
         pl.BlockSpec((tm, LANES), lambda j, i: (i, 0)),
         pl.BlockSpec((tm, LANES), lambda j, i: (i, 0))],
        pl.BlockSpec((tm, tn), lambda j, i: (i, j)),
        jax.ShapeDtypeStruct((rows, n), BF16),
        (n // tn, rows // tm), jobs, "qkv_rope")


def _mm_out_kernel(a_ref, w_ref, x_ref, p_ref, xo_ref, *maybe_h_ref, coef):
    post = coef * (p_ref[P_GATE:P_GATE + 1, :] * p_ref[P_POST_G:P_POST_G + 1, :])
    half = a_ref.shape[0] // 2
    for rs in (slice(0, half), slice(half, 2 * half)):
        y = jnp.dot(a_ref[rs, :], w_ref[...], preferred_element_type=F32)
        xn = x_ref[rs, :] + _rms(y) * post
        xo_ref[rs, :] = xn
        if maybe_h_ref:
            maybe_h_ref[0][rs, :] = _pre(xn, p_ref).astype(BF16)


def _mm_out(a, w, x, pp, s, *, rows, coef, want_h):
    k = a.shape[1]
    tm = 512 if k <= D_MODEL else 256
    out_shape = [jax.ShapeDtypeStruct((rows, D_MODEL), F32)]
    out_specs = [pl.BlockSpec((tm, D_MODEL), lambda i: (i, 0))]
    if want_h:
        out_shape.append(jax.ShapeDtypeStruct((rows, D_MODEL), BF16))
        out_specs.append(pl.BlockSpec((tm, D_MODEL), lambda i: (i, 0)))
    res = pl.pallas_call(
        functools.partial(_mm_out_kernel, coef=coef),
        grid=(rows // tm,),
        in_specs=[
            pl.BlockSpec((tm, k), lambda i: (i, 0)),
            pl.BlockSpec((k, D_MODEL), lambda i: (0, 0), pipeline_mode=pl.Buffered(1)),
            pl.BlockSpec((tm, D_MODEL), lambda i: (i, 0)),
            pl.BlockSpec((None, None, 8, D_MODEL), lambda i: (s, _group_of_block(i, tm), 0, 0)),
        ],
        out_specs=out_specs,
        out_shape=out_shape,
        compiler_params=_params(dimension_semantics=("arbitrary",)),
        name="mm_out",
    )(a, w, x, pp)
    return (res[0], res[1]) if want_h else (res[0], None)


def _dot_t(a, b):
    return lax.dot_general(a, b, (((1,), (1,)), ((), ())), preferred_element_type=F32)


def _dot_tn(a, b):
    return lax.dot_general(a, b, (((0,), (0,)), ((), ())), preferred_element_type=F32)


def _win_attn_kernel(sink_ref, q_ref, kl_ref, km_ref, kr_ref, vl_ref, vm_ref, vr_ref, kc_ref, vc_ref, o_ref):
    n = pl.program_id(1)
    n_lat = SEQ // A_BLOCK
    scale = HEAD_DIM ** -0.5
    expo = scale * math.log2(math.e)
    gq = A_GROUP * A_BLOCK
    lane = lax.broadcasted_iota(jnp.int32, (1, gq), 1)
    qi = lax.broadcasted_iota(jnp.int32, (A_BLOCK, gq), 1) & (A_BLOCK - 1)
    mi = lax.broadcasted_iota(jnp.int32, (A_BLOCK, gq), 0)

    def heads(kk):
        q4 = jnp.concatenate(
            [q_ref[:, (kk * A_GROUP + g) * HEAD_DIM:(kk * A_GROUP + g + 1) * HEAD_DIM] for g in range(A_GROUP)],
            axis=0)
        sink = jnp.full((1, gq), sink_ref[kk * A_GROUP] / scale, F32)
        for g in range(1, A_GROUP):
            sink = jnp.where(lane >= g * A_BLOCK, sink_ref[kk * A_GROUP + g] / scale, sink)
        return q4, sink, slice(kk * HEAD_DIM, (kk + 1) * HEAD_DIM)

    def finish(kk, pieces, sink):
        m = sink
        for s, _ in pieces:
            m = jnp.maximum(m, jnp.max(s, axis=0, keepdims=True))
        den = jnp.exp2((sink - m) * expo)
        ot = None
        for s, v in pieces:
            p = jnp.exp2((s - m) * expo)
            den = den + jnp.sum(p, axis=0, keepdims=True)
            pv = _dot_tn(v, p.astype(BF16))
            ot = pv if ot is None else ot + pv
        ot = ot / den
        for g in range(A_GROUP):
            h = kk * A_GROUP + g
            o_ref[:, h * HEAD_DIM:(h + 1) * HEAD_DIM] = ot[:, g * A_BLOCK:(g + 1) * A_BLOCK].T.astype(BF16)

    @pl.when(n < n_lat)
    def _():
        off_l = jnp.where(n > 0, 0, 2 * A_BLOCK)
        off_r = jnp.where(n < n_lat - 1, 0, 2 * A_BLOCK)
        ok_l = mi >= qi + off_l
        ok_r = mi <= qi - off_r
        for kk in range(A_KV_HEADS):
            q4, sink, cs = heads(kk)
            s_l = jnp.where(ok_l, _dot_t(kl_ref[:, cs], q4), NEG_INF)
            s_r = jnp.where(ok_r, _dot_t(kr_ref[:, cs], q4), NEG_INF)
            finish(kk, [(_dot_t(kc_ref[:, cs], q4), vc_ref[:, cs]), (s_l, vl_ref[:, cs]),
                        (_dot_t(km_ref[:, cs], q4), vm_ref[:, cs]), (s_r, vr_ref[:, cs])], sink)

    @pl.when(n >= n_lat)
    def _():
        for kk in range(A_KV_HEADS):
            q4, sink, cs = heads(kk)
            finish(kk, [(_dot_t(kc_ref[:, cs], q4), vc_ref[:, cs])], sink)


def _win_attn(qkv, sink, with_ctx):
    n_lat = SEQ // A_BLOCK
    n_ctx = CTX_LEN // A_BLOCK
    steps = n_lat + (n_ctx if with_ctx else 0)
    kvw = A_KV_HEADS * HEAD_DIM
    kcol = A_HEADS * HEAD_DIM // kvw
    ctx_blk0 = LAT_ROWS // A_BLOCK

    def qrow(b, n):
        return jnp.where(n < n_lat, b * n_lat + n, ctx_blk0 + b * n_ctx + (n - n_lat))

    def nb(b, n, d):
        return b * n_lat + jnp.clip(n + d, 0, n_lat - 1)

    kv_spec = lambda d, c: pl.BlockSpec((A_BLOCK, kvw), lambda b, n: (nb(b, n, d), c))
    ctx_spec = lambda c: pl.BlockSpec((CTX_LEN, kvw), lambda b, n: (LAT_ROWS // CTX_LEN + b, c))
    rows = LAT_ROWS + (CTX_ROWS if with_ctx else 0)
    return pl.pallas_call(
        _win_attn_kernel,
        grid=(BATCH, steps),
        in_specs=[
            pl.BlockSpec(memory_space=pltpu.SMEM),
            pl.BlockSpec((A_BLOCK, A_HEADS * HEAD_DIM), lambda b, n: (qrow(b, n), 0)),
            kv_spec(-1, kcol), kv_spec(0, kcol), kv_spec(1, kcol),
            kv_spec(-1, kcol + 1), kv_spec(0, kcol + 1), kv_spec(1, kcol + 1),
            ctx_spec(kcol), ctx_spec(kcol + 1),
        ],
        out_specs=pl.BlockSpec((A_BLOCK, D_MODEL), lambda b, n: (qrow(b, n), 0)),
        out_shape=jax.ShapeDtypeStruct((rows, D_MODEL), BF16),
        compiler_params=_params(dimension_semantics=("arbitrary", "arbitrary")),
        name="win_attn",
    )(sink, qkv, qkv, qkv, qkv, qkv, qkv, qkv, qkv, qkv)


def _gmlp_mid_kernel(u_ref, v_ref, g_ref, b_ref, ws_ref, bs_ref, o_ref, *, tm):
    v = v_ref[...].astype(F32)
    mu = jnp.mean(v, axis=-1, keepdims=True)
    vc = v - mu
    vn = vc * lax.rsqrt(jnp.mean(vc * vc, axis=-1, keepdims=True) + NORM_EPS)
    vn = (vn * g_ref[...] + b_ref[...]).astype(BF16)
    for g in range(B_GROUPS):
        cs = slice(g * B_GROUP_W, (g + 1) * B_GROUP_W)
        ws = ws_ref[g].astype(BF16)
        for c in range(tm // B_CHUNK):
            rs = slice(c * B_CHUNK, (c + 1) * B_CHUNK)
            mixed = jnp.dot(ws, vn[rs, cs], preferred_element_type=F32) + bs_ref[:, g:g + 1]
            o_ref[rs, cs] = (u_ref[rs, cs].astype(F32) * mixed).astype(BF16)


def _gmlp_mid(z, vn_g, vn_b, ws, bs_t, tm=256):
    rows = z.shape[0]
    return pl.pallas_call(
        functools.partial(_gmlp_mid_kernel, tm=tm),
        grid=(rows // tm,),
        in_specs=[
            pl.BlockSpec((tm, B_WIDTH), lambda i: (i, 0)),
            pl.BlockSpec((tm, B_WIDTH), lambda i: (i, 1)),
            pl.BlockSpec((1, B_WIDTH), lambda i: (0, 0)),
            pl.BlockSpec((1, B_WIDTH), lambda i: (0, 0)),
            pl.BlockSpec((B_GROUPS, B_CHUNK, B_CHUNK), lambda i: (0, 0, 0)),
            pl.BlockSpec((B_CHUNK, B_GROUPS), lambda i: (0, 0)),
        ],
        out_specs=pl.BlockSpec((tm, B_WIDTH), lambda i: (i, 0)),
        out_shape=jax.ShapeDtypeStruct((rows, B_WIDTH), BF16),
        compiler_params=_params(dimension_semantics=("arbitrary",)),
        name="gmlp_mid",
    )(z, z, vn_g, vn_b, ws, bs_t)


def _diff_attn_kernel(lq1_ref, lk1_ref, lq2_ref, lk2_ref, sg_ref, q_ref, k_ref, v_ref, kc_ref, vc_ref, o_ref, vt_ref,
                      *, lam_init, tq, tk):
    qi = pl.program_id(2)
    n_lat = SEQ // tq
    expo = (HEAD_DIM ** -0.5) * math.log2(math.e)
    lam = (jnp.exp(jnp.sum(lq1_ref[...] * lk1_ref[...], axis=-1, keepdims=True))
           - jnp.exp(jnp.sum(lq2_ref[...] * lk2_ref[...], axis=-1, keepdims=True)) + lam_init)

    @pl.when(qi == 0)
    def _():
        vt_ref[:, :CTX_LEN] = vc_ref[...].T
        for t in range(SEQ // tk):
            vt_ref[:, CTX_LEN + t * tk:CTX_LEN + (t + 1) * tk] = v_ref[t * tk:(t + 1) * tk, :].T

    def run(chunks):
        qs = [q_ref[:, c * HEAD_DIM:(c + 1) * HEAD_DIM] for c in range(2)]

        def scores(chunk):
            kref, lo, size, _ = chunk
            return [_dot_t(kref[lo:lo + size, c * HEAD_DIM:(c + 1) * HEAD_DIM], qs[c]) for c in range(2)]

        m = [None, None]
        l = [None, None]
        acc = [None, None]
        s_next = scores(chunks[0])
        for t, (_, _, size, vlo) in enumerate(chunks):
            s_cur = s_next
            if t + 1 < len(chunks):
                s_next = scores(chunks[t + 1])
            for c in range(2):
                st = s_cur[c]
                m_new = jnp.max(st, axis=0, keepdims=True)
                if m[c] is not None:
                    m_new = jnp.maximum(m[c], m_new)
                p = jnp.exp2((st - m_new) * expo)
                pv = jnp.dot(vt_ref[:, vlo:vlo + size], p.astype(BF16), preferred_element_type=F32)
                if m[c] is None:
                    l[c] = jnp.sum(p, axis=0, keepdims=True)
                    acc[c] = pv
                else:
                    alpha = jnp.exp2((m[c] - m_new) * expo)
                    l[c] = alpha * l[c] + jnp.sum(p, axis=0, keepdims=True)
                    acc[c] = alpha * acc[c] + pv
                m[c] = m_new
        o = acc[0] / l[0] - lam * (acc[1] / l[1])
        o = o * lax.rsqrt(jnp.mean(o * o, axis=0, keepdims=True) + NORM_EPS) * (sg_ref[...] * (1.0 - lam_init))
        o_ref[...] = o.T.astype(BF16)

    @pl.when(qi < n_lat)
    def _():
        run([(kc_ref, 0, CTX_LEN, 0)] + [(k_ref, t * tk, tk, CTX_LEN + t * tk) for t in range(SEQ // tk)])

    @pl.when(qi >= n_lat)
    def _():
        run([(kc_ref, 0, CTX_LEN, 0)])


def _diff_attn(qkv, lq1, lk1, lq2, lk2, subln_g_col, lam_init, tq=256, tk=1024):
    hw = 2 * HEAD_DIM
    n_lat = SEQ // tq
    n_ctx = CTX_LEN // tq
    kcol = D_MODEL // hw
    ctx_blk0 = LAT_ROWS // tq

    def qrow(b, qi):
        return jnp.where(qi < n_lat, b * n_lat + qi, ctx_blk0 + b * n_ctx + (qi - n_lat))

    vec = pl.BlockSpec((1, HEAD_DIM), lambda b, h, qi: (0, 0))
    kern = functools.partial(_diff_attn_kernel, lam_init=lam_init, tq=tq, tk=tk)
    return pl.pallas_call(
        kern,
        grid=(BATCH, C_HEADS, n_lat + n_ctx),
        in_specs=[
            vec, vec, vec, vec,
            pl.BlockSpec((hw, 1), lambda b, h, qi: (0, 0)),
            pl.BlockSpec((tq, hw), lambda b, h, qi: (qrow(b, qi), h)),
            pl.BlockSpec((SEQ, hw), lambda b, h, qi: (b, kcol + h)),
            pl.BlockSpec((SEQ, hw), lambda b, h, qi: (b, 2 * kcol + h)),
            pl.BlockSpec((CTX_LEN, hw), lambda b, h, qi: (LAT_ROWS // CTX_LEN + b, kcol + h)),
            pl.BlockSpec((CTX_LEN, hw), lambda b, h, qi: (LAT_ROWS // CTX_LEN + b, 2 * kcol + h)),
        ],
        out_specs=pl.BlockSpec((tq, hw), lambda b, h, qi: (qrow(b, qi), h)),
        out_shape=jax.ShapeDtypeStruct((ROWS, D_MODEL), BF16),
        scratch_shapes=[pltpu.VMEM((hw, CTX_LEN + SEQ), BF16)],
        compiler_params=_params(dimension_semantics=("arbitrary", "arbitrary", "arbitrary")),
        name="diff_attn",
    )(lq1, lk1, lq2, lk2, subln_g_col, qkv, qkv, qkv, qkv, qkv)


def _rope_tables():
    rows = SEQ // GRID_W
    row = jnp.repeat(jnp.arange(rows, dtype=F32), GRID_W)
    col = jnp.tile(jnp.arange(GRID_W, dtype=F32), rows)
    axis_dim = HEAD_DIM // 2
    inv = ROPE_THETA ** (-jnp.arange(0, axis_dim, 2, dtype=F32) / axis_dim)
    ang = jnp.concatenate([row[:, None] * inv, col[:, None] * inv], axis=-1)
    cos = jnp.repeat(jnp.cos(ang), 2, axis=-1)
    sin = jnp.repeat(jnp.sin(ang), 2, axis=-1) * jnp.tile(jnp.array([-1.0, 1.0], F32), HEAD_DIM // 2)
    cos_t = jnp.concatenate([jnp.tile(cos, (BATCH, 1)), jnp.ones((CTX_ROWS, HEAD_DIM), F32)], axis=0)
    sin_t = jnp.concatenate([jnp.tile(sin, (BATCH, 1)), jnp.zeros((CTX_ROWS, HEAD_DIM), F32)], axis=0)
    return cos_t, sin_t


def _param_tiles(mods, norm_g):
    m = mods[:, :N_GROUPS].reshape(DEPTH, N_GROUPS, N_MOD, D_MODEL)
    zero = jnp.zeros((N_GROUPS, D_MODEL), F32)
    bcast = lambda v: jnp.broadcast_to(v, (N_GROUPS, D_MODEL))
    tiles = []
    for s in range(-1, 3 * DEPTH):
        rows = [zero] * 8
        if s >= 0:
            i, slot = divmod(s, 3)
            rows[P_POST_G] = bcast(norm_g[i, 2 * slot + 1])
            rows[P_GATE] = m[i, :, 3 * slot + 2]
        if s + 1 < 3 * DEPTH:
            i, slot = divmod(s + 1, 3)
            rows[P_PRE_G] = bcast(norm_g[i, 2 * slot])
            rows[P_SCALE] = m[i, :, 3 * slot + 1]
            rows[P_SHIFT] = m[i, :, 3 * slot]
        tiles.append(jnp.stack(rows, axis=1))
    return jnp.stack(tiles, axis=0)


def kernel(x, c, ctx, c_ctx, ada_w, ada_b, norm_g, ffn_w_in, ffn_w_out, a_w_in, a_w_out, a_sink, b_w_in, b_vnorm_g,
           b_vnorm_b, b_ws, b_bs, b_w_out, c_w_in, c_w_out, c_lq1, c_lk1, c_lq2, c_lk2, c_subln_g):
    xs = jnp.concatenate([x.reshape(LAT_ROWS, D_MODEL), ctx.reshape(CTX_ROWS, D_MODEL)], axis=0)
    cvec = jnp.concatenate([c, c_ctx[None, :], jnp.zeros((8 - BATCH - 1, D_MODEL), F32)], axis=0)
    pp = _param_tiles(_ada_mods(cvec, ada_w, ada_b), norm_g)
    cos_t, sin_t = _rope_tables()

    mixer_w = ((a_w_in, a_w_out), (b_w_in, b_w_out), (c_w_in, c_w_out))
    wgu = _convert(_ffn_split_job(ffn_w_in, (0, 0)))
    h = _prenorm(xs, pp, 0)
    for i in range(DEPTH):
        kind, j = i % N_MIXERS, i // N_MIXERS
        ctx_live = i < DEPTH - 1
        rows_out = ROWS if ctx_live else LAT_ROWS
        s0 = 3 * i + 1
        w_mix_in, w_mix_out = mixer_w[kind]

        a, (wdn, w_i) = _ffn_in(h, wgu, [_cast_job(ffn_w_out, (i, 0), D_FF_PAD), _cast_job(w_mix_in, (j,))])
        xs, h = _mm_out(a, wdn, xs, pp, s0, rows=ROWS, coef=0.5, want_h=True)

        jobs = [_cast_job(w_mix_out, (j,)), _ffn_split_job(ffn_w_in, (i, 1))]
        if kind == 0:
            qkv, (w_o, wgu) = _qkv_rope(h, w_i, cos_t, sin_t, (A_HEADS + A_KV_HEADS) * HEAD_DIM, jobs)
            o = _win_attn(qkv, a_sink[j], ctx_live)
        elif kind == 1:
            z, (w_o, wgu) = _gmlp_in(h, w_i, jobs)
            o = _gmlp_mid(z, b_vnorm_g[j][None, :], b_vnorm_b[j][None, :], b_ws[j], b_bs[j].T)
        else:
            lam_init = 0.8 - 0.6 * math.exp(-0.3 * i)
            qkv, (w_o, wgu) = _qkv_rope(h, w_i, cos_t, sin_t, 2 * D_MODEL, jobs)
            o = _diff_attn(qkv, c_lq1[j][None, :], c_lk1[j][None, :], c_lq2[j][None, :], c_lk2[j][None, :],
                           c_subln_g[j][:, None], lam_init)
        xs, h = _mm_out(o, w_o, xs, pp, s0 + 1, rows=rows_out, coef=1.0, want_h=True)

        jobs = [_cast_job(ffn_w_out, (i, 1), D_FF_PAD)]
        if i + 1 < DEPTH:
            jobs.append(_ffn_split_job(ffn_w_in, (i + 1, 0)))
        a, conv = _ffn_in(h, wgu, jobs)
        xs, h = _mm_out(a, conv[0], xs, pp, s0 + 2, rows=rows_out, coef=0.5, want_h=i + 1 < DEPTH)
        wgu = conv[1] if i + 1 < DEPTH else None
    return xs.reshape(BATCH, SEQ, D_MODEL)
```

```python
import functools
import math
from typing import Callable, NamedTuple

import jax
import jax.numpy as jnp
from jax import lax
from jax.experimental import pallas as pl
from jax.experimental.pallas import tpu as pltpu

D_MODEL = 2048
BATCH = 2
SEQ = 4096
DEPTH = 4
GRID_W = 64
CTX_LEN = 256
N_MIXERS = 3
N_MOD = 9
NORM_EPS = 1e-6
ROPE_THETA = 10000.0
NEG_INF = -1e30
D_FF = 5504
HEAD_DIM = 128
A_HEADS = 16
A_KV_HEADS = 4
A_GROUP = 4
A_BLOCK = 128
B_CHUNK = 128
B_WIDTH = 3 * D_MODEL
B_GROUPS = 8
B_GROUP_W = B_WIDTH // B_GROUPS
C_HEADS = 8

LAT_ROWS = BATCH * SEQ
CTX_ROWS = BATCH * CTX_LEN
ROWS = LAT_ROWS + CTX_ROWS
N_GROUPS = 3
LANES = 128
BF16_SUBLANES = 16
MXU_WIDTH = 256
IN_ROW_TILES = 8
FF_TILE = 512
D_FF_PAD = -(-D_FF // FF_TILE) * FF_TILE
VMEM_LIMIT = 56 * 1024 * 1024

BF16 = jnp.bfloat16
F32 = jnp.float32

P_POST_G, P_GATE, P_PRE_G, P_SCALE, P_SHIFT = 0, 1, 2, 3, 4


def _params(**kw):
    return pltpu.CompilerParams(vmem_limit_bytes=VMEM_LIMIT, **kw)


def _group_of_block(i, tm):
    return jnp.minimum(i // (SEQ // tm), N_GROUPS - 1)


def _rms(x):
    return x * lax.rsqrt(jnp.mean(x * x, axis=-1, keepdims=True) + NORM_EPS)


def _ada_kernel(c_ref, w_ref, b_ref, o_ref):
    @pl.when(pl.program_id(1) == 0)
    def _():
        o_ref[...] = jnp.broadcast_to(b_ref[...], o_ref.shape)

    c = c_ref[...]
    a = (c * jax.nn.sigmoid(c)).astype(BF16)
    o_ref[...] += jnp.dot(a, w_ref[...].astype(BF16), preferred_element_type=F32)


def _ada_mods(cvec, ada_w, ada_b):
    kb = LANES
    n = N_MOD * D_MODEL
    return pl.pallas_call(
        _ada_kernel,
        grid=(DEPTH, D_MODEL // kb),
        in_specs=[
            pl.BlockSpec((8, kb), lambda l, k: (0, k)),
            pl.BlockSpec((None, kb, n), lambda l, k: (l, k, 0)),
            pl.BlockSpec((None, 1, n), lambda l, k: (l, 0, 0)),
        ],
        out_specs=pl.BlockSpec((None, 8, n), lambda l, k: (l, 0, 0)),
        out_shape=jax.ShapeDtypeStruct((DEPTH, 8, n), F32),
        compiler_params=_params(dimension_semantics=("arbitrary", "arbitrary")),
        name="ada_mods",
    )(cvec, ada_w, ada_b.reshape(DEPTH, 1, n))


def _pre(x, p_ref):
    gain = p_ref[P_PRE_G:P_PRE_G + 1, :] * (1.0 + p_ref[P_SCALE:P_SCALE + 1, :])
    return _rms(x) * gain + p_ref[P_SHIFT:P_SHIFT + 1, :]


def _prenorm_kernel(x_ref, p_ref, h_ref):
    h_ref[...] = _pre(x_ref[...], p_ref).astype(BF16)


def _prenorm(x, pp, s, tm=512):
    rows = x.shape[0]
    return pl.pallas_call(
        _prenorm_kernel,
        grid=(rows // tm,),
        in_specs=[
            pl.BlockSpec((tm, D_MODEL), lambda i: (i, 0)),
            pl.BlockSpec((None, None, 8, D_MODEL), lambda i: (s, _group_of_block(i, tm), 0, 0)),
        ],
        out_specs=pl.BlockSpec((tm, D_MODEL), lambda i: (i, 0)),
        out_shape=jax.ShapeDtypeStruct((rows, D_MODEL), BF16),
        compiler_params=_params(dimension_semantics=("arbitrary",)),
        name="prenorm",
    )(x, pp)


class _Job(NamedTuple):
    src: jax.Array
    in_spec: pl.BlockSpec
    out_spec: pl.BlockSpec
    out_shape: jax.ShapeDtypeStruct
    body: Callable
    n_blocks: int


def _row_block(rows, rows_out, steps):
    g = math.gcd(rows, rows_out)
    for rb in range(BF16_SUBLANES, g + 1, BF16_SUBLANES):
        if g % rb == 0 and rows_out // rb <= steps:
            return rb
    raise ValueError(f"no row block for {rows}->{rows_out} rows in {steps} steps")


def _cast_job(src, lead, rows_out=None):
    rows, cols = src.shape[-2:]
    rows_out = rows_out or rows

    def make(steps, lin):
        rb = _row_block(rows, rows_out, steps)
        nb_in, nb_out = rows // rb, rows_out // rb

        def body(src_ref, dst_ref, blk):
            v = src_ref[...].astype(BF16)
            if nb_out > nb_in:
                v = jnp.where(blk < nb_in, v, jnp.zeros_like(v))
            dst_ref[...] = v

        return _Job(
            src,
            pl.BlockSpec((None,) * len(lead) + (rb, cols),
                         lambda *g: lead + (jnp.minimum(lin(*g), nb_in - 1), 0)),
            pl.BlockSpec((rb, cols), lambda *g: (jnp.minimum(lin(*g), nb_out - 1), 0)),
            jax.ShapeDtypeStruct((rows_out, cols), BF16), body, nb_out)

    return make


def _ffn_split_job(ffn_w_in, lead):
    def make(steps, lin):
        rb = _row_block(D_MODEL, D_MODEL, steps)
        nb = D_MODEL // rb

        def body(src_ref, dst_ref, blk):
            del blk
            for part in range(2):
                dst_ref[part, :, :D_FF] = src_ref[:, part * D_FF:(part + 1) * D_FF].astype(BF16)
                dst_ref[part, :, D_FF:] = jnp.zeros((rb, D_FF_PAD - D_FF), BF16)

        return _Job(
            ffn_w_in,
            pl.BlockSpec((None,) * len(lead) + (rb, 2 * D_FF), lambda *g: lead + (jnp.minimum(lin(*g), nb - 1), 0)),
            pl.BlockSpec((2, rb, D_FF_PAD), lambda *g: (0, jnp.minimum(lin(*g), nb - 1), 0)),
            jax.ShapeDtypeStruct((2, D_MODEL, D_FF_PAD), BF16), body, nb)

    return make


def _hosted_call(main, main_args, main_in_specs, out_spec, out_shape, grid, job_makers, name):
    steps = math.prod(grid)
    strides = [math.prod(grid[d + 1:]) for d in range(len(grid))]
    lin = lambda *g: sum(gi * st for gi, st in zip(g, strides))
    jobs = [mk(steps, lin) for mk in job_makers]
    n_in, n_jobs = len(main_args), len(jobs)

    def kern(*refs):
        if main is not None:
            main(*refs[:n_in], refs[n_in + n_jobs])
        t = lin(*[pl.program_id(d) for d in range(len(grid))])
        n_main_out = 0 if main is None else 1
        for q, jb in enumerate(jobs):
            jb.body(refs[n_in + q], refs[n_in + n_jobs + n_main_out + q], jnp.minimum(t, jb.n_blocks - 1))

    main_out = [] if main is None else [(out_spec, out_shape)]
    res = pl.pallas_call(
        kern,
        grid=grid,
        in_specs=list(main_in_specs) + [jb.in_spec for jb in jobs],
        out_specs=[s for s, _ in main_out] + [jb.out_spec for jb in jobs],
        out_shape=[s for _, s in main_out] + [jb.out_shape for jb in jobs],
        compiler_params=_params(dimension_semantics=("arbitrary",) * len(grid)),
        name=name,
    )(*main_args, *[jb.src for jb in jobs])
    return (None, list(res)) if main is None else (res[0], list(res[1:]))


def _convert(job_maker, steps=32):
    return _hosted_call(None, (), (), None, None, (steps,), [job_maker], "convert")[1][0]


def _swiglu_kernel(h_ref, wg_ref, wu_ref, o_ref):
    h = h_ref[...]
    g = jnp.dot(h, wg_ref[...], preferred_element_type=F32)
    u = jnp.dot(h, wu_ref[...], preferred_element_type=F32)
    o_ref[...] = (g * jax.nn.sigmoid(g) * u).astype(BF16)


def _ffn_in(h, wgu, jobs, tn=FF_TILE):
    rows = h.shape[0]
    tm = rows // IN_ROW_TILES
    return _hosted_call(
        _swiglu_kernel, (h, wgu, wgu),
        [pl.BlockSpec((tm, D_MODEL), lambda j, i: (i, 0)),
         pl.BlockSpec((None, D_MODEL, tn), lambda j, i: (0, 0, j)),
         pl.BlockSpec((None, D_MODEL, tn), lambda j, i: (1, 0, j))],
        pl.BlockSpec((tm, tn), lambda j, i: (i, j)),
        jax.ShapeDtypeStruct((rows, D_FF_PAD), BF16),
        (D_FF_PAD // tn, rows // tm), jobs, "ffn_in")


def _gelu_kernel(h_ref, w_ref, o_ref):
    y = jnp.dot(h_ref[...], w_ref[...], preferred_element_type=F32)
    o_ref[...] = jax.nn.gelu(y).astype(BF16)


def _gmlp_in(h, w, jobs, tn=1024):
    rows = h.shape[0]
    tm = rows // IN_ROW_TILES
    n = w.shape[1]
    return _hosted_call(
        _gelu_kernel, (h, w),
        [pl.BlockSpec((tm, D_MODEL), lambda j, i: (i, 0)),
         pl.BlockSpec((D_MODEL, tn), lambda j, i: (0, j))],
        pl.BlockSpec((tm, tn), lambda j, i: (i, j)),
        jax.ShapeDtypeStruct((rows, n), BF16),
        (n // tn, rows // tm), jobs, "gmlp_in")


def _qkv_rope_kernel(h_ref, w_ref, cos_ref, sin_ref, swap_ref, o_ref, *, n_rope_tiles, tn):
    j = pl.program_id(0)

    @pl.when(j < n_rope_tiles)
    def _():
        h = h_ref[...]
        c = cos_ref[...]
        s = sin_ref[...]
        swap = swap_ref[...]
        y = jnp.dot(h, w_ref[...], preferred_element_type=F32)
        hi = y.astype(BF16)
        lo = (y - hi.astype(F32)).astype(BF16)
        for q0 in range(0, tn, MXU_WIDTH):
            cs = slice(q0, q0 + MXU_WIDTH)
            partner = (jnp.dot(hi[:, cs], swap, preferred_element_type=F32)
                       + jnp.dot(lo[:, cs], swap, preferred_element_type=F32))
            o_ref[:, cs] = (y[:, cs] * c + partner * s).astype(BF16)

    @pl.when(j >= n_rope_tiles)
    def _():
        o_ref[...] = jnp.dot(h_ref[...], w_ref[...], preferred_element_type=F32).astype(BF16)


def _qkv_rope(h, w, cos_t, sin_t, rope_width, jobs, tn=512):
    rows = h.shape[0]
    tm = rows // IN_ROW_TILES
    n = w.shape[1]
    kern = functools.partial(_qkv_rope_kernel, n_rope_tiles=rope_width // tn, tn=tn)
    lane = jnp.arange(MXU_WIDTH)
    swap = (lane[:, None] == (lane[None, :] ^ 1)).astype(BF16)
    return _hosted_call(
        kern, (h, w, cos_t, sin_t, swap),
        [pl.BlockSpec((tm, D_MODEL), lambda j, i: (i, 0)),
         pl.BlockSpec((D_MODEL, tn), lambda j, i: (0, j)),
         pl.BlockSpec((tm, MXU_WIDTH), lambda j, i: (i, 0)),
         pl.BlockSpec((tm, MXU_WIDTH), lambda j, i: (i, 0)),
         pl.BlockSpec((MXU_WIDTH, MXU_WIDTH), lambda j, i: (0, 0))],
        pl.BlockSpec((tm, tn), lambda j, i: (i, j)),
        jax.ShapeDtypeStruct((rows, n), BF16),
        (n // tn, rows // tm), jobs, "qkv_rope")


def _mm_out_kernel(a_ref, w_ref, x_ref, p_ref, xo_ref, *maybe_h_ref, coef):
    post = coef * (p_ref[P_GATE:P_GATE + 1, :] * p_ref[P_POST_G:P_POST_G + 1, :])
    half = a_ref.shape[0] // 2
    for rs in (slice(0, half), slice(half, 2 * half)):
        y = jnp.dot(a_ref[rs, :], w_ref[...], preferred_element_type=F32)
        xn = x_ref[rs, :] + _rms(y) * post
        xo_ref[rs, :] = xn
        if maybe_h_ref:
            maybe_h_ref[0][rs, :] = _pre(xn, p_ref).astype(BF16)


def _mm_out(a, w, x, pp, s, *, rows, coef, want_h):
    k = a.shape[1]
    tm = 512 if k <= D_MODEL else 256
    out_shape = [jax.ShapeDtypeStruct((rows, D_MODEL), F32)]
    out_specs = [pl.BlockSpec((tm, D_MODEL), lambda i: (i, 0))]
    if want_h:
        out_shape.append(jax.ShapeDtypeStruct((rows, D_MODEL), BF16))
        out_specs.append(pl.BlockSpec((tm, D_MODEL), lambda i: (i, 0)))
    res = pl.pallas_call(
        functools.partial(_mm_out_kernel, coef=coef),
        grid=(rows // tm,),
        in_specs=[
            pl.BlockSpec((tm, k), lambda i: (i, 0)),
            pl.BlockSpec((k, D_MODEL), lambda i: (0, 0), pipeline_mode=pl.Buffered(1)),
            pl.BlockSpec((tm, D_MODEL), lambda i: (i, 0)),
            pl.BlockSpec((None, None, 8, D_MODEL), lambda i: (s, _group_of_block(i, tm), 0, 0)),
        ],
        out_specs=out_specs,
        out_shape=out_shape,
        compiler_params=_params(dimension_semantics=("arbitrary",)),
        name="mm_out",
    )(a, w, x, pp)
    return (res[0], res[1]) if want_h else (res[0], None)


def _dot_t(a, b):
    return lax.dot_general(a, b, (((1,), (1,)), ((), ())), preferred_element_type=F32)


def _dot_tn(a, b):
    return lax.dot_general(a, b, (((0,), (0,)), ((), ())), preferred_element_type=F32)


def _win_attn_kernel(sink_ref, q_ref, kv_ref, kvc_ref, o_ref):
    n = pl.program_id(1)
    n_lat = SEQ // A_BLOCK
    kvw = A_KV_HEADS * HEAD_DIM
    scale = HEAD_DIM ** -0.5
    expo = scale * math.log2(math.e)
    gq = A_GROUP * A_BLOCK
    lane = lax.broadcasted_iota(jnp.int32, (1, gq), 1)
    qi = lax.broadcasted_iota(jnp.int32, (A_BLOCK, gq), 1) & (A_BLOCK - 1)
    mi = lax.broadcasted_iota(jnp.int32, (A_BLOCK, gq), 0)

    def heads(kk):
        q4 = jnp.concatenate(
            [q_ref[:, (kk * A_GROUP + g) * HEAD_DIM:(kk * A_GROUP + g + 1) * HEAD_DIM] for g in range(A_GROUP)],
            axis=0)
        sink = jnp.full((1, gq), sink_ref[kk * A_GROUP] / scale, F32)
        for g in range(1, A_GROUP):
            sink = jnp.where(lane >= g * A_BLOCK, sink_ref[kk * A_GROUP + g] / scale, sink)
        return q4, sink, slice(kk * HEAD_DIM, (kk + 1) * HEAD_DIM), slice(kvw + kk * HEAD_DIM, kvw + (kk + 1) * HEAD_DIM)

    def finish(kk, pieces, sink):
        m = sink
        for s, _ in pieces:
            m = jnp.maximum(m, jnp.max(s, axis=0, keepdims=True))
        den = jnp.exp2((sink - m) * expo)
        ot = None
        for s, v in pieces:
            p = jnp.exp2((s - m) * expo)
            den = den + jnp.sum(p, axis=0, keepdims=True)
            pv = _dot_tn(v, p.astype(BF16))
            ot = pv if ot is None else ot + pv
        ot = ot / den
        for g in range(A_GROUP):
            h = kk * A_GROUP + g
            o_ref[:, h * HEAD_DIM:(h + 1) * HEAD_DIM] = ot[:, g * A_BLOCK:(g + 1) * A_BLOCK].T.astype(BF16)

    @pl.when(n < n_lat)
    def _():
        off_l = jnp.where(n > 0, 0, 2 * A_BLOCK)
        off_r = jnp.where(n < n_lat - 1, 0, 2 * A_BLOCK)
        ok_l = mi >= qi + off_l
        ok_r = mi <= qi - off_r
        rows_l, rows_m, rows_r = [pl.ds(pl.multiple_of(jnp.clip(n + d, 0, n_lat - 1) * A_BLOCK, A_BLOCK), A_BLOCK)
                                  for d in (-1, 0, 1)]
        for kk in range(A_KV_HEADS):
            q4, sink, kc, vc = heads(kk)
            s_l = jnp.where(ok_l, _dot_t(kv_ref[rows_l, kc], q4), NEG_INF)
            s_r = jnp.where(ok_r, _dot_t(kv_ref[rows_r, kc], q4), NEG_INF)
            finish(kk, [(_dot_t(kvc_ref[:, kc], q4), kvc_ref[:, vc]), (s_l, kv_ref[rows_l, vc]),
                        (_dot_t(kv_ref[rows_m, kc], q4), kv_ref[rows_m, vc]), (s_r, kv_ref[rows_r, vc])], sink)

    @pl.when(n >= n_lat)
    def _():
        for kk in range(A_KV_HEADS):
            q4, sink, kc, vc = heads(kk)
            finish(kk, [(_dot_t(kvc_ref[:, kc], q4), kvc_ref[:, vc])], sink)


def _win_attn(qkv, sink, with_ctx):
    n_lat = SEQ // A_BLOCK
    n_ctx = CTX_LEN // A_BLOCK
    steps = n_lat + (n_ctx if with_ctx else 0)
    qw = A_HEADS * HEAD_DIM
    kvw2 = 2 * A_KV_HEADS * HEAD_DIM
    ctx_blk0 = LAT_ROWS // A_BLOCK

    def qrow(b, n):
        return jnp.where(n < n_lat, b * n_lat + n, ctx_blk0 + b * n_ctx + (n - n_lat))

    rows = LAT_ROWS + (CTX_ROWS if with_ctx else 0)
    return pl.pallas_call(
        _win_attn_kernel,
        grid=(BATCH, steps),
        in_specs=[
            pl.BlockSpec(memory_space=pltpu.SMEM),
            pl.BlockSpec((A_BLOCK, qw), lambda b, n: (qrow(b, n), 0)),
            pl.BlockSpec((SEQ, kvw2), lambda b, n: (b, qw // kvw2)),
            pl.BlockSpec((CTX_LEN, kvw2), lambda b, n: (LAT_ROWS // CTX_LEN + b, qw // kvw2)),
        ],
        out_specs=pl.BlockSpec((A_BLOCK, D_MODEL), lambda b, n: (qrow(b, n), 0)),
        out_shape=jax.ShapeDtypeStruct((rows, D_MODEL), BF16),
        compiler_params=_params(dimension_semantics=("arbitrary", "arbitrary")),
        name="win_attn",
    )(sink, qkv, qkv, qkv)


def _gmlp_mid_kernel(u_ref, v_ref, g_ref, b_ref, ws_ref, bs_ref, o_ref, *, tm):
    v = v_ref[...].astype(F32)
    mu = jnp.mean(v, axis=-1, keepdims=True)
    vc = v - mu
    vn = vc * lax.rsqrt(jnp.mean(vc * vc, axis=-1, keepdims=True) + NORM_EPS)
    vn = (vn * g_ref[...] + b_ref[...]).astype(BF16)
    for g in range(B_GROUPS):
        cs = slice(g * B_GROUP_W, (g + 1) * B_GROUP_W)
        ws = ws_ref[g].astype(BF16)
        for c in range(tm // B_CHUNK):
            rs = slice(c * B_CHUNK, (c + 1) * B_CHUNK)
            mixed = jnp.dot(ws, vn[rs, cs], preferred_element_type=F32) + bs_ref[:, g:g + 1]
            o_ref[rs, cs] = (u_ref[rs, cs].astype(F32) * mixed).astype(BF16)


def _gmlp_mid(z, vn_g, vn_b, ws, bs_t, tm=256):
    rows = z.shape[0]
    return pl.pallas_call(
        functools.partial(_gmlp_mid_kernel, tm=tm),
        grid=(rows // tm,),
        in_specs=[
            pl.BlockSpec((tm, B_WIDTH), lambda i: (i, 0)),
            pl.BlockSpec((tm, B_WIDTH), lambda i: (i, 1)),
            pl.BlockSpec((1, B_WIDTH), lambda i: (0, 0)),
            pl.BlockSpec((1, B_WIDTH), lambda i: (0, 0)),
            pl.BlockSpec((B_GROUPS, B_CHUNK, B_CHUNK), lambda i: (0, 0, 0)),
            pl.BlockSpec((B_CHUNK, B_GROUPS), lambda i: (0, 0)),
        ],
        out_specs=pl.BlockSpec((tm, B_WIDTH), lambda i: (i, 0)),
        out_shape=jax.ShapeDtypeStruct((rows, B_WIDTH), BF16),
        compiler_params=_params(dimension_semantics=("arbitrary",)),
        name="gmlp_mid",
    )(z, z, vn_g, vn_b, ws, bs_t)


def _diff_attn_kernel(lq1_ref, lk1_ref, lq2_ref, lk2_ref, sg_ref, q_ref, k_ref, v_ref, kc_ref, vc_ref, o_ref, vt_ref,
                      *, lam_init, tq, tk):
    qi = pl.program_id(2)
    n_lat = SEQ // tq
    expo = (HEAD_DIM ** -0.5) * math.log2(math.e)
    lam = (jnp.exp(jnp.sum(lq1_ref[...] * lk1_ref[...], axis=-1, keepdims=True))
           - jnp.exp(jnp.sum(lq2_ref[...] * lk2_ref[...], axis=-1, keepdims=True)) + lam_init)

    @pl.when(qi == 0)
    def _():
        vt_ref[:, :CTX_LEN] = vc_ref[...].T
        for t in range(SEQ // tk):
            vt_ref[:, CTX_LEN + t * tk:CTX_LEN + (t + 1) * tk] = v_ref[t * tk:(t + 1) * tk, :].T

    def run(chunks):
        qs = [q_ref[:, c * HEAD_DIM:(c + 1) * HEAD_DIM] for c in range(2)]

        def scores(chunk):
            kref, lo, size, _ = chunk
            return [_dot_t(kref[lo:lo + size, c * HEAD_DIM:(c + 1) * HEAD_DIM], qs[c]) for c in range(2)]

        m = [None, None]
        l = [None, None]
        acc = [None, None]
        s_next = scores(chunks[0])
        for t, (_, _, size, vlo) in enumerate(chunks):
            s_cur = s_next
            if t + 1 < len(chunks):
                s_next = scores(chunks[t + 1])
            for c in range(2):
                st = s_cur[c]
                m_new = jnp.max(st, axis=0, keepdims=True)
                if m[c] is not None:
                    m_new = jnp.maximum(m[c], m_new)
                p = jnp.exp2((st - m_new) * expo)
                pv = jnp.dot(vt_ref[:, vlo:vlo + size], p.astype(BF16), preferred_element_type=F32)
                if m[c] is None:
                    l[c] = jnp.sum(p, axis=0, keepdims=True)
                    acc[c] = pv
                else:
                    alpha = jnp.exp2((m[c] - m_new) * expo)
                    l[c] = alpha * l[c] + jnp.sum(p, axis=0, keepdims=True)
                    acc[c] = alpha * acc[c] + pv
                m[c] = m_new
        o = acc[0] / l[0] - lam * (acc[1] / l[1])
        o = o * lax.rsqrt(jnp.mean(o * o, axis=0, keepdims=True) + NORM_EPS) * (sg_ref[...] * (1.0 - lam_init))
        o_ref[...] = o.T.astype(BF16)

    @pl.when(qi < n_lat)
    def _():
        run([(kc_ref, 0, CTX_LEN, 0)] + [(k_ref, t * tk, tk, CTX_LEN + t * tk) for t in range(SEQ // tk)])

    @pl.when(qi >= n_lat)
    def _():
        run([(kc_ref, 0, CTX_LEN, 0)])


def _diff_attn(qkv, lq1, lk1, lq2, lk2, subln_g_col, lam_init, tq=256, tk=1024):
    hw = 2 * HEAD_DIM
    n_lat = SEQ // tq
    n_ctx = CTX_LEN // tq
    kcol = D_MODEL // hw
    ctx_blk0 = LAT_ROWS // tq

    def qrow(b, qi):
        return jnp.where(qi < n_lat, b * n_lat + qi, ctx_blk0 + b * n_ctx + (qi - n_lat))

    vec = pl.BlockSpec((1, HEAD_DIM), lambda b, h, qi: (0, 0))
    kern = functools.partial(_diff_attn_kernel, lam_init=lam_init, tq=tq, tk=tk)
    return pl.pallas_call(
        kern,
        grid=(BATCH, C_HEADS, n_lat + n_ctx),
        in_specs=[
            vec, vec, vec, vec,
            pl.BlockSpec((hw, 1), lambda b, h, qi: (0, 0)),
            pl.BlockSpec((tq, hw), lambda b, h, qi: (qrow(b, qi), h)),
            pl.BlockSpec((SEQ, hw), lambda b, h, qi: (b, kcol + h)),
            pl.BlockSpec((SEQ, hw), lambda b, h, qi: (b, 2 * kcol + h)),
            pl.BlockSpec((CTX_LEN, hw), lambda b, h, qi: (LAT_ROWS // CTX_LEN + b, kcol + h)),
            pl.BlockSpec((CTX_LEN, hw), lambda b, h, qi: (LAT_ROWS // CTX_LEN + b, 2 * kcol + h)),
        ],
        out_specs=pl.BlockSpec((tq, hw), lambda b, h, qi: (qrow(b, qi), h)),
        out_shape=jax.ShapeDtypeStruct((ROWS, D_MODEL), BF16),
        scratch_shapes=[pltpu.VMEM((hw, CTX_LEN + SEQ), BF16)],
        compiler_params=_params(dimension_semantics=("arbitrary", "arbitrary", "arbitrary")),
        name="diff_attn",
    )(lq1, lk1, lq2, lk2, subln_g_col, qkv, qkv, qkv, qkv, qkv)


def _rope_tables():
    rows = SEQ // GRID_W
    row = jnp.repeat(jnp.arange(rows, dtype=F32), GRID_W)
    col = jnp.tile(jnp.arange(GRID_W, dtype=F32), rows)
    axis_dim = HEAD_DIM // 2
    inv = ROPE_THETA ** (-jnp.arange(0, axis_dim, 2, dtype=F32) / axis_dim)
    ang = jnp.concatenate([row[:, None] * inv, col[:, None] * inv], axis=-1)
    cos = jnp.repeat(jnp.cos(ang), 2, axis=-1)
    sin = jnp.repeat(jnp.sin(ang), 2, axis=-1) * jnp.tile(jnp.array([-1.0, 1.0], F32), HEAD_DIM // 2)
    heads_per_chunk = MXU_WIDTH // HEAD_DIM
    cos_t = jnp.concatenate([jnp.tile(cos, (BATCH, heads_per_chunk)), jnp.ones((CTX_ROWS, MXU_WIDTH), F32)], axis=0)
    sin_t = jnp.concatenate([jnp.tile(sin, (BATCH, heads_per_chunk)), jnp.zeros((CTX_ROWS, MXU_WIDTH), F32)], axis=0)
    return cos_t, sin_t


def _param_tiles(mods, norm_g):
    m = mods[:, :N_GROUPS].reshape(DEPTH, N_GROUPS, N_MOD, D_MODEL)
    zero = jnp.zeros((N_GROUPS, D_MODEL), F32)
    bcast = lambda v: jnp.broadcast_to(v, (N_GROUPS, D_MODEL))
    tiles = []
    for s in range(-1, 3 * DEPTH):
        rows = [zero] * 8
        if s >= 0:
            i, slot = divmod(s, 3)
            rows[P_POST_G] = bcast(norm_g[i, 2 * slot + 1])
            rows[P_GATE] = m[i, :, 3 * slot + 2]
        if s + 1 < 3 * DEPTH:
            i, slot = divmod(s + 1, 3)
            rows[P_PRE_G] = bcast(norm_g[i, 2 * slot])
            rows[P_SCALE] = m[i, :, 3 * slot + 1]
            rows[P_SHIFT] = m[i, :, 3 * slot]
        tiles.append(jnp.stack(rows, axis=1))
    return jnp.stack(tiles, axis=0)


def kernel(x, c, ctx, c_ctx, ada_w, ada_b, norm_g, ffn_w_in, ffn_w_out, a_w_in, a_w_out, a_sink, b_w_in, b_vnorm_g,
           b_vnorm_b, b_ws, b_bs, b_w_out, c_w_in, c_w_out, c_lq1, c_lk1, c_lq2, c_lk2, c_subln_g):
    xs = jnp.concatenate([x.reshape(LAT_ROWS, D_MODEL), ctx.reshape(CTX_ROWS, D_MODEL)], axis=0)
    cvec = jnp.concatenate([c, c_ctx[None, :], jnp.zeros((8 - BATCH - 1, D_MODEL), F32)], axis=0)
    pp = _param_tiles(_ada_mods(cvec, ada_w, ada_b), norm_g)
    cos_t, sin_t = _rope_tables()

    mixer_w = ((a_w_in, a_w_out), (b_w_in, b_w_out), (c_w_in, c_w_out))
    wgu = _convert(_ffn_split_job(ffn_w_in, (0, 0)))
    h = _prenorm(xs, pp, 0)
    for i in range(DEPTH):
        kind, j = i % N_MIXERS, i // N_MIXERS
        ctx_live = i < DEPTH - 1
        rows_out = ROWS if ctx_live else LAT_ROWS
        s0 = 3 * i + 1
        w_mix_in, w_mix_out = mixer_w[kind]

        a, (wdn, w_i) = _ffn_in(h, wgu, [_cast_job(ffn_w_out, (i, 0), D_FF_PAD), _cast_job(w_mix_in, (j,))])
        xs, h = _mm_out(a, wdn, xs, pp, s0, rows=ROWS, coef=0.5, want_h=True)

        jobs = [_cast_job(w_mix_out, (j,)), _ffn_split_job(ffn_w_in, (i, 1))]
        if kind == 0:
            qkv, (w_o, wgu) = _qkv_rope(h, w_i, cos_t, sin_t, (A_HEADS + A_KV_HEADS) * HEAD_DIM, jobs)
            o = _win_attn(qkv, a_sink[j], ctx_live)
        elif kind == 1:
            z, (w_o, wgu) = _gmlp_in(h, w_i, jobs)
            o = _gmlp_mid(z, b_vnorm_g[j][None, :], b_vnorm_b[j][None, :], b_ws[j], b_bs[j].T)
        else:
            lam_init = 0.8 - 0.6 * math.exp(-0.3 * i)
            qkv, (w_o, wgu) = _qkv_rope(h, w_i, cos_t, sin_t, 2 * D_MODEL, jobs)
            o = _diff_attn(qkv, c_lq1[j][None, :], c_lk1[j][None, :], c_lq2[j][None, :], c_lk2[j][None, :],
                           c_subln_g[j][:, None], lam_init)
        xs, h = _mm_out(o, w_o, xs, pp, s0 + 1, rows=rows_out, coef=1.0, want_h=True)

        jobs = [_cast_job(ffn_w_out, (i, 1), D_FF_PAD)]
        if i + 1 < DEPTH:
            jobs.append(_ffn_split_job(ffn_w_in, (i + 1, 0)))
        a, conv = _ffn_in(h, wgu, jobs)
        xs, h = _mm_out(a, conv[0], xs, pp, s0 + 2, rows=rows_out, coef=0.5, want_h=i + 1 < DEPTH)
        wgu = conv[1] if i + 1 < DEPTH else None
    return xs.reshape(BATCH, SEQ, D_MODEL)
```

```python
import functools
import math
from typing import Callable, NamedTuple

import jax
import jax.numpy as jnp
from jax import lax
from jax.experimental import pallas as pl
from jax.experimental.pallas import tpu as pltpu

D_MODEL = 2048
BATCH = 2
SEQ = 4096
DEPTH = 4
GRID_W = 64
CTX_LEN = 256
N_MIXERS = 3
N_MOD = 9
NORM_EPS = 1e-6
ROPE_THETA = 10000.0
NEG_INF = -1e30
D_FF = 5504
HEAD_DIM = 128
A_HEADS = 16
A_KV_HEADS = 4
A_GROUP = 4
A_BLOCK = 128
B_CHUNK = 128
B_WIDTH = 3 * D_MODEL
B_GROUPS = 8
B_GROUP_W = B_WIDTH // B_GROUPS
C_HEADS = 8

LAT_ROWS = BATCH * SEQ
CTX_ROWS = BATCH * CTX_LEN
ROWS = LAT_ROWS + CTX_ROWS
N_GROUPS = 3
LANES = 128
BF16_SUBLANES = 16
MXU_WIDTH = 256
IN_ROW_TILES = 8
FF_TILE = 512
D_FF_PAD = -(-D_FF // FF_TILE) * FF_TILE
VMEM_LIMIT = 56 * 1024 * 1024

BF16 = jnp.bfloat16
F32 = jnp.float32

P_POST_G, P_GATE, P_PRE_G, P_SCALE, P_SHIFT = 0, 1, 2, 3, 4


def _params(**kw):
    return pltpu.CompilerParams(vmem_limit_bytes=VMEM_LIMIT, **kw)


def _group_of_block(i, tm):
    return jnp.minimum(i // (SEQ // tm), N_GROUPS - 1)


def _rms(x):
    return x * lax.rsqrt(jnp.mean(x * x, axis=-1, keepdims=True) + NORM_EPS)


def _ada_kernel(c_ref, w_ref, b_ref, o_ref):
    @pl.when(pl.program_id(1) == 0)
    def _():
        o_ref[...] = jnp.broadcast_to(b_ref[...], o_ref.shape)

    c = c_ref[...]
    a = (c * jax.nn.sigmoid(c)).astype(BF16)
    o_ref[...] += jnp.dot(a, w_ref[...].astype(BF16), preferred_element_type=F32)


def _ada_mods(cvec, ada_w, ada_b):
    kb = LANES
    n = N_MOD * D_MODEL
    return pl.pallas_call(
        _ada_kernel,
        grid=(DEPTH, D_MODEL // kb),
        in_specs=[
            pl.BlockSpec((8, kb), lambda l, k: (0, k)),
            pl.BlockSpec((None, kb, n), lambda l, k: (l, k, 0)),
            pl.BlockSpec((None, 1, n), lambda l, k: (l, 0, 0)),
        ],
        out_specs=pl.BlockSpec((None, 8, n), lambda l, k: (l, 0, 0)),
        out_shape=jax.ShapeDtypeStruct((DEPTH, 8, n), F32),
        compiler_params=_params(dimension_semantics=("arbitrary", "arbitrary")),
        name="ada_mods",
    )(cvec, ada_w, ada_b.reshape(DEPTH, 1, n))


def _pre(x, p_ref):
    gain = p_ref[P_PRE_G:P_PRE_G + 1, :] * (1.0 + p_ref[P_SCALE:P_SCALE + 1, :])
    return _rms(x) * gain + p_ref[P_SHIFT:P_SHIFT + 1, :]


def _prenorm_kernel(x_ref, p_ref, h_ref):
    h_ref[...] = _pre(x_ref[...], p_ref).astype(BF16)


def _prenorm(x, pp, s, tm=512):
    rows = x.shape[0]
    return pl.pallas_call(
        _prenorm_kernel,
        grid=(rows // tm,),
        in_specs=[
            pl.BlockSpec((tm, D_MODEL), lambda i: (i, 0)),
            pl.BlockSpec((None, None, 8, D_MODEL), lambda i: (s, _group_of_block(i, tm), 0, 0)),
        ],
        out_specs=pl.BlockSpec((tm, D_MODEL), lambda i: (i, 0)),
        out_shape=jax.ShapeDtypeStruct((rows, D_MODEL), BF16),
        compiler_params=_params(dimension_semantics=("arbitrary",)),
        name="prenorm",
    )(x, pp)


class _Job(NamedTuple):
    src: jax.Array
    in_spec: pl.BlockSpec
    out_spec: pl.BlockSpec
    out_shape: jax.ShapeDtypeStruct
    body: Callable
    n_blocks: int


def _row_block(rows, rows_out, steps):
    g = math.gcd(rows, rows_out)
    for rb in range(BF16_SUBLANES, g + 1, BF16_SUBLANES):
        if g % rb == 0 and rows_out // rb <= steps:
            return rb
    raise ValueError(f"no row block for {rows}->{rows_out} rows in {steps} steps")


def _cast_job(src, lead, rows_out=None):
    rows, cols = src.shape[-2:]
    rows_out = rows_out or rows

    def make(steps, lin):
        rb = _row_block(rows, rows_out, steps)
        nb_in, nb_out = rows // rb, rows_out // rb

        def body(src_ref, dst_ref, blk):
            v = src_ref[...].astype(BF16)
            if nb_out > nb_in:
                v = jnp.where(blk < nb_in, v, jnp.zeros_like(v))
            dst_ref[...] = v

        return _Job(
            src,
            pl.BlockSpec((None,) * len(lead) + (rb, cols),
                         lambda *g: lead + (jnp.minimum(lin(*g), nb_in - 1), 0)),
            pl.BlockSpec((rb, cols), lambda *g: (jnp.minimum(lin(*g), nb_out - 1), 0)),
            jax.ShapeDtypeStruct((rows_out, cols), BF16), body, nb_out)

    return make


def _ffn_split_job(ffn_w_in, lead):
    def make(steps, lin):
        rb = _row_block(D_MODEL, D_MODEL, steps)
        nb = D_MODEL // rb

        def body(src_ref, dst_ref, blk):
            del blk
            for part in range(2):
                dst_ref[part, :, :D_FF] = src_ref[:, part * D_FF:(part + 1) * D_FF].astype(BF16)
                dst_ref[part, :, D_FF:] = jnp.zeros((rb, D_FF_PAD - D_FF), BF16)

        return _Job(
            ffn_w_in,
            pl.BlockSpec((None,) * len(lead) + (rb, 2 * D_FF), lambda *g: lead + (jnp.minimum(lin(*g), nb - 1), 0)),
            pl.BlockSpec((2, rb, D_FF_PAD), lambda *g: (0, jnp.minimum(lin(*g), nb - 1), 0)),
            jax.ShapeDtypeStruct((2, D_MODEL, D_FF_PAD), BF16), body, nb)

    return make


def _hosted_call(main, main_args, main_in_specs, out_spec, out_shape, grid, job_makers, name):
    steps = math.prod(grid)
    strides = [math.prod(grid[d + 1:]) for d in range(len(grid))]
    lin = lambda *g: sum(gi * st for gi, st in zip(g, strides))
    jobs = [mk(steps, lin) for mk in job_makers]
    n_in, n_jobs = len(main_args), len(jobs)

    def kern(*refs):
        if main is not None:
            main(*refs[:n_in], refs[n_in + n_jobs])
        t = lin(*[pl.program_id(d) for d in range(len(grid))])
        n_main_out = 0 if main is None else 1
        for q, jb in enumerate(jobs):
            jb.body(refs[n_in + q], refs[n_in + n_jobs + n_main_out + q], jnp.minimum(t, jb.n_blocks - 1))

    main_out = [] if main is None else [(out_spec, out_shape)]
    res = pl.pallas_call(
        kern,
        grid=grid,
        in_specs=list(main_in_specs) + [jb.in_spec for jb in jobs],
        out_specs=[s for s, _ in main_out] + [jb.out_spec for jb in jobs],
        out_shape=[s for _, s in main_out] + [jb.out_shape for jb in jobs],
        compiler_params=_params(dimension_semantics=("arbitrary",) * len(grid)),
        name=name,
    )(*main_args, *[jb.src for jb in jobs])
    return (None, list(res)) if main is None else (res[0], list(res[1:]))


def _convert(job_maker, steps=32):
    return _hosted_call(None, (), (), None, None, (steps,), [job_maker], "convert")[1][0]


def _swiglu_kernel(h_ref, wg_ref, wu_ref, o_ref):
    h = h_ref[...]
    g = jnp.dot(h, wg_ref[...], preferred_element_type=F32)
    u = jnp.dot(h, wu_ref[...], preferred_element_type=F32)
    o_ref[...] = (g * jax.nn.sigmoid(g) * u).astype(BF16)


def _ffn_in(h, wgu, jobs, tn=FF_TILE):
    rows = h.shape[0]
    tm = rows // IN_ROW_TILES
    return _hosted_call(
        _swiglu_kernel, (h, wgu, wgu),
        [pl.BlockSpec((tm, D_MODEL), lambda j, i: (i, 0)),
         pl.BlockSpec((None, D_MODEL, tn), lambda j, i: (0, 0, j)),
         pl.BlockSpec((None, D_MODEL, tn), lambda j, i: (1, 0, j))],
        pl.BlockSpec((tm, tn), lambda j, i: (i, j)),
        jax.ShapeDtypeStruct((rows, D_FF_PAD), BF16),
        (D_FF_PAD // tn, rows // tm), jobs, "ffn_in")


def _gelu_kernel(h_ref, w_ref, o_ref):
    y = jnp.dot(h_ref[...], w_ref[...], preferred_element_type=F32)
    o_ref[...] = jax.nn.gelu(y).astype(BF16)


def _gmlp_in(h, w, jobs, tn=1024):
    rows = h.shape[0]
    tm = rows // IN_ROW_TILES
    n = w.shape[1]
    return _hosted_call(
        _gelu_kernel, (h, w),
        [pl.BlockSpec((tm, D_MODEL), lambda j, i: (i, 0)),
         pl.BlockSpec((D_MODEL, tn), lambda j, i: (0, j))],
        pl.BlockSpec((tm, tn), lambda j, i: (i, j)),
        jax.ShapeDtypeStruct((rows, n), BF16),
        (n // tn, rows // tm), jobs, "gmlp_in")


def _qkv_rope_kernel(h_ref, w_ref, cos_ref, sin_ref, swap_ref, o_ref, *, n_rope_tiles, tn):
    j = pl.program_id(1)

    @pl.when(j < n_rope_tiles)
    def _():
        h = h_ref[...]
        c = cos_ref[...]
        s = sin_ref[...]
        swap = swap_ref[...]
        y = jnp.dot(h, w_ref[...], preferred_element_type=F32)
        hi = y.astype(BF16)
        lo = (y - hi.astype(F32)).astype(BF16)
        for q0 in range(0, tn, MXU_WIDTH):
            cs = slice(q0, q0 + MXU_WIDTH)
            partner = (jnp.dot(hi[:, cs], swap, preferred_element_type=F32)
                       + jnp.dot(lo[:, cs], swap, preferred_element_type=F32))
            o_ref[:, cs] = (y[:, cs] * c + partner * s).astype(BF16)

    @pl.when(j >= n_rope_tiles)
    def _():
        o_ref[...] = jnp.dot(h_ref[...], w_ref[...], preferred_element_type=F32).astype(BF16)


def _qkv_rope(h, w, cos_t, sin_t, rope_width, jobs, tn=512):
    rows = h.shape[0]
    tm = rows // IN_ROW_TILES
    n = w.shape[1]
    kern = functools.partial(_qkv_rope_kernel, n_rope_tiles=rope_width // tn, tn=tn)
    lane = jnp.arange(MXU_WIDTH)
    swap = (lane[:, None] == (lane[None, :] ^ 1)).astype(BF16)
    return _hosted_call(
        kern, (h, w, cos_t, sin_t, swap),
        [pl.BlockSpec((tm, D_MODEL), lambda i, j: (i, 0)),
         pl.BlockSpec((D_MODEL, tn), lambda i, j: (0, j)),
         pl.BlockSpec((tm, MXU_WIDTH), lambda i, j: (i, 0)),
         pl.BlockSpec((tm, MXU_WIDTH), lambda i, j: (i, 0)),
         pl.BlockSpec((MXU_WIDTH, MXU_WIDTH), lambda i, j: (0, 0))],
        pl.BlockSpec((tm, tn), lambda i, j: (i, j)),
        jax.ShapeDtypeStruct((rows, n), BF16),
        (rows // tm, n // tn), jobs, "qkv_rope")


def _mm_out_kernel(a_ref, w_ref, x_ref, p_ref, xo_ref, *maybe_h_ref, coef):
    post = coef * (p_ref[P_GATE:P_GATE + 1, :] * p_ref[P_POST_G:P_POST_G + 1, :])
    half = a_ref.shape[0] // 2
    for rs in (slice(0, half), slice(half, 2 * half)):
        y = jnp.dot(a_ref[rs, :], w_ref[...], preferred_element_type=F32)
        xn = x_ref[rs, :] + _rms(y) * post
        xo_ref[rs, :] = xn
        if maybe_h_ref:
            maybe_h_ref[0][rs, :] = _pre(xn, p_ref).astype(BF16)


def _mm_out(a, w, x, pp, s, *, rows, coef, want_h):
    k = a.shape[1]
    tm = 512 if k <= D_MODEL else 256
    out_shape = [jax.ShapeDtypeStruct((rows, D_MODEL), F32)]
    out_specs = [pl.BlockSpec((tm, D_MODEL), lambda i: (i, 0))]
    if want_h:
        out_shape.append(jax.ShapeDtypeStruct((rows, D_MODEL), BF16))
        out_specs.append(pl.BlockSpec((tm, D_MODEL), lambda i: (i, 0)))
    res = pl.pallas_call(
        functools.partial(_mm_out_kernel, coef=coef),
        grid=(rows // tm,),
        in_specs=[
            pl.BlockSpec((tm, k), lambda i: (i, 0)),
            pl.BlockSpec((k, D_MODEL), lambda i: (0, 0), pipeline_mode=pl.Buffered(1)),
            pl.BlockSpec((tm, D_MODEL), lambda i: (i, 0)),
            pl.BlockSpec((None, None, 8, D_MODEL), lambda i: (s, _group_of_block(i, tm), 0, 0)),
        ],
        out_specs=out_specs,
        out_shape=out_shape,
        compiler_params=_params(dimension_semantics=("arbitrary",)),
        name="mm_out",
    )(a, w, x, pp)
    return (res[0], res[1]) if want_h else (res[0], None)


def _dot_t(a, b):
    return lax.dot_general(a, b, (((1,), (1,)), ((), ())), preferred_element_type=F32)


def _dot_tn(a, b):
    return lax.dot_general(a, b, (((0,), (0,)), ((), ())), preferred_element_type=F32)


def _win_attn_kernel(sink_ref, q_ref, kv_ref, kvc_ref, o_ref):
    n = pl.program_id(1)
    n_lat = SEQ // A_BLOCK
    kvw = A_KV_HEADS * HEAD_DIM
    scale = HEAD_DIM ** -0.5
    expo = scale * math.log2(math.e)
    gq = A_GROUP * A_BLOCK
    lane = lax.broadcasted_iota(jnp.int32, (1, gq), 1)
    qi = lax.broadcasted_iota(jnp.int32, (A_BLOCK, gq), 1) & (A_BLOCK - 1)
    mi = lax.broadcasted_iota(jnp.int32, (A_BLOCK, gq), 0)

    def heads(kk):
        q4 = jnp.concatenate(
            [q_ref[:, (kk * A_GROUP + g) * HEAD_DIM:(kk * A_GROUP + g + 1) * HEAD_DIM] for g in range(A_GROUP)],
            axis=0)
        sink = jnp.full((1, gq), sink_ref[kk * A_GROUP] / scale, F32)
        for g in range(1, A_GROUP):
            sink = jnp.where(lane >= g * A_BLOCK, sink_ref[kk * A_GROUP + g] / scale, sink)
        return q4, sink, slice(kk * HEAD_DIM, (kk + 1) * HEAD_DIM), slice(kvw + kk * HEAD_DIM, kvw + (kk + 1) * HEAD_DIM)

    def finish(kk, pieces, sink):
        m = sink
        for s, _ in pieces:
            m = jnp.maximum(m, jnp.max(s, axis=0, keepdims=True))
        den = jnp.exp2((sink - m) * expo)
        ot = None
        for s, v in pieces:
            p = jnp.exp2((s - m) * expo)
            den = den + jnp.sum(p, axis=0, keepdims=True)
            pv = _dot_tn(v, p.astype(BF16))
            ot = pv if ot is None else ot + pv
        ot = ot / den
        for g in range(A_GROUP):
            h = kk * A_GROUP + g
            o_ref[:, h * HEAD_DIM:(h + 1) * HEAD_DIM] = ot[:, g * A_BLOCK:(g + 1) * A_BLOCK].T.astype(BF16)

    @pl.when(n < n_lat)
    def _():
        off_l = jnp.where(n > 0, 0, 2 * A_BLOCK)
        off_r = jnp.where(n < n_lat - 1, 0, 2 * A_BLOCK)
        ok_l = mi >= qi + off_l
        ok_r = mi <= qi - off_r
        rows_l, rows_m, rows_r = [pl.ds(pl.multiple_of(jnp.clip(n + d, 0, n_lat - 1) * A_BLOCK, A_BLOCK), A_BLOCK)
                                  for d in (-1, 0, 1)]
        for kk in range(A_KV_HEADS):
            q4, sink, kc, vc = heads(kk)
            s_l = jnp.where(ok_l, _dot_t(kv_ref[rows_l, kc], q4), NEG_INF)
            s_r = jnp.where(ok_r, _dot_t(kv_ref[rows_r, kc], q4), NEG_INF)
            finish(kk, [(_dot_t(kvc_ref[:, kc], q4), kvc_ref[:, vc]), (s_l, kv_ref[rows_l, vc]),
                        (_dot_t(kv_ref[rows_m, kc], q4), kv_ref[rows_m, vc]), (s_r, kv_ref[rows_r, vc])], sink)

    @pl.when(n >= n_lat)
    def _():
        for kk in range(A_KV_HEADS):
            q4, sink, kc, vc = heads(kk)
            finish(kk, [(_dot_t(kvc_ref[:, kc], q4), kvc_ref[:, vc])], sink)


def _win_attn(qkv, sink, with_ctx):
    n_lat = SEQ // A_BLOCK
    n_ctx = CTX_LEN // A_BLOCK
    steps = n_lat + (n_ctx if with_ctx else 0)
    qw = A_HEADS * HEAD_DIM
    kvw2 = 2 * A_KV_HEADS * HEAD_DIM
    ctx_blk0 = LAT_ROWS // A_BLOCK

    def qrow(b, n):
        return jnp.where(n < n_lat, b * n_lat + n, ctx_blk0 + b * n_ctx + (n - n_lat))

    rows = LAT_ROWS + (CTX_ROWS if with_ctx else 0)
    return pl.pallas_call(
        _win_attn_kernel,
        grid=(BATCH, steps),
        in_specs=[
            pl.BlockSpec(memory_space=pltpu.SMEM),
            pl.BlockSpec((A_BLOCK, qw), lambda b, n: (qrow(b, n), 0)),
            pl.BlockSpec((SEQ, kvw2), lambda b, n: (b, qw // kvw2)),
            pl.BlockSpec((CTX_LEN, kvw2), lambda b, n: (LAT_ROWS // CTX_LEN + b, qw // kvw2)),
        ],
        out_specs=pl.BlockSpec((A_BLOCK, D_MODEL), lambda b, n: (qrow(b, n), 0)),
        out_shape=jax.ShapeDtypeStruct((rows, D_MODEL), BF16),
        compiler_params=_params(dimension_semantics=("arbitrary", "arbitrary")),
        name="win_attn",
    )(sink, qkv, qkv, qkv)


def _gmlp_mid_kernel(u_ref, v_ref, g_ref, b_ref, ws_ref, bs_ref, o_ref, *, tm):
    v = v_ref[...].astype(F32)
    mu = jnp.mean(v, axis=-1, keepdims=True)
    vc = v - mu
    vn = vc * lax.rsqrt(jnp.mean(vc * vc, axis=-1, keepdims=True) + NORM_EPS)
    vn = (vn * g_ref[...] + b_ref[...]).astype(BF16)
    for g in range(B_GROUPS):
        cs = slice(g * B_GROUP_W, (g + 1) * B_GROUP_W)
        ws = ws_ref[g].astype(BF16)
        for c in range(tm // B_CHUNK):
            rs = slice(c * B_CHUNK, (c + 1) * B_CHUNK)
            mixed = jnp.dot(ws, vn[rs, cs], preferred_element_type=F32) + bs_ref[:, g:g + 1]
            o_ref[rs, cs] = (u_ref[rs, cs].astype(F32) * mixed).astype(BF16)


def _gmlp_mid(z, vn_g, vn_b, ws, bs_t, tm=256):
    rows = z.shape[0]
    return pl.pallas_call(
        functools.partial(_gmlp_mid_kernel, tm=tm),
        grid=(rows // tm,),
        in_specs=[
            pl.BlockSpec((tm, B_WIDTH), lambda i: (i, 0)),
            pl.BlockSpec((tm, B_WIDTH), lambda i: (i, 1)),
            pl.BlockSpec((1, B_WIDTH), lambda i: (0, 0)),
            pl.BlockSpec((1, B_WIDTH), lambda i: (0, 0)),
            pl.BlockSpec((B_GROUPS, B_CHUNK, B_CHUNK), lambda i: (0, 0, 0)),
            pl.BlockSpec((B_CHUNK, B_GROUPS), lambda i: (0, 0)),
        ],
        out_specs=pl.BlockSpec((tm, B_WIDTH), lambda i: (i, 0)),
        out_shape=jax.ShapeDtypeStruct((rows, B_WIDTH), BF16),
        compiler_params=_params(dimension_semantics=("arbitrary",)),
        name="gmlp_mid",
    )(z, z, vn_g, vn_b, ws, bs_t)


def _diff_attn_kernel(lq1_ref, lk1_ref, lq2_ref, lk2_ref, sg_ref, q_ref, k_ref, v_ref, kc_ref, vc_ref, o_ref, vt_ref,
                      *, lam_init, tq, tk):
    qi = pl.program_id(2)
    n_lat = SEQ // tq
    expo = (HEAD_DIM ** -0.5) * math.log2(math.e)
    lam = (jnp.exp(jnp.sum(lq1_ref[...] * lk1_ref[...], axis=-1, keepdims=True))
           - jnp.exp(jnp.sum(lq2_ref[...] * lk2_ref[...], axis=-1, keepdims=True)) + lam_init)

    @pl.when(qi == 0)
    def _():
        vt_ref[:, :CTX_LEN] = vc_ref[...].T
        for t in range(SEQ // tk):
            vt_ref[:, CTX_LEN + t * tk:CTX_LEN + (t + 1) * tk] = v_ref[t * tk:(t + 1) * tk, :].T

    def run(chunks):
        qs = [q_ref[:, c * HEAD_DIM:(c + 1) * HEAD_DIM] for c in range(2)]

        def scores(chunk):
            kref, lo, size, _ = chunk
            return [_dot_t(kref[lo:lo + size, c * HEAD_DIM:(c + 1) * HEAD_DIM], qs[c]) for c in range(2)]

        m = [None, None]
        l = [None, None]
        acc = [None, None]
        s_next = scores(chunks[0])
        for t, (_, _, size, vlo) in enumerate(chunks):
            s_cur = s_next
            if t + 1 < len(chunks):
                s_next = scores(chunks[t + 1])
            for c in range(2):
                st = s_cur[c]
                m_new = jnp.max(st, axis=0, keepdims=True)
                if m[c] is not None:
                    m_new = jnp.maximum(m[c], m_new)
                p = jnp.exp2((st - m_new) * expo)
                pv = jnp.dot(vt_ref[:, vlo:vlo + size], p.astype(BF16), preferred_element_type=F32)
                if m[c] is None:
                    l[c] = jnp.sum(p, axis=0, keepdims=True)
                    acc[c] = pv
                else:
                    alpha = jnp.exp2((m[c] - m_new) * expo)
                    l[c] = alpha * l[c] + jnp.sum(p, axis=0, keepdims=True)
                    acc[c] = alpha * acc[c] + pv
                m[c] = m_new
        o = acc[0] / l[0] - lam * (acc[1] / l[1])
        o = o * lax.rsqrt(jnp.mean(o * o, axis=0, keepdims=True) + NORM_EPS) * (sg_ref[...] * (1.0 - lam_init))
        o_ref[...] = o.T.astype(BF16)

    @pl.when(qi < n_lat)
    def _():
        run([(kc_ref, 0, CTX_LEN, 0)] + [(k_ref, t * tk, tk, CTX_LEN + t * tk) for t in range(SEQ // tk)])

    @pl.when(qi >= n_lat)
    def _():
        run([(kc_ref, 0, CTX_LEN, 0)])


def _diff_attn(qkv, lq1, lk1, lq2, lk2, subln_g_col, lam_init, tq=256, tk=1024):
    hw = 2 * HEAD_DIM
    n_lat = SEQ // tq
    n_ctx = CTX_LEN // tq
    kcol = D_MODEL // hw
    ctx_blk0 = LAT_ROWS // tq

    def qrow(b, qi):
        return jnp.where(qi < n_lat, b * n_lat + qi, ctx_blk0 + b * n_ctx + (qi - n_lat))

    vec = pl.BlockSpec((1, HEAD_DIM), lambda b, h, qi: (0, 0))
    kern = functools.partial(_diff_attn_kernel, lam_init=lam_init, tq=tq, tk=tk)
    return pl.pallas_call(
        kern,
        grid=(BATCH, C_HEADS, n_lat + n_ctx),
        in_specs=[
            vec, vec, vec, vec,
            pl.BlockSpec((hw, 1), lambda b, h, qi: (0, 0)),
            pl.BlockSpec((tq, hw), lambda b, h, qi: (qrow(b, qi), h)),
            pl.BlockSpec((SEQ, hw), lambda b, h, qi: (b, kcol + h)),
            pl.BlockSpec((SEQ, hw), lambda b, h, qi: (b, 2 * kcol + h)),
            pl.BlockSpec((CTX_LEN, hw), lambda b, h, qi: (LAT_ROWS // CTX_LEN + b, kcol + h)),
            pl.BlockSpec((CTX_LEN, hw), lambda b, h, qi: (LAT_ROWS // CTX_LEN + b, 2 * kcol + h)),
        ],
        out_specs=pl.BlockSpec((tq, hw), lambda b, h, qi: (qrow(b, qi), h)),
        out_shape=jax.ShapeDtypeStruct((ROWS, D_MODEL), BF16),
        scratch_shapes=[pltpu.VMEM((hw, CTX_LEN + SEQ), BF16)],
        compiler_params=_params(dimension_semantics=("arbitrary", "arbitrary", "arbitrary")),
        name="diff_attn",
    )(lq1, lk1, lq2, lk2, subln_g_col, qkv, qkv, qkv, qkv, qkv)


def _rope_tables():
    rows = SEQ // GRID_W
    row = jnp.repeat(jnp.arange(rows, dtype=F32), GRID_W)
    col = jnp.tile(jnp.arange(GRID_W, dtype=F32), rows)
    axis_dim = HEAD_DIM // 2
    inv = ROPE_THETA ** (-jnp.arange(0, axis_dim, 2, dtype=F32) / axis_dim)
    ang = jnp.concatenate([row[:, None] * inv, col[:, None] * inv], axis=-1)
    cos = jnp.repeat(jnp.cos(ang), 2, axis=-1)
    sin = jnp.repeat(jnp.sin(ang), 2, axis=-1) * jnp.tile(jnp.array([-1.0, 1.0], F32), HEAD_DIM // 2)
    heads_per_chunk = MXU_WIDTH // HEAD_DIM
    cos_t = jnp.concatenate([jnp.tile(cos, (BATCH, heads_per_chunk)), jnp.ones((CTX_ROWS, MXU_WIDTH), F32)], axis=0)
    sin_t = jnp.concatenate([jnp.tile(sin, (BATCH, heads_per_chunk)), jnp.zeros((CTX_ROWS, MXU_WIDTH), F32)], axis=0)
    return cos_t, sin_t


def _param_tiles(mods, norm_g):
    m = mods[:, :N_GROUPS].reshape(DEPTH, N_GROUPS, N_MOD, D_MODEL)
    zero = jnp.zeros((N_GROUPS, D_MODEL), F32)
    bcast = lambda v: jnp.broadcast_to(v, (N_GROUPS, D_MODEL))
    tiles = []
    for s in range(-1, 3 * DEPTH):
        rows = [zero] * 8
        if s >= 0:
            i, slot = divmod(s, 3)
            rows[P_POST_G] = bcast(norm_g[i, 2 * slot + 1])
            rows[P_GATE] = m[i, :, 3 * slot + 2]
        if s + 1 < 3 * DEPTH:
            i, slot = divmod(s + 1, 3)
            rows[P_PRE_G] = bcast(norm_g[i, 2 * slot])
            rows[P_SCALE] = m[i, :, 3 * slot + 1]
            rows[P_SHIFT] = m[i, :, 3 * slot]
        tiles.append(jnp.stack(rows, axis=1))
    return jnp.stack(tiles, axis=0)


def kernel(x, c, ctx, c_ctx, ada_w, ada_b, norm_g, ffn_w_in, ffn_w_out, a_w_in, a_w_out, a_sink, b_w_in, b_vnorm_g,
           b_vnorm_b, b_ws, b_bs, b_w_out, c_w_in, c_w_out, c_lq1, c_lk1, c_lq2, c_lk2, c_subln_g):
    xs = jnp.concatenate([x.reshape(LAT_ROWS, D_MODEL), ctx.reshape(CTX_ROWS, D_MODEL)], axis=0)
    cvec = jnp.concatenate([c, c_ctx[None, :], jnp.zeros((8 - BATCH - 1, D_MODEL), F32)], axis=0)
    pp = _param_tiles(_ada_mods(cvec, ada_w, ada_b), norm_g)
    cos_t, sin_t = _rope_tables()

    mixer_w = ((a_w_in, a_w_out), (b_w_in, b_w_out), (c_w_in, c_w_out))
    wgu = _convert(_ffn_split_job(ffn_w_in, (0, 0)))
    h = _prenorm(xs, pp, 0)
    for i in range(DEPTH):
        kind, j = i % N_MIXERS, i // N_MIXERS
        ctx_live = i < DEPTH - 1
        rows_out = ROWS if ctx_live else LAT_ROWS
        s0 = 3 * i + 1
        w_mix_in, w_mix_out = mixer_w[kind]

        a, (wdn, w_i) = _ffn_in(h, wgu, [_cast_job(ffn_w_out, (i, 0), D_FF_PAD), _cast_job(w_mix_in, (j,))])
        xs, h = _mm_out(a, wdn, xs, pp, s0, rows=ROWS, coef=0.5, want_h=True)

        jobs = [_cast_job(w_mix_out, (j,)), _ffn_split_job(ffn_w_in, (i, 1))]
        if kind == 0:
            qkv, (w_o, wgu) = _qkv_rope(h, w_i, cos_t, sin_t, (A_HEADS + A_KV_HEADS) * HEAD_DIM, jobs)
            o = _win_attn(qkv, a_sink[j], ctx_live)
        elif kind == 1:
            z, (w_o, wgu) = _gmlp_in(h, w_i, jobs)
            o = _gmlp_mid(z, b_vnorm_g[j][None, :], b_vnorm_b[j][None, :], b_ws[j], b_bs[j].T)
        else:
            lam_init = 0.8 - 0.6 * math.exp(-0.3 * i)
            qkv, (w_o, wgu) = _qkv_rope(h, w_i, cos_t, sin_t, 2 * D_MODEL, jobs)
            o = _diff_attn(qkv, c_lq1[j][None, :], c_lk1[j][None, :], c_lq2[j][None, :], c_lk2[j][None, :],
                           c_subln_g[j][:, None], lam_init)
        xs, h = _mm_out(o, w_o, xs, pp, s0 + 1, rows=rows_out, coef=1.0, want_h=True)

        jobs = [_cast_job(ffn_w_out, (i, 1), D_FF_PAD)]
        if i + 1 < DEPTH:
            jobs.append(_ffn_split_job(ffn_w_in, (i + 1, 0)))
        a, conv = _ffn_in(h, wgu, jobs)
        xs, h = _mm_out(a, conv[0], xs, pp, s0 + 2, rows=rows_out, coef=0.5, want_h=i + 1 < DEPTH)
        wgu = conv[1] if i + 1 < DEPTH else None
    return xs.reshape(BATCH, SEQ, D_MODEL)
```

```python
import functools
import math
from typing import Callable, NamedTuple

import jax
import jax.numpy as jnp
from jax import lax
from jax.experimental import pallas as pl
from jax.experimental.pallas import tpu as pltpu

D_MODEL = 2048
BATCH = 2
SEQ = 4096
DEPTH = 4
GRID_W = 64
CTX_LEN = 256
N_MIXERS = 3
N_MOD = 9
NORM_EPS = 1e-6
ROPE_THETA = 10000.0
NEG_INF = -1e30
D_FF = 5504
HEAD_DIM = 128
A_HEADS = 16
A_KV_HEADS = 4
A_GROUP = 4
A_BLOCK = 128
B_CHUNK = 128
B_WIDTH = 3 * D_MODEL
B_GROUPS = 8
B_GROUP_W = B_WIDTH // B_GROUPS
C_HEADS = 8

LAT_ROWS = BATCH * SEQ
CTX_ROWS = BATCH * CTX_LEN
ROWS = LAT_ROWS + CTX_ROWS
N_GROUPS = 3
LANES = 128
BF16_SUBLANES = 16
MXU_WIDTH = 256
IN_ROW_TILES = 8
FF_TILE = 512
D_FF_PAD = -(-D_FF // FF_TILE) * FF_TILE
VMEM_LIMIT = 56 * 1024 * 1024

BF16 = jnp.bfloat16
F32 = jnp.float32

P_POST_G, P_GATE, P_PRE_G, P_SCALE, P_SHIFT = 0, 1, 2, 3, 4


def _params(**kw):
    return pltpu.CompilerParams(vmem_limit_bytes=VMEM_LIMIT, **kw)


def _group_of_block(i, tm):
    return jnp.minimum(i // (SEQ // tm), N_GROUPS - 1)


def _rms(x):
    return x * lax.rsqrt(jnp.mean(x * x, axis=-1, keepdims=True) + NORM_EPS)


def _ada_kernel(c_ref, w_ref, b_ref, o_ref):
    @pl.when(pl.program_id(1) == 0)
    def _():
        o_ref[...] = jnp.broadcast_to(b_ref[...], o_ref.shape)

    c = c_ref[...]
    a = (c * jax.nn.sigmoid(c)).astype(BF16)
    o_ref[...] += jnp.dot(a, w_ref[...].astype(BF16), preferred_element_type=F32)


def _ada_mods(cvec, ada_w, ada_b):
    kb = LANES
    n = N_MOD * D_MODEL
    return pl.pallas_call(
        _ada_kernel,
        grid=(DEPTH, D_MODEL // kb),
        in_specs=[
            pl.BlockSpec((8, kb), lambda l, k: (0, k)),
            pl.BlockSpec((None, kb, n), lambda l, k: (l, k, 0)),
            pl.BlockSpec((None, 1, n), lambda l, k: (l, 0, 0)),
        ],
        out_specs=pl.BlockSpec((None, 8, n), lambda l, k: (l, 0, 0)),
        out_shape=jax.ShapeDtypeStruct((DEPTH, 8, n), F32),
        compiler_params=_params(dimension_semantics=("arbitrary", "arbitrary")),
        name="ada_mods",
    )(cvec, ada_w, ada_b.reshape(DEPTH, 1, n))


def _pre(x, p_ref):
    gain = p_ref[P_PRE_G:P_PRE_G + 1, :] * (1.0 + p_ref[P_SCALE:P_SCALE + 1, :])
    return _rms(x) * gain + p_ref[P_SHIFT:P_SHIFT + 1, :]


def _stream_specs(x, tm):
    if not isinstance(x, tuple):
        return (x,), [pl.BlockSpec((tm, D_MODEL), lambda i: (i, 0))]
    n_lat = LAT_ROWS // tm
    return x, [pl.BlockSpec((tm, D_MODEL), lambda i: (jnp.minimum(i, n_lat - 1), 0)),
               pl.BlockSpec((tm, D_MODEL), lambda i: (jnp.maximum(i - n_lat, 0), 0))]


def _stream_tile(x_refs, rs=slice(None)):
    if len(x_refs) == 1:
        return x_refs[0][rs, :]
    n_lat = LAT_ROWS // x_refs[0].shape[0]
    return jnp.where(pl.program_id(0) < n_lat, x_refs[0][rs, :], x_refs[1][rs, :])


def _prenorm_kernel(*refs):
    *x_refs, p_ref, h_ref = refs
    h_ref[...] = _pre(_stream_tile(x_refs), p_ref).astype(BF16)


def _prenorm(x, pp, s, tm=512):
    x_arrays, x_specs = _stream_specs(x, tm)
    return pl.pallas_call(
        _prenorm_kernel,
        grid=(ROWS // tm,),
        in_specs=x_specs + [
            pl.BlockSpec((None, None, 8, D_MODEL), lambda i: (s, _group_of_block(i, tm), 0, 0)),
        ],
        out_specs=pl.BlockSpec((tm, D_MODEL), lambda i: (i, 0)),
        out_shape=jax.ShapeDtypeStruct((ROWS, D_MODEL), BF16),
        compiler_params=_params(dimension_semantics=("arbitrary",)),
        name="prenorm",
    )(*x_arrays, pp)


class _Job(NamedTuple):
    src: jax.Array
    in_spec: pl.BlockSpec
    out_spec: pl.BlockSpec
    out_shape: jax.ShapeDtypeStruct
    body: Callable
    n_blocks: int


def _row_block(rows, rows_out, steps):
    g = math.gcd(rows, rows_out)
    for rb in range(BF16_SUBLANES, g + 1, BF16_SUBLANES):
        if g % rb == 0 and rows_out // rb <= steps:
            return rb
    raise ValueError(f"no row block for {rows}->{rows_out} rows in {steps} steps")


def _cast_job(src, lead, rows_out=None):
    rows, cols = src.shape[-2:]
    rows_out = rows_out or rows

    def make(steps, lin):
        rb = _row_block(rows, rows_out, steps)
        nb_in, nb_out = rows // rb, rows_out // rb

        def body(src_ref, dst_ref, blk):
            v = src_ref[...].astype(BF16)
            if nb_out > nb_in:
                v = jnp.where(blk < nb_in, v, jnp.zeros_like(v))
            dst_ref[...] = v

        return _Job(
            src,
            pl.BlockSpec((None,) * len(lead) + (rb, cols),
                         lambda *g: lead + (jnp.minimum(lin(*g), nb_in - 1), 0)),
            pl.BlockSpec((rb, cols), lambda *g: (jnp.minimum(lin(*g), nb_out - 1), 0)),
            jax.ShapeDtypeStruct((rows_out, cols), BF16), body, nb_out)

    return make


def _ffn_split_job(ffn_w_in, lead):
    def make(steps, lin):
        rb = _row_block(D_MODEL, D_MODEL, steps)
        nb = D_MODEL // rb

        def body(src_ref, dst_ref, blk):
            del blk
            for part in range(2):
                dst_ref[part, :, :D_FF] = src_ref[:, part * D_FF:(part + 1) * D_FF].astype(BF16)
                dst_ref[part, :, D_FF:] = jnp.zeros((rb, D_FF_PAD - D_FF), BF16)

        return _Job(
            ffn_w_in,
            pl.BlockSpec((None,) * len(lead) + (rb, 2 * D_FF), lambda *g: lead + (jnp.minimum(lin(*g), nb - 1), 0)),
            pl.BlockSpec((2, rb, D_FF_PAD), lambda *g: (0, jnp.minimum(lin(*g), nb - 1), 0)),
            jax.ShapeDtypeStruct((2, D_MODEL, D_FF_PAD), BF16), body, nb)

    return make


def _hosted_call(main, main_args, main_in_specs, out_spec, out_shape, grid, job_makers, name):
    steps = math.prod(grid)
    strides = [math.prod(grid[d + 1:]) for d in range(len(grid))]
    lin = lambda *g: sum(gi * st for gi, st in zip(g, strides))
    jobs = [mk(steps, lin) for mk in job_makers]
    n_in, n_jobs = len(main_args), len(jobs)

    def kern(*refs):
        if main is not None:
            main(*refs[:n_in], refs[n_in + n_jobs])
        t = lin(*[pl.program_id(d) for d in range(len(grid))])
        n_main_out = 0 if main is None else 1
        for q, jb in enumerate(jobs):
            jb.body(refs[n_in + q], refs[n_in + n_jobs + n_main_out + q], jnp.minimum(t, jb.n_blocks - 1))

    main_out = [] if main is None else [(out_spec, out_shape)]
    res = pl.pallas_call(
        kern,
        grid=grid,
        in_specs=list(main_in_specs) + [jb.in_spec for jb in jobs],
        out_specs=[s for s, _ in main_out] + [jb.out_spec for jb in jobs],
        out_shape=[s for _, s in main_out] + [jb.out_shape for jb in jobs],
        compiler_params=_params(dimension_semantics=("arbitrary",) * len(grid)),
        name=name,
    )(*main_args, *[jb.src for jb in jobs])
    return (None, list(res)) if main is None else (res[0], list(res[1:]))


def _convert(job_maker, steps=32):
    return _hosted_call(None, (), (), None, None, (steps,), [job_maker], "convert")[1][0]


def _swiglu_kernel(h_ref, wg_ref, wu_ref, o_ref):
    h = h_ref[...]
    g = jnp.dot(h, wg_ref[...], preferred_element_type=F32)
    u = jnp.dot(h, wu_ref[...], preferred_element_type=F32)
    o_ref[...] = (g * jax.nn.sigmoid(g) * u).astype(BF16)


def _ffn_in(h, wgu, jobs, tn=FF_TILE):
    rows = h.shape[0]
    tm = rows // IN_ROW_TILES
    return _hosted_call(
        _swiglu_kernel, (h, wgu, wgu),
        [pl.BlockSpec((tm, D_MODEL), lambda j, i: (i, 0)),
         pl.BlockSpec((None, D_MODEL, tn), lambda j, i: (0, 0, j)),
         pl.BlockSpec((None, D_MODEL, tn), lambda j, i: (1, 0, j))],
        pl.BlockSpec((tm, tn), lambda j, i: (i, j)),
        jax.ShapeDtypeStruct((rows, D_FF_PAD), BF16),
        (D_FF_PAD // tn, rows // tm), jobs, "ffn_in")


def _gelu_kernel(h_ref, w_ref, o_ref):
    y = jnp.dot(h_ref[...], w_ref[...], preferred_element_type=F32)
    o_ref[...] = jax.nn.gelu(y).astype(BF16)


def _gmlp_in(h, w, jobs, tn=1024):
    rows = h.shape[0]
    tm = rows // IN_ROW_TILES
    n = w.shape[1]
    return _hosted_call(
        _gelu_kernel, (h, w),
        [pl.BlockSpec((tm, D_MODEL), lambda j, i: (i, 0)),
         pl.BlockSpec((D_MODEL, tn), lambda j, i: (0, j))],
        pl.BlockSpec((tm, tn), lambda j, i: (i, j)),
        jax.ShapeDtypeStruct((rows, n), BF16),
        (n // tn, rows // tm), jobs, "gmlp_in")


def _qkv_rope_kernel(h_ref, w_ref, cos_ref, sin_ref, swap_ref, o_ref, *, n_rope_tiles, tn):
    j = pl.program_id(1)

    @pl.when(j < n_rope_tiles)
    def _():
        h = h_ref[...]
        c = cos_ref[...]
        s = sin_ref[...]
        swap = swap_ref[...]
        y = jnp.dot(h, w_ref[...], preferred_element_type=F32)
        hi = y.astype(BF16)
        lo = (y - hi.astype(F32)).astype(BF16)
        for q0 in range(0, tn, MXU_WIDTH):
            cs = slice(q0, q0 + MXU_WIDTH)
            partner = (jnp.dot(hi[:, cs], swap, preferred_element_type=F32)
                       + jnp.dot(lo[:, cs], swap, preferred_element_type=F32))
            o_ref[:, cs] = (y[:, cs] * c + partner * s).astype(BF16)

    @pl.when(j >= n_rope_tiles)
    def _():
        o_ref[...] = jnp.dot(h_ref[...], w_ref[...], preferred_element_type=F32).astype(BF16)


def _qkv_rope(h, w, cos_t, sin_t, rope_width, jobs, tn=512):
    rows = h.shape[0]
    tm = rows // IN_ROW_TILES
    n = w.shape[1]
    kern = functools.partial(_qkv_rope_kernel, n_rope_tiles=rope_width // tn, tn=tn)
    lane = jnp.arange(MXU_WIDTH)
    swap = (lane[:, None] == (lane[None, :] ^ 1)).astype(BF16)
    return _hosted_call(
        kern, (h, w, cos_t, sin_t, swap),
        [pl.BlockSpec((tm, D_MODEL), lambda i, j: (i, 0)),
         pl.BlockSpec((D_MODEL, tn), lambda i, j: (0, j)),
         pl.BlockSpec((tm, MXU_WIDTH), lambda i, j: (i, 0)),
         pl.BlockSpec((tm, MXU_WIDTH), lambda i, j: (i, 0)),
         pl.BlockSpec((MXU_WIDTH, MXU_WIDTH), lambda i, j: (0, 0))],
        pl.BlockSpec((tm, tn), lambda i, j: (i, j)),
        jax.ShapeDtypeStruct((rows, n), BF16),
        (rows // tm, n // tn), jobs, "qkv_rope")


def _mm_out_kernel(a_ref, w_ref, p_ref, *refs, coef, n_x, want_h):
    x_refs, xo_ref, maybe_h_ref = refs[:n_x], refs[n_x], refs[n_x + 1:]
    post = coef * (p_ref[P_GATE:P_GATE + 1, :] * p_ref[P_POST_G:P_POST_G + 1, :])
    half = a_ref.shape[0] // 2
    for rs in (slice(0, half), slice(half, 2 * half)):
        y = jnp.dot(a_ref[rs, :], w_ref[...], preferred_element_type=F32)
        xn = _stream_tile(x_refs, rs) + _rms(y) * post
        xo_ref[rs, :] = xn
        if want_h:
            maybe_h_ref[0][rs, :] = _pre(xn, p_ref).astype(BF16)


def _mm_out(a, w, x, pp, s, *, rows, coef, want_h):
    k = a.shape[1]
    tm = 512 if k <= D_MODEL else 256
    x_arrays, x_specs = _stream_specs(x, tm)
    out_shape = [jax.ShapeDtypeStruct((rows, D_MODEL), F32)]
    out_specs = [pl.BlockSpec((tm, D_MODEL), lambda i: (i, 0))]
    if want_h:
        out_shape.append(jax.ShapeDtypeStruct((rows, D_MODEL), BF16))
        out_specs.append(pl.BlockSpec((tm, D_MODEL), lambda i: (i, 0)))
    res = pl.pallas_call(
        functools.partial(_mm_out_kernel, coef=coef, n_x=len(x_arrays), want_h=want_h),
        grid=(rows // tm,),
        in_specs=[
            pl.BlockSpec((tm, k), lambda i: (i, 0)),
            pl.BlockSpec((k, D_MODEL), lambda i: (0, 0), pipeline_mode=pl.Buffered(1)),
            pl.BlockSpec((None, None, 8, D_MODEL), lambda i: (s, _group_of_block(i, tm), 0, 0)),
        ] + x_specs,
        out_specs=out_specs,
        out_shape=out_shape,
        compiler_params=_params(dimension_semantics=("arbitrary",)),
        name="mm_out",
    )(a, w, pp, *x_arrays)
    return (res[0], res[1]) if want_h else (res[0], None)


def _dot_t(a, b):
    return lax.dot_general(a, b, (((1,), (1,)), ((), ())), preferred_element_type=F32)


def _dot_tn(a, b):
    return lax.dot_general(a, b, (((0,), (0,)), ((), ())), preferred_element_type=F32)


def _win_attn_kernel(sink_ref, q_ref, kv_ref, kvc_ref, o_ref):
    n = pl.program_id(1)
    n_lat = SEQ // A_BLOCK
    kvw = A_KV_HEADS * HEAD_DIM
    scale = HEAD_DIM ** -0.5
    expo = scale * math.log2(math.e)
    gq = A_GROUP * A_BLOCK
    lane = lax.broadcasted_iota(jnp.int32, (1, gq), 1)
    qi = lax.broadcasted_iota(jnp.int32, (A_BLOCK, gq), 1) & (A_BLOCK - 1)
    mi = lax.broadcasted_iota(jnp.int32, (A_BLOCK, gq), 0)

    def heads(kk):
        q4 = jnp.concatenate(
            [q_ref[:, (kk * A_GROUP + g) * HEAD_DIM:(kk * A_GROUP + g + 1) * HEAD_DIM] for g in range(A_GROUP)],
            axis=0)
        sink = jnp.full((1, gq), sink_ref[kk * A_GROUP] / scale, F32)
        for g in range(1, A_GROUP):
            sink = jnp.where(lane >= g * A_BLOCK, sink_ref[kk * A_GROUP + g] / scale, sink)
        return q4, sink, slice(kk * HEAD_DIM, (kk + 1) * HEAD_DIM), slice(kvw + kk * HEAD_DIM, kvw + (kk + 1) * HEAD_DIM)

    def finish(kk, pieces, sink):
        m = sink
        for s, _ in pieces:
            m = jnp.maximum(m, jnp.max(s, axis=0, keepdims=True))
        den = jnp.exp2((sink - m) * expo)
        ot = None
        for s, v in pieces:
            p = jnp.exp2((s - m) * expo)
            den = den + jnp.sum(p, axis=0, keepdims=True)
            pv = _dot_tn(v, p.astype(BF16))
            ot = pv if ot is None else ot + pv
        ot = ot / den
        for g in range(A_GROUP):
            h = kk * A_GROUP + g
            o_ref[:, h * HEAD_DIM:(h + 1) * HEAD_DIM] = ot[:, g * A_BLOCK:(g + 1) * A_BLOCK].T.astype(BF16)

    @pl.when(n < n_lat)
    def _():
        off_l = jnp.where(n > 0, 0, 2 * A_BLOCK)
        off_r = jnp.where(n < n_lat - 1, 0, 2 * A_BLOCK)
        ok_l = mi >= qi + off_l
        ok_r = mi <= qi - off_r
        rows_l, rows_m, rows_r = [pl.ds(pl.multiple_of(jnp.clip(n + d, 0, n_lat - 1) * A_BLOCK, A_BLOCK), A_BLOCK)
                                  for d in (-1, 0, 1)]
        for kk in range(A_KV_HEADS):
            q4, sink, kc, vc = heads(kk)
            s_l = jnp.where(ok_l, _dot_t(kv_ref[rows_l, kc], q4), NEG_INF)
            s_r = jnp.where(ok_r, _dot_t(kv_ref[rows_r, kc], q4), NEG_INF)
            finish(kk, [(_dot_t(kvc_ref[:, kc], q4), kvc_ref[:, vc]), (s_l, kv_ref[rows_l, vc]),
                        (_dot_t(kv_ref[rows_m, kc], q4), kv_ref[rows_m, vc]), (s_r, kv_ref[rows_r, vc])], sink)

    @pl.when(n >= n_lat)
    def _():
        for kk in range(A_KV_HEADS):
            q4, sink, kc, vc = heads(kk)
            finish(kk, [(_dot_t(kvc_ref[:, kc], q4), kvc_ref[:, vc])], sink)


def _win_attn(qkv, sink, with_ctx):
    n_lat = SEQ // A_BLOCK
    n_ctx = CTX_LEN // A_BLOCK
    steps = n_lat + (n_ctx if with_ctx else 0)
    qw = A_HEADS * HEAD_DIM
    kvw2 = 2 * A_KV_HEADS * HEAD_DIM
    ctx_blk0 = LAT_ROWS // A_BLOCK

    def qrow(b, n):
        return jnp.where(n < n_lat, b * n_lat + n, ctx_blk0 + b * n_ctx + (n - n_lat))

    rows = LAT_ROWS + (CTX_ROWS if with_ctx else 0)
    return pl.pallas_call(
        _win_attn_kernel,
        grid=(BATCH, steps),
        in_specs=[
            pl.BlockSpec(memory_space=pltpu.SMEM),
            pl.BlockSpec((A_BLOCK, qw), lambda b, n: (qrow(b, n), 0)),
            pl.BlockSpec((SEQ, kvw2), lambda b, n: (b, qw // kvw2)),
            pl.BlockSpec((CTX_LEN, kvw2), lambda b, n: (LAT_ROWS // CTX_LEN + b, qw // kvw2)),
        ],
        out_specs=pl.BlockSpec((A_BLOCK, D_MODEL), lambda b, n: (qrow(b, n), 0)),
        out_shape=jax.ShapeDtypeStruct((rows, D_MODEL), BF16),
        compiler_params=_params(dimension_semantics=("arbitrary", "arbitrary")),
        name="win_attn",
    )(sink, qkv, qkv, qkv)


def _gmlp_mid_kernel(u_ref, v_ref, g_ref, b_ref, ws_ref, bs_ref, o_ref, *, tm):
    v = v_ref[...].astype(F32)
    mu = jnp.mean(v, axis=-1, keepdims=True)
    vc = v - mu
    vn = vc * lax.rsqrt(jnp.mean(vc * vc, axis=-1, keepdims=True) + NORM_EPS)
    vn = (vn * g_ref[...] + b_ref[...]).astype(BF16)
    for g in range(B_GROUPS):
        cs = slice(g * B_GROUP_W, (g + 1) * B_GROUP_W)
        ws = ws_ref[g].astype(BF16)
        for c in range(tm // B_CHUNK):
            rs = slice(c * B_CHUNK, (c + 1) * B_CHUNK)
            mixed = jnp.dot(ws, vn[rs, cs], preferred_element_type=F32) + bs_ref[:, g:g + 1]
            o_ref[rs, cs] = (u_ref[rs, cs].astype(F32) * mixed).astype(BF16)


def _gmlp_mid(z, vn_g, vn_b, ws, bs_t, tm=256):
    rows = z.shape[0]
    return pl.pallas_call(
        functools.partial(_gmlp_mid_kernel, tm=tm),
        grid=(rows // tm,),
        in_specs=[
            pl.BlockSpec((tm, B_WIDTH), lambda i: (i, 0)),
            pl.BlockSpec((tm, B_WIDTH), lambda i: (i, 1)),
            pl.BlockSpec((1, B_WIDTH), lambda i: (0, 0)),
            pl.BlockSpec((1, B_WIDTH), lambda i: (0, 0)),
            pl.BlockSpec((B_GROUPS, B_CHUNK, B_CHUNK), lambda i: (0, 0, 0)),
            pl.BlockSpec((B_CHUNK, B_GROUPS), lambda i: (0, 0)),
        ],
        out_specs=pl.BlockSpec((tm, B_WIDTH), lambda i: (i, 0)),
        out_shape=jax.ShapeDtypeStruct((rows, B_WIDTH), BF16),
        compiler_params=_params(dimension_semantics=("arbitrary",)),
        name="gmlp_mid",
    )(z, z, vn_g, vn_b, ws, bs_t)


def _diff_attn_kernel(lq1_ref, lk1_ref, lq2_ref, lk2_ref, sg_ref, q_ref, k_ref, v_ref, kc_ref, vc_ref, o_ref, vt_ref,
                      *, lam_init, tq, tk):
    qi = pl.program_id(2)
    n_lat = SEQ // tq
    expo = (HEAD_DIM ** -0.5) * math.log2(math.e)
    lam = (jnp.exp(jnp.sum(lq1_ref[...] * lk1_ref[...], axis=-1, keepdims=True))
           - jnp.exp(jnp.sum(lq2_ref[...] * lk2_ref[...], axis=-1, keepdims=True)) + lam_init)

    @pl.when(qi == 0)
    def _():
        vt_ref[:, :CTX_LEN] = vc_ref[...].T
        for t in range(SEQ // tk):
            vt_ref[:, CTX_LEN + t * tk:CTX_LEN + (t + 1) * tk] = v_ref[t * tk:(t + 1) * tk, :].T

    def run(chunks):
        qs = [q_ref[:, c * HEAD_DIM:(c + 1) * HEAD_DIM] for c in range(2)]

        def scores(chunk):
            kref, lo, size, _ = chunk
            return [_dot_t(kref[lo:lo + size, c * HEAD_DIM:(c + 1) * HEAD_DIM], qs[c]) for c in range(2)]

        m = [None, None]
        l = [None, None]
        acc = [None, None]
        s_next = scores(chunks[0])
        for t, (_, _, size, vlo) in enumerate(chunks):
            s_cur = s_next
            if t + 1 < len(chunks):
                s_next = scores(chunks[t + 1])
            for c in range(2):
                st = s_cur[c]
                m_new = jnp.max(st, axis=0, keepdims=True)
                if m[c] is not None:
                    m_new = jnp.maximum(m[c], m_new)
                p = jnp.exp2((st - m_new) * expo)
                pv = jnp.dot(vt_ref[:, vlo:vlo + size], p.astype(BF16), preferred_element_type=F32)
                if m[c] is None:
                    l[c] = jnp.sum(p, axis=0, keepdims=True)
                    acc[c] = pv
                else:
                    alpha = jnp.exp2((m[c] - m_new) * expo)
                    l[c] = alpha * l[c] + jnp.sum(p, axis=0, keepdims=True)
                    acc[c] = alpha * acc[c] + pv
                m[c] = m_new
        o = acc[0] / l[0] - lam * (acc[1] / l[1])
        o = o * lax.rsqrt(jnp.mean(o * o, axis=0, keepdims=True) + NORM_EPS) * (sg_ref[...] * (1.0 - lam_init))
        o_ref[...] = o.T.astype(BF16)

    @pl.when(qi < n_lat)
    def _():
        run([(kc_ref, 0, CTX_LEN, 0)] + [(k_ref, t * tk, tk, CTX_LEN + t * tk) for t in range(SEQ // tk)])

    @pl.when(qi >= n_lat)
    def _():
        run([(kc_ref, 0, CTX_LEN, 0)])


def _diff_attn(qkv, lq1, lk1, lq2, lk2, subln_g_col, lam_init, tq=256, tk=1024):
    hw = 2 * HEAD_DIM
    n_lat = SEQ // tq
    n_ctx = CTX_LEN // tq
    kcol = D_MODEL // hw
    ctx_blk0 = LAT_ROWS // tq

    def qrow(b, qi):
        return jnp.where(qi < n_lat, b * n_lat + qi, ctx_blk0 + b * n_ctx + (qi - n_lat))

    vec = pl.BlockSpec((1, HEAD_DIM), lambda b, h, qi: (0, 0))
    kern = functools.partial(_diff_attn_kernel, lam_init=lam_init, tq=tq, tk=tk)
    return pl.pallas_call(
        kern,
        grid=(BATCH, C_HEADS, n_lat + n_ctx),
        in_specs=[
            vec, vec, vec, vec,
            pl.BlockSpec((hw, 1), lambda b, h, qi: (0, 0)),
            pl.BlockSpec((tq, hw), lambda b, h, qi: (qrow(b, qi), h)),
            pl.BlockSpec((SEQ, hw), lambda b, h, qi: (b, kcol + h)),
            pl.BlockSpec((SEQ, hw), lambda b, h, qi: (b, 2 * kcol + h)),
            pl.BlockSpec((CTX_LEN, hw), lambda b, h, qi: (LAT_ROWS // CTX_LEN + b, kcol + h)),
            pl.BlockSpec((CTX_LEN, hw), lambda b, h, qi: (LAT_ROWS // CTX_LEN + b, 2 * kcol + h)),
        ],
        out_specs=pl.BlockSpec((tq, hw), lambda b, h, qi: (qrow(b, qi), h)),
        out_shape=jax.ShapeDtypeStruct((ROWS, D_MODEL), BF16),
        scratch_shapes=[pltpu.VMEM((hw, CTX_LEN + SEQ), BF16)],
        compiler_params=_params(dimension_semantics=("arbitrary", "arbitrary", "arbitrary")),
        name="diff_attn",
    )(lq1, lk1, lq2, lk2, subln_g_col, qkv, qkv, qkv, qkv, qkv)


def _rope_tables():
    rows = SEQ // GRID_W
    row = jnp.repeat(jnp.arange(rows, dtype=F32), GRID_W)
    col = jnp.tile(jnp.arange(GRID_W, dtype=F32), rows)
    axis_dim = HEAD_DIM // 2
    inv = ROPE_THETA ** (-jnp.arange(0, axis_dim, 2, dtype=F32) / axis_dim)
    ang = jnp.concatenate([row[:, None] * inv, col[:, None] * inv], axis=-1)
    cos = jnp.repeat(jnp.cos(ang), 2, axis=-1)
    sin = jnp.repeat(jnp.sin(ang), 2, axis=-1) * jnp.tile(jnp.array([-1.0, 1.0], F32), HEAD_DIM // 2)
    heads_per_chunk = MXU_WIDTH // HEAD_DIM
    cos_t = jnp.concatenate([jnp.tile(cos, (BATCH, heads_per_chunk)), jnp.ones((CTX_ROWS, MXU_WIDTH), F32)], axis=0)
    sin_t = jnp.concatenate([jnp.tile(sin, (BATCH, heads_per_chunk)), jnp.zeros((CTX_ROWS, MXU_WIDTH), F32)], axis=0)
    return cos_t, sin_t


def _param_tiles(mods, norm_g):
    m = mods[:, :N_GROUPS].reshape(DEPTH, N_GROUPS, N_MOD, D_MODEL)
    zero = jnp.zeros((N_GROUPS, D_MODEL), F32)
    bcast = lambda v: jnp.broadcast_to(v, (N_GROUPS, D_MODEL))
    tiles = []
    for s in range(-1, 3 * DEPTH):
        rows = [zero] * 8
        if s >= 0:
            i, slot = divmod(s, 3)
            rows[P_POST_G] = bcast(norm_g[i, 2 * slot + 1])
            rows[P_GATE] = m[i, :, 3 * slot + 2]
        if s + 1 < 3 * DEPTH:
            i, slot = divmod(s + 1, 3)
            rows[P_PRE_G] = bcast(norm_g[i, 2 * slot])
            rows[P_SCALE] = m[i, :, 3 * slot + 1]
            rows[P_SHIFT] = m[i, :, 3 * slot]
        tiles.append(jnp.stack(rows, axis=1))
    return jnp.stack(tiles, axis=0)


def kernel(x, c, ctx, c_ctx, ada_w, ada_b, norm_g, ffn_w_in, ffn_w_out, a_w_in, a_w_out, a_sink, b_w_in, b_vnorm_g,
           b_vnorm_b, b_ws, b_bs, b_w_out, c_w_in, c_w_out, c_lq1, c_lk1, c_lq2, c_lk2, c_subln_g):
    xs = (x.reshape(LAT_ROWS, D_MODEL), ctx.reshape(CTX_ROWS, D_MODEL))
    cvec = jnp.concatenate([c, c_ctx[None, :], jnp.zeros((8 - BATCH - 1, D_MODEL), F32)], axis=0)
    pp = _param_tiles(_ada_mods(cvec, ada_w, ada_b), norm_g)
    cos_t, sin_t = _rope_tables()

    mixer_w = ((a_w_in, a_w_out), (b_w_in, b_w_out), (c_w_in, c_w_out))
    wgu = _convert(_ffn_split_job(ffn_w_in, (0, 0)))
    h = _prenorm(xs, pp, 0)
    for i in range(DEPTH):
        kind, j = i % N_MIXERS, i // N_MIXERS
        ctx_live = i < DEPTH - 1
        rows_out = ROWS if ctx_live else LAT_ROWS
        s0 = 3 * i + 1
        w_mix_in, w_mix_out = mixer_w[kind]

        ffn2_split = _ffn_split_job(ffn_w_in, (i, 1))
        jobs = [_cast_job(ffn_w_out, (i, 0), D_FF_PAD), _cast_job(w_mix_in, (j,))] + ([ffn2_split] if kind != 1 else [])
        a, conv = _ffn_in(h, wgu, jobs)
        xs, h = _mm_out(a, conv[0], xs, pp, s0, rows=ROWS, coef=0.5, want_h=True)
        w_i = conv[1]

        jobs = [_cast_job(w_mix_out, (j,))] + ([ffn2_split] if kind == 1 else [])
        if kind == 0:
            qkv, mix_conv = _qkv_rope(h, w_i, cos_t, sin_t, (A_HEADS + A_KV_HEADS) * HEAD_DIM, jobs)
            o = _win_attn(qkv, a_sink[j], ctx_live)
        elif kind == 1:
            z, mix_conv = _gmlp_in(h, w_i, jobs)
            o = _gmlp_mid(z, b_vnorm_g[j][None, :], b_vnorm_b[j][None, :], b_ws[j], b_bs[j].T)
        else:
            lam_init = 0.8 - 0.6 * math.exp(-0.3 * i)
            qkv, mix_conv = _qkv_rope(h, w_i, cos_t, sin_t, 2 * D_MODEL, jobs)
            o = _diff_attn(qkv, c_lq1[j][None, :], c_lk1[j][None, :], c_lq2[j][None, :], c_lk2[j][None, :],
                           c_subln_g[j][:, None], lam_init)
        w_o = mix_conv[0]
        wgu = mix_conv[1] if kind == 1 else conv[2]
        xs, h = _mm_out(o, w_o, xs, pp, s0 + 1, rows=rows_out, coef=1.0, want_h=True)

        jobs = [_cast_job(ffn_w_out, (i, 1), D_FF_PAD)]
        if i + 1 < DEPTH:
            jobs.append(_ffn_split_job(ffn_w_in, (i + 1, 0)))
        a, conv = _ffn_in(h, wgu, jobs)
        xs, h = _mm_out(a, conv[0], xs, pp, s0 + 2, rows=rows_out, coef=0.5, want_h=i + 1 < DEPTH)
        wgu = conv[1] if i + 1 < DEPTH else None
    return xs.reshape(BATCH, SEQ, D_MODEL)
```

```python
import functools
import math
from typing import Callable, NamedTuple

import jax
import jax.numpy as jnp
from jax import lax
from jax.experimental import pallas as pl
from jax.experimental.pallas import tpu as pltpu

D_MODEL = 2048
BATCH = 2
SEQ = 4096
DEPTH = 4
GRID_W = 64
CTX_LEN = 256
N_MIXERS = 3
N_MOD = 9
NORM_EPS = 1e-6
ROPE_THETA = 10000.0
NEG_INF = -1e30
D_FF = 5504
HEAD_DIM = 128
A_HEADS = 16
A_KV_HEADS = 4
A_GROUP = 4
A_BLOCK = 128
B_CHUNK = 128
B_WIDTH = 3 * D_MODEL
B_GROUPS = 8
B_GROUP_W = B_WIDTH // B_GROUPS
C_HEADS = 8

LAT_ROWS = BATCH * SEQ
CTX_ROWS = BATCH * CTX_LEN
ROWS = LAT_ROWS + CTX_ROWS
N_GROUPS = 3
LANES = 128
BF16_SUBLANES = 16
MXU_WIDTH = 256
IN_ROW_TILES = 8
FF_TILE = 512
D_FF_PAD = -(-D_FF // FF_TILE) * FF_TILE
VMEM_LIMIT = 56 * 1024 * 1024

BF16 = jnp.bfloat16
F32 = jnp.float32

P_POST_G, P_GATE, P_PRE_G, P_SCALE, P_SHIFT = 0, 1, 2, 3, 4


def _params(**kw):
    return pltpu.CompilerParams(vmem_limit_bytes=VMEM_LIMIT, **kw)


def _group_of_block(i, tm):
    return jnp.minimum(i // (SEQ // tm), N_GROUPS - 1)


def _rms(x):
    return x * lax.rsqrt(jnp.mean(x * x, axis=-1, keepdims=True) + NORM_EPS)


def _ada_kernel(c_ref, w_ref, b_ref, o_ref):
    @pl.when(pl.program_id(1) == 0)
    def _():
        o_ref[...] = jnp.broadcast_to(b_ref[...], o_ref.shape)

    c = c_ref[...]
    a = (c * jax.nn.sigmoid(c)).astype(BF16)
    o_ref[...] += jnp.dot(a, w_ref[...].astype(BF16), preferred_element_type=F32)


def _ada_mods(cvec, ada_w, ada_b):
    kb = LANES
    n = N_MOD * D_MODEL
    return pl.pallas_call(
        _ada_kernel,
        grid=(DEPTH, D_MODEL // kb),
        in_specs=[
            pl.BlockSpec((8, kb), lambda l, k: (0, k)),
            pl.BlockSpec((None, kb, n), lambda l, k: (l, k, 0)),
            pl.BlockSpec((None, 1, n), lambda l, k: (l, 0, 0)),
        ],
        out_specs=pl.BlockSpec((None, 8, n), lambda l, k: (l, 0, 0)),
        out_shape=jax.ShapeDtypeStruct((DEPTH, 8, n), F32),
        compiler_params=_params(dimension_semantics=("arbitrary", "arbitrary")),
        name="ada_mods",
    )(cvec, ada_w, ada_b.reshape(DEPTH, 1, n))


def _pre(x, p_ref):
    gain = p_ref[P_PRE_G:P_PRE_G + 1, :] * (1.0 + p_ref[P_SCALE:P_SCALE + 1, :])
    return _rms(x) * gain + p_ref[P_SHIFT:P_SHIFT + 1, :]


def _stream_specs(x, tm):
    if not isinstance(x, tuple):
        return (x,), [pl.BlockSpec((tm, D_MODEL), lambda i: (i, 0))]
    n_lat = LAT_ROWS // tm
    return x, [pl.BlockSpec((tm, D_MODEL), lambda i: (jnp.minimum(i, n_lat - 1), 0)),
               pl.BlockSpec((tm, D_MODEL), lambda i: (jnp.maximum(i - n_lat, 0), 0))]


def _stream_tile(x_refs, rs=slice(None)):
    if len(x_refs) == 1:
        return x_refs[0][rs, :]
    n_lat = LAT_ROWS // x_refs[0].shape[0]
    return jnp.where(pl.program_id(0) < n_lat, x_refs[0][rs, :], x_refs[1][rs, :])


def _prenorm_kernel(*refs):
    *x_refs, p_ref, h_ref = refs
    h_ref[...] = _pre(_stream_tile(x_refs), p_ref).astype(BF16)


def _prenorm(x, pp, s, tm=512):
    x_arrays, x_specs = _stream_specs(x, tm)
    return pl.pallas_call(
        _prenorm_kernel,
        grid=(ROWS // tm,),
        in_specs=x_specs + [
            pl.BlockSpec((None, None, 8, D_MODEL), lambda i: (s, _group_of_block(i, tm), 0, 0)),
        ],
        out_specs=pl.BlockSpec((tm, D_MODEL), lambda i: (i, 0)),
        out_shape=jax.ShapeDtypeStruct((ROWS, D_MODEL), BF16),
        compiler_params=_params(dimension_semantics=("arbitrary",)),
        name="prenorm",
    )(*x_arrays, pp)


class _Job(NamedTuple):
    src: jax.Array
    in_spec: pl.BlockSpec
    out_spec: pl.BlockSpec
    out_shape: jax.ShapeDtypeStruct
    body: Callable
    n_blocks: int


def _row_block(rows, rows_out, steps):
    g = math.gcd(rows, rows_out)
    for rb in range(BF16_SUBLANES, g + 1, BF16_SUBLANES):
        if g % rb == 0 and rows_out // rb <= steps:
            return rb
    raise ValueError(f"no row block for {rows}->{rows_out} rows in {steps} steps")


def _cast_job(src, lead, rows_out=None):
    rows, cols = src.shape[-2:]
    rows_out = rows_out or rows

    def make(steps, lin):
        rb = _row_block(rows, rows_out, steps)
        nb_in, nb_out = rows // rb, rows_out // rb

        def body(src_ref, dst_ref, blk):
            v = src_ref[...].astype(BF16)
            if nb_out > nb_in:
                v = jnp.where(blk < nb_in, v, jnp.zeros_like(v))
            dst_ref[...] = v

        return _Job(
            src,
            pl.BlockSpec((None,) * len(lead) + (rb, cols),
                         lambda *g: lead + (jnp.minimum(lin(*g), nb_in - 1), 0)),
            pl.BlockSpec((rb, cols), lambda *g: (jnp.minimum(lin(*g), nb_out - 1), 0)),
            jax.ShapeDtypeStruct((rows_out, cols), BF16), body, nb_out)

    return make


def _ffn_split_job(ffn_w_in, lead):
    def make(steps, lin):
        rb = _row_block(D_MODEL, D_MODEL, steps)
        nb = D_MODEL // rb

        def body(src_ref, dst_ref, blk):
            del blk
            for part in range(2):
                dst_ref[part, :, :D_FF] = src_ref[:, part * D_FF:(part + 1) * D_FF].astype(BF16)
                dst_ref[part, :, D_FF:] = jnp.zeros((rb, D_FF_PAD - D_FF), BF16)

        return _Job(
            ffn_w_in,
            pl.BlockSpec((None,) * len(lead) + (rb, 2 * D_FF), lambda *g: lead + (jnp.minimum(lin(*g), nb - 1), 0)),
            pl.BlockSpec((2, rb, D_FF_PAD), lambda *g: (0, jnp.minimum(lin(*g), nb - 1), 0)),
            jax.ShapeDtypeStruct((2, D_MODEL, D_FF_PAD), BF16), body, nb)

    return make


def _hosted_call(main, main_args, main_in_specs, out_spec, out_shape, grid, job_makers, name):
    steps = math.prod(grid)
    strides = [math.prod(grid[d + 1:]) for d in range(len(grid))]
    lin = lambda *g: sum(gi * st for gi, st in zip(g, strides))
    jobs = [mk(steps, lin) for mk in job_makers]
    n_in, n_jobs = len(main_args), len(jobs)

    def kern(*refs):
        if main is not None:
            main(*refs[:n_in], refs[n_in + n_jobs])
        t = lin(*[pl.program_id(d) for d in range(len(grid))])
        n_main_out = 0 if main is None else 1
        for q, jb in enumerate(jobs):
            jb.body(refs[n_in + q], refs[n_in + n_jobs + n_main_out + q], jnp.minimum(t, jb.n_blocks - 1))

    main_out = [] if main is None else [(out_spec, out_shape)]
    res = pl.pallas_call(
        kern,
        grid=grid,
        in_specs=list(main_in_specs) + [jb.in_spec for jb in jobs],
        out_specs=[s for s, _ in main_out] + [jb.out_spec for jb in jobs],
        out_shape=[s for _, s in main_out] + [jb.out_shape for jb in jobs],
        compiler_params=_params(dimension_semantics=("arbitrary",) * len(grid)),
        name=name,
    )(*main_args, *[jb.src for jb in jobs])
    return (None, list(res)) if main is None else (res[0], list(res[1:]))


def _convert(job_maker, steps=32):
    return _hosted_call(None, (), (), None, None, (steps,), [job_maker], "convert")[1][0]


def _swiglu_kernel(h_ref, wg_ref, wu_ref, o_ref):
    h = h_ref[...]
    g = jnp.dot(h, wg_ref[...], preferred_element_type=F32)
    u = jnp.dot(h, wu_ref[...], preferred_element_type=F32)
    o_ref[...] = (g * jax.nn.sigmoid(g) * u).astype(BF16)


def _ffn_in(h, wgu, jobs, tn=FF_TILE):
    rows = h.shape[0]
    tm = rows // IN_ROW_TILES
    return _hosted_call(
        _swiglu_kernel, (h, wgu, wgu),
        [pl.BlockSpec((tm, D_MODEL), lambda j, i: (i, 0)),
         pl.BlockSpec((None, D_MODEL, tn), lambda j, i: (0, 0, j)),
         pl.BlockSpec((None, D_MODEL, tn), lambda j, i: (1, 0, j))],
        pl.BlockSpec((tm, tn), lambda j, i: (i, j)),
        jax.ShapeDtypeStruct((rows, D_FF_PAD), BF16),
        (D_FF_PAD // tn, rows // tm), jobs, "ffn_in")


def _gelu_kernel(h_ref, w_ref, o_ref):
    y = jnp.dot(h_ref[...], w_ref[...], preferred_element_type=F32)
    o_ref[...] = jax.nn.gelu(y).astype(BF16)


def _gmlp_in(h, w, jobs, tn=1024):
    rows = h.shape[0]
    tm = rows // IN_ROW_TILES
    n = w.shape[1]
    return _hosted_call(
        _gelu_kernel, (h, w),
        [pl.BlockSpec((tm, D_MODEL), lambda j, i: (i, 0)),
         pl.BlockSpec((D_MODEL, tn), lambda j, i: (0, j))],
        pl.BlockSpec((tm, tn), lambda j, i: (i, j)),
        jax.ShapeDtypeStruct((rows, n), BF16),
        (n // tn, rows // tm), jobs, "gmlp_in")


def _qkv_rope_kernel(h_ref, w_ref, cos_ref, sin_ref, swap_ref, o_ref, *, n_rope_tiles, tn):
    j = pl.program_id(1)

    @pl.when(j < n_rope_tiles)
    def _():
        h = h_ref[...]
        c = cos_ref[...]
        s = sin_ref[...]
        swap = swap_ref[...]
        y = jnp.dot(h, w_ref[...], preferred_element_type=F32)
        hi = y.astype(BF16)
        lo = (y - hi.astype(F32)).astype(BF16)
        for q0 in range(0, tn, MXU_WIDTH):
            cs = slice(q0, q0 + MXU_WIDTH)
            partner = (jnp.dot(hi[:, cs], swap, preferred_element_type=F32)
                       + jnp.dot(lo[:, cs], swap, preferred_element_type=F32))
            o_ref[:, cs] = (y[:, cs] * c + partner * s).astype(BF16)

    @pl.when(j >= n_rope_tiles)
    def _():
        o_ref[...] = jnp.dot(h_ref[...], w_ref[...], preferred_element_type=F32).astype(BF16)


def _qkv_rope(h, w, cos_t, sin_t, rope_width, jobs, tn=512):
    rows = h.shape[0]
    tm = rows // IN_ROW_TILES
    n = w.shape[1]
    kern = functools.partial(_qkv_rope_kernel, n_rope_tiles=rope_width // tn, tn=tn)
    lane = jnp.arange(MXU_WIDTH)
    swap = (lane[:, None] == (lane[None, :] ^ 1)).astype(BF16)
    return _hosted_call(
        kern, (h, w, cos_t, sin_t, swap),
        [pl.BlockSpec((tm, D_MODEL), lambda i, j: (i, 0)),
         pl.BlockSpec((D_MODEL, tn), lambda i, j: (0, j)),
         pl.BlockSpec((tm, MXU_WIDTH), lambda i, j: (i, 0)),
         pl.BlockSpec((tm, MXU_WIDTH), lambda i, j: (i, 0)),
         pl.BlockSpec((MXU_WIDTH, MXU_WIDTH), lambda i, j: (0, 0))],
        pl.BlockSpec((tm, tn), lambda i, j: (i, j)),
        jax.ShapeDtypeStruct((rows, n), BF16),
        (rows // tm, n // tn), jobs, "qkv_rope")


def _mm_out_kernel(a_ref, w_ref, p_ref, *refs, coef, n_x, want_h):
    x_refs, xo_ref, maybe_h_ref = refs[:n_x], refs[n_x], refs[n_x + 1:]
    post = coef * (p_ref[P_GATE:P_GATE + 1, :] * p_ref[P_POST_G:P_POST_G + 1, :])
    half = a_ref.shape[0] // 2
    for rs in (slice(0, half), slice(half, 2 * half)):
        y = jnp.dot(a_ref[rs, :], w_ref[...], preferred_element_type=F32)
        xn = _stream_tile(x_refs, rs) + _rms(y) * post
        xo_ref[rs, :] = xn
        if want_h:
            maybe_h_ref[0][rs, :] = _pre(xn, p_ref).astype(BF16)


def _mm_out(a, w, x, pp, s, *, rows, coef, want_h):
    k = a.shape[1]
    tm = 512 if k <= D_MODEL else 256
    x_arrays, x_specs = _stream_specs(x, tm)
    out_shape = [jax.ShapeDtypeStruct((rows, D_MODEL), F32)]
    out_specs = [pl.BlockSpec((tm, D_MODEL), lambda i: (i, 0))]
    if want_h:
        out_shape.append(jax.ShapeDtypeStruct((rows, D_MODEL), BF16))
        out_specs.append(pl.BlockSpec((tm, D_MODEL), lambda i: (i, 0)))
    res = pl.pallas_call(
        functools.partial(_mm_out_kernel, coef=coef, n_x=len(x_arrays), want_h=want_h),
        grid=(rows // tm,),
        in_specs=[
            pl.BlockSpec((tm, k), lambda i: (i, 0)),
            pl.BlockSpec((k, D_MODEL), lambda i: (0, 0), pipeline_mode=pl.Buffered(1)),
            pl.BlockSpec((None, None, 8, D_MODEL), lambda i: (s, _group_of_block(i, tm), 0, 0)),
        ] + x_specs,
        out_specs=out_specs,
        out_shape=out_shape,
        compiler_params=_params(dimension_semantics=("arbitrary",)),
        name="mm_out",
    )(a, w, pp, *x_arrays)
    return (res[0], res[1]) if want_h else (res[0], None)


def _dot_t(a, b):
    return lax.dot_general(a, b, (((1,), (1,)), ((), ())), preferred_element_type=F32)


def _dot_tn(a, b):
    return lax.dot_general(a, b, (((0,), (0,)), ((), ())), preferred_element_type=F32)


def _win_attn_kernel(sink_ref, q_ref, kv_ref, kvc_ref, o_ref):
    n = pl.program_id(1)
    n_lat = SEQ // A_BLOCK
    kvw = A_KV_HEADS * HEAD_DIM
    scale = HEAD_DIM ** -0.5
    expo = scale * math.log2(math.e)
    gq = A_GROUP * A_BLOCK
    lane = lax.broadcasted_iota(jnp.int32, (1, gq), 1)
    qi = lax.broadcasted_iota(jnp.int32, (A_BLOCK, gq), 1) & (A_BLOCK - 1)
    mi = lax.broadcasted_iota(jnp.int32, (A_BLOCK, gq), 0)

    def heads(kk):
        q4 = jnp.concatenate(
            [q_ref[:, (kk * A_GROUP + g) * HEAD_DIM:(kk * A_GROUP + g + 1) * HEAD_DIM] for g in range(A_GROUP)],
            axis=0)
        sink = jnp.full((1, gq), sink_ref[kk * A_GROUP] / scale, F32)
        for g in range(1, A_GROUP):
            sink = jnp.where(lane >= g * A_BLOCK, sink_ref[kk * A_GROUP + g] / scale, sink)
        return q4, sink, slice(kk * HEAD_DIM, (kk + 1) * HEAD_DIM), slice(kvw + kk * HEAD_DIM, kvw + (kk + 1) * HEAD_DIM)

    def finish(kk, pieces, sink):
        m = sink
        for s, _ in pieces:
            m = jnp.maximum(m, jnp.max(s, axis=0, keepdims=True))
        den = jnp.exp2((sink - m) * expo)
        ot = None
        for s, v in pieces:
            p = jnp.exp2((s - m) * expo)
            den = den + jnp.sum(p, axis=0, keepdims=True)
            pv = _dot_tn(v, p.astype(BF16))
            ot = pv if ot is None else ot + pv
        ot = ot / den
        for g in range(A_GROUP):
            h = kk * A_GROUP + g
            o_ref[:, h * HEAD_DIM:(h + 1) * HEAD_DIM] = ot[:, g * A_BLOCK:(g + 1) * A_BLOCK].T.astype(BF16)

    @pl.when(n < n_lat)
    def _():
        off_l = jnp.where(n > 0, 0, 2 * A_BLOCK)
        off_r = jnp.where(n < n_lat - 1, 0, 2 * A_BLOCK)
        ok_l = mi >= qi + off_l
        ok_r = mi <= qi - off_r
        rows_l, rows_m, rows_r = [pl.ds(pl.multiple_of(jnp.clip(n + d, 0, n_lat - 1) * A_BLOCK, A_BLOCK), A_BLOCK)
                                  for d in (-1, 0, 1)]
        def scores(kk):
            q4, sink, kc, vc = heads(kk)
            s_l = jnp.where(ok_l, _dot_t(kv_ref[rows_l, kc], q4), NEG_INF)
            s_r = jnp.where(ok_r, _dot_t(kv_ref[rows_r, kc], q4), NEG_INF)
            return ([(_dot_t(kvc_ref[:, kc], q4), kvc_ref[:, vc]), (s_l, kv_ref[rows_l, vc]),
                     (_dot_t(kv_ref[rows_m, kc], q4), kv_ref[rows_m, vc]), (s_r, kv_ref[rows_r, vc])], sink)

        nxt = scores(0)
        for kk in range(A_KV_HEADS):
            cur = nxt
            if kk + 1 < A_KV_HEADS:
                nxt = scores(kk + 1)
            finish(kk, *cur)

    @pl.when(n >= n_lat)
    def _():
        for kk in range(A_KV_HEADS):
            q4, sink, kc, vc = heads(kk)
            finish(kk, [(_dot_t(kvc_ref[:, kc], q4), kvc_ref[:, vc])], sink)


def _win_attn(qkv, sink, with_ctx):
    n_lat = SEQ // A_BLOCK
    n_ctx = CTX_LEN // A_BLOCK
    steps = n_lat + (n_ctx if with_ctx else 0)
    qw = A_HEADS * HEAD_DIM
    kvw2 = 2 * A_KV_HEADS * HEAD_DIM
    ctx_blk0 = LAT_ROWS // A_BLOCK

    def qrow(b, n):
        return jnp.where(n < n_lat, b * n_lat + n, ctx_blk0 + b * n_ctx + (n - n_lat))

    rows = LAT_ROWS + (CTX_ROWS if with_ctx else 0)
    return pl.pallas_call(
        _win_attn_kernel,
        grid=(BATCH, steps),
        in_specs=[
            pl.BlockSpec(memory_space=pltpu.SMEM),
            pl.BlockSpec((A_BLOCK, qw), lambda b, n: (qrow(b, n), 0)),
            pl.BlockSpec((SEQ, kvw2), lambda b, n: (b, qw // kvw2)),
            pl.BlockSpec((CTX_LEN, kvw2), lambda b, n: (LAT_ROWS // CTX_LEN + b, qw // kvw2)),
        ],
        out_specs=pl.BlockSpec((A_BLOCK, D_MODEL), lambda b, n: (qrow(b, n), 0)),
        out_shape=jax.ShapeDtypeStruct((rows, D_MODEL), BF16),
        compiler_params=_params(dimension_semantics=("arbitrary", "arbitrary")),
        name="win_attn",
    )(sink, qkv, qkv, qkv)


def _gmlp_mid_kernel(u_ref, v_ref, g_ref, b_ref, ws_ref, bs_ref, o_ref, *, tm):
    v = v_ref[...].astype(F32)
    mu = jnp.mean(v, axis=-1, keepdims=True)
    vc = v - mu
    vn = vc * lax.rsqrt(jnp.mean(vc * vc, axis=-1, keepdims=True) + NORM_EPS)
    vn = (vn * g_ref[...] + b_ref[...]).astype(BF16)
    for g in range(B_GROUPS):
        cs = slice(g * B_GROUP_W, (g + 1) * B_GROUP_W)
        ws = ws_ref[g].astype(BF16)
        for c in range(tm // B_CHUNK):
            rs = slice(c * B_CHUNK, (c + 1) * B_CHUNK)
            mixed = jnp.dot(ws, vn[rs, cs], preferred_element_type=F32) + bs_ref[:, g:g + 1]
            o_ref[rs, cs] = (u_ref[rs, cs].astype(F32) * mixed).astype(BF16)


def _gmlp_mid(z, vn_g, vn_b, ws, bs_t, tm=256):
    rows = z.shape[0]
    return pl.pallas_call(
        functools.partial(_gmlp_mid_kernel, tm=tm),
        grid=(rows // tm,),
        in_specs=[
            pl.BlockSpec((tm, B_WIDTH), lambda i: (i, 0)),
            pl.BlockSpec((tm, B_WIDTH), lambda i: (i, 1)),
            pl.BlockSpec((1, B_WIDTH), lambda i: (0, 0)),
            pl.BlockSpec((1, B_WIDTH), lambda i: (0, 0)),
            pl.BlockSpec((B_GROUPS, B_CHUNK, B_CHUNK), lambda i: (0, 0, 0)),
            pl.BlockSpec((B_CHUNK, B_GROUPS), lambda i: (0, 0)),
        ],
        out_specs=pl.BlockSpec((tm, B_WIDTH), lambda i: (i, 0)),
        out_shape=jax.ShapeDtypeStruct((rows, B_WIDTH), BF16),
        compiler_params=_params(dimension_semantics=("arbitrary",)),
        name="gmlp_mid",
    )(z, z, vn_g, vn_b, ws, bs_t)


def _diff_attn_kernel(lq1_ref, lk1_ref, lq2_ref, lk2_ref, sg_ref, q_ref, k_ref, v_ref, kc_ref, vc_ref, o_ref, vt_ref,
                      *, lam_init, tq, tk):
    qi = pl.program_id(2)
    n_lat = SEQ // tq
    expo = (HEAD_DIM ** -0.5) * math.log2(math.e)
    lam = (jnp.exp(jnp.sum(lq1_ref[...] * lk1_ref[...], axis=-1, keepdims=True))
           - jnp.exp(jnp.sum(lq2_ref[...] * lk2_ref[...], axis=-1, keepdims=True)) + lam_init)

    @pl.when(qi == 0)
    def _():
        vt_ref[:, :CTX_LEN] = vc_ref[...].T
        for t in range(SEQ // tk):
            vt_ref[:, CTX_LEN + t * tk:CTX_LEN + (t + 1) * tk] = v_ref[t * tk:(t + 1) * tk, :].T

    def run(chunks):
        qs = [q_ref[:, c * HEAD_DIM:(c + 1) * HEAD_DIM] for c in range(2)]

        def scores(chunk):
            kref, lo, size, _ = chunk
            return [_dot_t(kref[lo:lo + size, c * HEAD_DIM:(c + 1) * HEAD_DIM], qs[c]) for c in range(2)]

        m = [None, None]
        l = [None, None]
        acc = [None, None]
        s_next = scores(chunks[0])
        for t, (_, _, size, vlo) in enumerate(chunks):
            s_cur = s_next
            if t + 1 < len(chunks):
                s_next = scores(chunks[t + 1])
            for c in range(2):
                st = s_cur[c]
                m_new = jnp.max(st, axis=0, keepdims=True)
                if m[c] is not None:
                    m_new = jnp.maximum(m[c], m_new)
                p = jnp.exp2((st - m_new) * expo)
                pv = jnp.dot(vt_ref[:, vlo:vlo + size], p.astype(BF16), preferred_element_type=F32)
                if m[c] is None:
                    l[c] = jnp.sum(p, axis=0, keepdims=True)
                    acc[c] = pv
                else:
                    alpha = jnp.exp2((m[c] - m_new) * expo)
                    l[c] = alpha * l[c] + jnp.sum(p, axis=0, keepdims=True)
                    acc[c] = alpha * acc[c] + pv
                m[c] = m_new
        o = acc[0] / l[0] - lam * (acc[1] / l[1])
        o = o * lax.rsqrt(jnp.mean(o * o, axis=0, keepdims=True) + NORM_EPS) * (sg_ref[...] * (1.0 - lam_init))
        o_ref[...] = o.T.astype(BF16)

    @pl.when(qi < n_lat)
    def _():
        run([(kc_ref, 0, CTX_LEN, 0)] + [(k_ref, t * tk, tk, CTX_LEN + t * tk) for t in range(SEQ // tk)])

    @pl.when(qi >= n_lat)
    def _():
        run([(kc_ref, 0, CTX_LEN, 0)])


def _diff_attn(qkv, lq1, lk1, lq2, lk2, subln_g_col, lam_init, tq=256, tk=1024):
    hw = 2 * HEAD_DIM
    n_lat = SEQ // tq
    n_ctx = CTX_LEN // tq
    kcol = D_MODEL // hw
    ctx_blk0 = LAT_ROWS // tq

    def qrow(b, qi):
        return jnp.where(qi < n_lat, b * n_lat + qi, ctx_blk0 + b * n_ctx + (qi - n_lat))

    vec = pl.BlockSpec((1, HEAD_DIM), lambda b, h, qi: (0, 0))
    kern = functools.partial(_diff_attn_kernel, lam_init=lam_init, tq=tq, tk=tk)
    return pl.pallas_call(
        kern,
        grid=(BATCH, C_HEADS, n_lat + n_ctx),
        in_specs=[
            vec, vec, vec, vec,
            pl.BlockSpec((hw, 1), lambda b, h, qi: (0, 0)),
            pl.BlockSpec((tq, hw), lambda b, h, qi: (qrow(b, qi), h)),
            pl.BlockSpec((SEQ, hw), lambda b, h, qi: (b, kcol + h)),
            pl.BlockSpec((SEQ, hw), lambda b, h, qi: (b, 2 * kcol + h)),
            pl.BlockSpec((CTX_LEN, hw), lambda b, h, qi: (LAT_ROWS // CTX_LEN + b, kcol + h)),
            pl.BlockSpec((CTX_LEN, hw), lambda b, h, qi: (LAT_ROWS // CTX_LEN + b, 2 * kcol + h)),
        ],
        out_specs=pl.BlockSpec((tq, hw), lambda b, h, qi: (qrow(b, qi), h)),
        out_shape=jax.ShapeDtypeStruct((ROWS, D_MODEL), BF16),
        scratch_shapes=[pltpu.VMEM((hw, CTX_LEN + SEQ), BF16)],
        compiler_params=_params(dimension_semantics=("arbitrary", "arbitrary", "arbitrary")),
        name="diff_attn",
    )(lq1, lk1, lq2, lk2, subln_g_col, qkv, qkv, qkv, qkv, qkv)


def _rope_tables():
    rows = SEQ // GRID_W
    row = jnp.repeat(jnp.arange(rows, dtype=F32), GRID_W)
    col = jnp.tile(jnp.arange(GRID_W, dtype=F32), rows)
    axis_dim = HEAD_DIM // 2
    inv = ROPE_THETA ** (-jnp.arange(0, axis_dim, 2, dtype=F32) / axis_dim)
    ang = jnp.concatenate([row[:, None] * inv, col[:, None] * inv], axis=-1)
    cos = jnp.repeat(jnp.cos(ang), 2, axis=-1)
    sin = jnp.repeat(jnp.sin(ang), 2, axis=-1) * jnp.tile(jnp.array([-1.0, 1.0], F32), HEAD_DIM // 2)
    heads_per_chunk = MXU_WIDTH // HEAD_DIM
    cos_t = jnp.concatenate([jnp.tile(cos, (BATCH, heads_per_chunk)), jnp.ones((CTX_ROWS, MXU_WIDTH), F32)], axis=0)
    sin_t = jnp.concatenate([jnp.tile(sin, (BATCH, heads_per_chunk)), jnp.zeros((CTX_ROWS, MXU_WIDTH), F32)], axis=0)
    return cos_t, sin_t


def _param_tiles(mods, norm_g):
    m = mods[:, :N_GROUPS].reshape(DEPTH, N_GROUPS, N_MOD, D_MODEL)
    zero = jnp.zeros((N_GROUPS, D_MODEL), F32)
    bcast = lambda v: jnp.broadcast_to(v, (N_GROUPS, D_MODEL))
    tiles = []
    for s in range(-1, 3 * DEPTH):
        rows = [zero] * 8
        if s >= 0:
            i, slot = divmod(s, 3)
            rows[P_POST_G] = bcast(norm_g[i, 2 * slot + 1])
            rows[P_GATE] = m[i, :, 3 * slot + 2]
        if s + 1 < 3 * DEPTH:
            i, slot = divmod(s + 1, 3)
            rows[P_PRE_G] = bcast(norm_g[i, 2 * slot])
            rows[P_SCALE] = m[i, :, 3 * slot + 1]
            rows[P_SHIFT] = m[i, :, 3 * slot]
        tiles.append(jnp.stack(rows, axis=1))
    return jnp.stack(tiles, axis=0)


def kernel(x, c, ctx, c_ctx, ada_w, ada_b, norm_g, ffn_w_in, ffn_w_out, a_w_in, a_w_out, a_sink, b_w_in, b_vnorm_g,
           b_vnorm_b, b_ws, b_bs, b_w_out, c_w_in, c_w_out, c_lq1, c_lk1, c_lq2, c_lk2, c_subln_g):
    xs = (x.reshape(LAT_ROWS, D_MODEL), ctx.reshape(CTX_ROWS, D_MODEL))
    cvec = jnp.concatenate([c, c_ctx[None, :], jnp.zeros((8 - BATCH - 1, D_MODEL), F32)], axis=0)
    pp = _param_tiles(_ada_mods(cvec, ada_w, ada_b), norm_g)
    cos_t, sin_t = _rope_tables()

    mixer_w = ((a_w_in, a_w_out), (b_w_in, b_w_out), (c_w_in, c_w_out))
    wgu = _convert(_ffn_split_job(ffn_w_in, (0, 0)))
    h = _prenorm(xs, pp, 0)
    for i in range(DEPTH):
        kind, j = i % N_MIXERS, i // N_MIXERS
        ctx_live = i < DEPTH - 1
        rows_out = ROWS if ctx_live else LAT_ROWS
        s0 = 3 * i + 1
        w_mix_in, w_mix_out = mixer_w[kind]

        ffn2_split = _ffn_split_job(ffn_w_in, (i, 1))
        jobs = [_cast_job(ffn_w_out, (i, 0), D_FF_PAD), _cast_job(w_mix_in, (j,))] + ([ffn2_split] if kind != 1 else [])
        a, conv = _ffn_in(h, wgu, jobs)
        xs, h = _mm_out(a, conv[0], xs, pp, s0, rows=ROWS, coef=0.5, want_h=True)
        w_i = conv[1]

        jobs = [_cast_job(w_mix_out, (j,))] + ([ffn2_split] if kind == 1 else [])
        if kind == 0:
            qkv, mix_conv = _qkv_rope(h, w_i, cos_t, sin_t, (A_HEADS + A_KV_HEADS) * HEAD_DIM, jobs)
            o = _win_attn(qkv, a_sink[j], ctx_live)
        elif kind == 1:
            z, mix_conv = _gmlp_in(h, w_i, jobs)
            o = _gmlp_mid(z, b_vnorm_g[j][None, :], b_vnorm_b[j][None, :], b_ws[j], b_bs[j].T)
        else:
            lam_init = 0.8 - 0.6 * math.exp(-0.3 * i)
            qkv, mix_conv = _qkv_rope(h, w_i, cos_t, sin_t, 2 * D_MODEL, jobs)
            o = _diff_attn(qkv, c_lq1[j][None, :], c_lk1[j][None, :], c_lq2[j][None, :], c_lk2[j][None, :],
                           c_subln_g[j][:, None], lam_init)
        w_o = mix_conv[0]
        wgu = mix_conv[1] if kind == 1 else conv[2]
        xs, h = _mm_out(o, w_o, xs, pp, s0 + 1, rows=rows_out, coef=1.0, want_h=True)

        jobs = [_cast_job(ffn_w_out, (i, 1), D_FF_PAD)]
        if i + 1 < DEPTH:
            jobs.append(_ffn_split_job(ffn_w_in, (i + 1, 0)))
        a, conv = _ffn_in(h, wgu, jobs)
        xs, h = _mm_out(a, conv[0], xs, pp, s0 + 2, rows=rows_out, coef=0.5, want_h=i + 1 < DEPTH)
        wgu = conv[1] if i + 1 < DEPTH else None
    return xs.reshape(BATCH, SEQ, D_MODEL)
```

```python
import functools
import math
from typing import Callable, NamedTuple

import jax
import jax.numpy as jnp
from jax import lax
from jax.experimental import pallas as pl
from jax.experimental.pallas import tpu as pltpu

D_MODEL = 2048
BATCH = 2
SEQ = 4096
DEPTH = 4
GRID_W = 64
CTX_LEN = 256
N_MIXERS = 3
N_MOD = 9
NORM_EPS = 1e-6
ROPE_THETA = 10000.0
NEG_INF = -1e30
D_FF = 5504
HEAD_DIM = 128
A_HEADS = 16
A_KV_HEADS = 4
A_GROUP = 4
A_BLOCK = 128
B_CHUNK = 128
B_WIDTH = 3 * D_MODEL
B_GROUPS = 8
B_GROUP_W = B_WIDTH // B_GROUPS
C_HEADS = 8

LAT_ROWS = BATCH * SEQ
CTX_ROWS = BATCH * CTX_LEN
ROWS = LAT_ROWS + CTX_ROWS
N_GROUPS = 3
LANES = 128
BF16_SUBLANES = 16
MXU_WIDTH = 256
IN_ROW_TILES = 8
FF_TILE = 512
D_FF_PAD = -(-D_FF // FF_TILE) * FF_TILE
VMEM_LIMIT = 56 * 1024 * 1024

BF16 = jnp.bfloat16
F32 = jnp.float32

P_POST_G, P_GATE, P_PRE_G, P_SCALE, P_SHIFT = 0, 1, 2, 3, 4


def _params(**kw):
    return pltpu.CompilerParams(vmem_limit_bytes=VMEM_LIMIT, **kw)


def _group_of_block(i, tm):
    return jnp.minimum(i // (SEQ // tm), N_GROUPS - 1)


def _rms(x):
    return x * lax.rsqrt(jnp.mean(x * x, axis=-1, keepdims=True) + NORM_EPS)


def _ada_kernel(c_ref, w_ref, b_ref, o_ref):
    @pl.when(pl.program_id(1) == 0)
    def _():
        o_ref[...] = jnp.broadcast_to(b_ref[...], o_ref.shape)

    c = c_ref[...]
    a = (c * jax.nn.sigmoid(c)).astype(BF16)
    o_ref[...] += jnp.dot(a, w_ref[...].astype(BF16), preferred_element_type=F32)


def _ada_mods(cvec, ada_w, ada_b):
    kb = LANES
    n = N_MOD * D_MODEL
    return pl.pallas_call(
        _ada_kernel,
        grid=(DEPTH, D_MODEL // kb),
        in_specs=[
            pl.BlockSpec((8, kb), lambda l, k: (0, k)),
            pl.BlockSpec((None, kb, n), lambda l, k: (l, k, 0)),
            pl.BlockSpec((None, 1, n), lambda l, k: (l, 0, 0)),
        ],
        out_specs=pl.BlockSpec((None, 8, n), lambda l, k: (l, 0, 0)),
        out_shape=jax.ShapeDtypeStruct((DEPTH, 8, n), F32),
        compiler_params=_params(dimension_semantics=("arbitrary", "arbitrary")),
        name="ada_mods",
    )(cvec, ada_w, ada_b.reshape(DEPTH, 1, n))


def _pre(x, p_ref):
    gain = p_ref[P_PRE_G:P_PRE_G + 1, :] * (1.0 + p_ref[P_SCALE:P_SCALE + 1, :])
    return _rms(x) * gain + p_ref[P_SHIFT:P_SHIFT + 1, :]


def _stream_specs(x, tm):
    if not isinstance(x, tuple):
        return (x,), [pl.BlockSpec((tm, D_MODEL), lambda i: (i, 0))]
    n_lat = LAT_ROWS // tm
    return x, [pl.BlockSpec((tm, D_MODEL), lambda i: (jnp.minimum(i, n_lat - 1), 0)),
               pl.BlockSpec((tm, D_MODEL), lambda i: (jnp.maximum(i - n_lat, 0), 0))]


def _stream_tile(x_refs, rs=slice(None)):
    if len(x_refs) == 1:
        return x_refs[0][rs, :]
    n_lat = LAT_ROWS // x_refs[0].shape[0]
    return jnp.where(pl.program_id(0) < n_lat, x_refs[0][rs, :], x_refs[1][rs, :])


def _prenorm_kernel(*refs):
    *x_refs, p_ref, h_ref = refs
    h_ref[...] = _pre(_stream_tile(x_refs), p_ref).astype(BF16)


def _prenorm(x, pp, s, tm=512):
    x_arrays, x_specs = _stream_specs(x, tm)
    return pl.pallas_call(
        _prenorm_kernel,
        grid=(ROWS // tm,),
        in_specs=x_specs + [
            pl.BlockSpec((None, None, 8, D_MODEL), lambda i: (s, _group_of_block(i, tm), 0, 0)),
        ],
        out_specs=pl.BlockSpec((tm, D_MODEL), lambda i: (i, 0)),
        out_shape=jax.ShapeDtypeStruct((ROWS, D_MODEL), BF16),
        compiler_params=_params(dimension_semantics=("arbitrary",)),
        name="prenorm",
    )(*x_arrays, pp)


class _Job(NamedTuple):
    src: jax.Array
    in_spec: pl.BlockSpec
    out_spec: pl.BlockSpec
    out_shape: jax.ShapeDtypeStruct
    body: Callable
    n_blocks: int


def _row_block(rows, rows_out, steps):
    g = math.gcd(rows, rows_out)
    for rb in range(BF16_SUBLANES, g + 1, BF16_SUBLANES):
        if g % rb == 0 and rows_out // rb <= steps:
            return rb
    raise ValueError(f"no row block for {rows}->{rows_out} rows in {steps} steps")


def _cast_job(src, lead, rows_out=None):
    rows, cols = src.shape[-2:]
    rows_out = rows_out or rows

    def make(steps, lin):
        rb = _row_block(rows, rows_out, steps)
        nb_in, nb_out = rows // rb, rows_out // rb

        def body(src_ref, dst_ref, blk):
            v = src_ref[...].astype(BF16)
            if nb_out > nb_in:
                v = jnp.where(blk < nb_in, v, jnp.zeros_like(v))
            dst_ref[...] = v

        return _Job(
            src,
            pl.BlockSpec((None,) * len(lead) + (rb, cols),
                         lambda *g: lead + (jnp.minimum(lin(*g), nb_in - 1), 0)),
            pl.BlockSpec((rb, cols), lambda *g: (jnp.minimum(lin(*g), nb_out - 1), 0)),
            jax.ShapeDtypeStruct((rows_out, cols), BF16), body, nb_out)

    return make


def _ffn_split_job(ffn_w_in, lead):
    def make(steps, lin):
        rb = _row_block(D_MODEL, D_MODEL, steps)
        nb = D_MODEL // rb

        def body(src_ref, dst_ref, blk):
            del blk
            for part in range(2):
                dst_ref[part, :, :D_FF] = src_ref[:, part * D_FF:(part + 1) * D_FF].astype(BF16)
                dst_ref[part, :, D_FF:] = jnp.zeros((rb, D_FF_PAD - D_FF), BF16)

        return _Job(
            ffn_w_in,
            pl.BlockSpec((None,) * len(lead) + (rb, 2 * D_FF), lambda *g: lead + (jnp.minimum(lin(*g), nb - 1), 0)),
            pl.BlockSpec((2, rb, D_FF_PAD), lambda *g: (0, jnp.minimum(lin(*g), nb - 1), 0)),
            jax.ShapeDtypeStruct((2, D_MODEL, D_FF_PAD), BF16), body, nb)

    return make


def _hosted_call(main, main_args, main_in_specs, out_spec, out_shape, grid, job_makers, name):
    steps = math.prod(grid)
    strides = [math.prod(grid[d + 1:]) for d in range(len(grid))]
    lin = lambda *g: sum(gi * st for gi, st in zip(g, strides))
    jobs = [mk(steps, lin) for mk in job_makers]
    n_in, n_jobs = len(main_args), len(jobs)

    def kern(*refs):
        if main is not None:
            main(*refs[:n_in], refs[n_in + n_jobs])
        t = lin(*[pl.program_id(d) for d in range(len(grid))])
        n_main_out = 0 if main is None else 1
        for q, jb in enumerate(jobs):
            jb.body(refs[n_in + q], refs[n_in + n_jobs + n_main_out + q], jnp.minimum(t, jb.n_blocks - 1))

    main_out = [] if main is None else [(out_spec, out_shape)]
    res = pl.pallas_call(
        kern,
        grid=grid,
        in_specs=list(main_in_specs) + [jb.in_spec for jb in jobs],
        out_specs=[s for s, _ in main_out] + [jb.out_spec for jb in jobs],
        out_shape=[s for _, s in main_out] + [jb.out_shape for jb in jobs],
        compiler_params=_params(dimension_semantics=("arbitrary",) * len(grid)),
        name=name,
    )(*main_args, *[jb.src for jb in jobs])
    return (None, list(res)) if main is None else (res[0], list(res[1:]))


def _convert(job_maker, steps=32):
    return _hosted_call(None, (), (), None, None, (steps,), [job_maker], "convert")[1][0]


def _swiglu_kernel(h_ref, wg_ref, wu_ref, o_ref):
    h = h_ref[...]
    half = o_ref.shape[1] // 2
    parts = []
    for cs in (slice(0, half), slice(half, 2 * half)):
        parts.append((cs, jnp.dot(h, wg_ref[:, cs], preferred_element_type=F32),
                      jnp.dot(h, wu_ref[:, cs], preferred_element_type=F32)))
    for cs, g, u in parts:
        o_ref[:, cs] = (g * jax.nn.sigmoid(g) * u).astype(BF16)


def _ffn_in(h, wgu, jobs, tn=FF_TILE):
    rows = h.shape[0]
    tm = rows // IN_ROW_TILES
    return _hosted_call(
        _swiglu_kernel, (h, wgu, wgu),
        [pl.BlockSpec((tm, D_MODEL), lambda j, i: (i, 0)),
         pl.BlockSpec((None, D_MODEL, tn), lambda j, i: (0, 0, j)),
         pl.BlockSpec((None, D_MODEL, tn), lambda j, i: (1, 0, j))],
        pl.BlockSpec((tm, tn), lambda j, i: (i, j)),
        jax.ShapeDtypeStruct((rows, D_FF_PAD), BF16),
        (D_FF_PAD // tn, rows // tm), jobs, "ffn_in")


def _gelu_tanh(y):
    c0 = math.sqrt(2.0 / math.pi)
    half_y = 0.5 * y
    return half_y + half_y * jnp.tanh(y * (c0 + (c0 * 0.044715) * (y * y)))


def _gelu_kernel(h_ref, w_ref, o_ref):
    half = h_ref.shape[0] // 2
    halves = (slice(0, half), slice(half, 2 * half))
    ys = [jnp.dot(h_ref[rs, :], w_ref[...], preferred_element_type=F32) for rs in halves]
    for rs, y in zip(halves, ys):
        o_ref[rs, :] = _gelu_tanh(y).astype(BF16)


def _gmlp_in(h, w, jobs, tn=1024):
    rows = h.shape[0]
    tm = rows // IN_ROW_TILES
    n = w.shape[1]
    return _hosted_call(
        _gelu_kernel, (h, w),
        [pl.BlockSpec((tm, D_MODEL), lambda j, i: (i, 0)),
         pl.BlockSpec((D_MODEL, tn), lambda j, i: (0, j))],
        pl.BlockSpec((tm, tn), lambda j, i: (i, j)),
        jax.ShapeDtypeStruct((rows, n), BF16),
        (n // tn, rows // tm), jobs, "gmlp_in")


def _qkv_rope_kernel(h_ref, w_ref, cos_ref, sin_ref, swap_ref, o_ref, *, n_rope_tiles, tn):
    j = pl.program_id(1)

    @pl.when(j < n_rope_tiles)
    def _():
        h = h_ref[...]
        c = cos_ref[...]
        s = sin_ref[...]
        swap = swap_ref[...]
        y = jnp.dot(h, w_ref[...], preferred_element_type=F32)
        hi = y.astype(BF16)
        lo = (y - hi.astype(F32)).astype(BF16)
        for q0 in range(0, tn, MXU_WIDTH):
            cs = slice(q0, q0 + MXU_WIDTH)
            partner = (jnp.dot(hi[:, cs], swap, preferred_element_type=F32)
                       + jnp.dot(lo[:, cs], swap, preferred_element_type=F32))
            o_ref[:, cs] = (y[:, cs] * c + partner * s).astype(BF16)

    @pl.when(j >= n_rope_tiles)
    def _():
        o_ref[...] = jnp.dot(h_ref[...], w_ref[...], preferred_element_type=F32).astype(BF16)


def _qkv_rope(h, w, cos_t, sin_t, rope_width, jobs, tn=512):
    rows = h.shape[0]
    tm = rows // IN_ROW_TILES
    n = w.shape[1]
    kern = functools.partial(_qkv_rope_kernel, n_rope_tiles=rope_width // tn, tn=tn)
    lane = jnp.arange(MXU_WIDTH)
    swap = (lane[:, None] == (lane[None, :] ^ 1)).astype(BF16)
    return _hosted_call(
        kern, (h, w, cos_t, sin_t, swap),
        [pl.BlockSpec((tm, D_MODEL), lambda i, j: (i, 0)),
         pl.BlockSpec((D_MODEL, tn), lambda i, j: (0, j)),
         pl.BlockSpec((tm, MXU_WIDTH), lambda i, j: (i, 0)),
         pl.BlockSpec((tm, MXU_WIDTH), lambda i, j: (i, 0)),
         pl.BlockSpec((MXU_WIDTH, MXU_WIDTH), lambda i, j: (0, 0))],
        pl.BlockSpec((tm, tn), lambda i, j: (i, j)),
        jax.ShapeDtypeStruct((rows, n), BF16),
        (rows // tm, n // tn), jobs, "qkv_rope")


def _mm_out_kernel(a_ref, w_ref, p_ref, *refs, coef, n_x, want_h):
    x_refs, xo_ref, maybe_h_ref = refs[:n_x], refs[n_x], refs[n_x + 1:]
    post = coef * (p_ref[P_GATE:P_GATE + 1, :] * p_ref[P_POST_G:P_POST_G + 1, :])
    half = a_ref.shape[0] // 2
    for rs in (slice(0, half), slice(half, 2 * half)):
        y = jnp.dot(a_ref[rs, :], w_ref[...], preferred_element_type=F32)
        xn = _stream_tile(x_refs, rs) + _rms(y) * post
        xo_ref[rs, :] = xn
        if want_h:
            maybe_h_ref[0][rs, :] = _pre(xn, p_ref).astype(BF16)


def _mm_out(a, w, x, pp, s, *, rows, coef, want_h):
    k = a.shape[1]
    tm = 512 if k <= D_MODEL else 256
    x_arrays, x_specs = _stream_specs(x, tm)
    out_shape = [jax.ShapeDtypeStruct((rows, D_MODEL), F32)]
    out_specs = [pl.BlockSpec((tm, D_MODEL), lambda i: (i, 0))]
    if want_h:
        out_shape.append(jax.ShapeDtypeStruct((rows, D_MODEL), BF16))
        out_specs.append(pl.BlockSpec((tm, D_MODEL), lambda i: (i, 0)))
    res = pl.pallas_call(
        functools.partial(_mm_out_kernel, coef=coef, n_x=len(x_arrays), want_h=want_h),
        grid=(rows // tm,),
        in_specs=[
            pl.BlockSpec((tm, k), lambda i: (i, 0)),
            pl.BlockSpec((k, D_MODEL), lambda i: (0, 0), pipeline_mode=pl.Buffered(1)),
            pl.BlockSpec((None, None, 8, D_MODEL), lambda i: (s, _group_of_block(i, tm), 0, 0)),
        ] + x_specs,
        out_specs=out_specs,
        out_shape=out_shape,
        compiler_params=_params(dimension_semantics=("arbitrary",)),
        name="mm_out",
    )(a, w, pp, *x_arrays)
    return (res[0], res[1]) if want_h else (res[0], None)


def _dot_t(a, b):
    return lax.dot_general(a, b, (((1,), (1,)), ((), ())), preferred_element_type=F32)


def _dot_tn(a, b):
    return lax.dot_general(a, b, (((0,), (0,)), ((), ())), preferred_element_type=F32)


def _win_attn_kernel(sink_ref, q_ref, kv_ref, kvc_ref, o_ref):
    n = pl.program_id(1)
    n_lat = SEQ // A_BLOCK
    kvw = A_KV_HEADS * HEAD_DIM
    scale = HEAD_DIM ** -0.5
    expo = scale * math.log2(math.e)
    gq = A_GROUP * A_BLOCK
    lane = lax.broadcasted_iota(jnp.int32, (1, gq), 1)
    qi = lax.broadcasted_iota(jnp.int32, (A_BLOCK, gq), 1) & (A_BLOCK - 1)
    mi = lax.broadcasted_iota(jnp.int32, (A_BLOCK, gq), 0)

    def heads(kk):
        q4 = jnp.concatenate(
            [q_ref[:, (kk * A_GROUP + g) * HEAD_DIM:(kk * A_GROUP + g + 1) * HEAD_DIM] for g in range(A_GROUP)],
            axis=0)
        sink = jnp.full((1, gq), sink_ref[kk * A_GROUP] / scale, F32)
        for g in range(1, A_GROUP):
            sink = jnp.where(lane >= g * A_BLOCK, sink_ref[kk * A_GROUP + g] / scale, sink)
        return q4, sink, slice(kk * HEAD_DIM, (kk + 1) * HEAD_DIM), slice(kvw + kk * HEAD_DIM, kvw + (kk + 1) * HEAD_DIM)

    def finish(kk, pieces, sink):
        m = sink
        for s, _ in pieces:
            m = jnp.maximum(m, jnp.max(s, axis=0, keepdims=True))
        den = jnp.exp2((sink - m) * expo)
        ot = None
        for s, v in pieces:
            p = jnp.exp2((s - m) * expo)
            den = den + jnp.sum(p, axis=0, keepdims=True)
            pv = _dot_tn(v, p.astype(BF16))
            ot = pv if ot is None else ot + pv
        ot = ot / den
        for g in range(A_GROUP):
            h = kk * A_GROUP + g
            o_ref[:, h * HEAD_DIM:(h + 1) * HEAD_DIM] = ot[:, g * A_BLOCK:(g + 1) * A_BLOCK].T.astype(BF16)

    @pl.when(n < n_lat)
    def _():
        off_l = jnp.where(n > 0, 0, 2 * A_BLOCK)
        off_r = jnp.where(n < n_lat - 1, 0, 2 * A_BLOCK)
        ok_l = mi >= qi + off_l
        ok_r = mi <= qi - off_r
        rows_l, rows_m, rows_r = [pl.ds(pl.multiple_of(jnp.clip(n + d, 0, n_lat - 1) * A_BLOCK, A_BLOCK), A_BLOCK)
                                  for d in (-1, 0, 1)]
        def scores(kk):
            q4, sink, kc, vc = heads(kk)
            s_l = jnp.where(ok_l, _dot_t(kv_ref[rows_l, kc], q4), NEG_INF)
            s_r = jnp.where(ok_r, _dot_t(kv_ref[rows_r, kc], q4), NEG_INF)
            return ([(_dot_t(kvc_ref[:, kc], q4), kvc_ref[:, vc]), (s_l, kv_ref[rows_l, vc]),
                     (_dot_t(kv_ref[rows_m, kc], q4), kv_ref[rows_m, vc]), (s_r, kv_ref[rows_r, vc])], sink)

        nxt = scores(0)
        for kk in range(A_KV_HEADS):
            cur = nxt
            if kk + 1 < A_KV_HEADS:
                nxt = scores(kk + 1)
            finish(kk, *cur)

    @pl.when(n >= n_lat)
    def _():
        for kk in range(A_KV_HEADS):
            q4, sink, kc, vc = heads(kk)
            finish(kk, [(_dot_t(kvc_ref[:, kc], q4), kvc_ref[:, vc])], sink)


def _win_attn(qkv, sink, with_ctx):
    n_lat = SEQ // A_BLOCK
    n_ctx = CTX_LEN // A_BLOCK
    steps = n_lat + (n_ctx if with_ctx else 0)
    qw = A_HEADS * HEAD_DIM
    kvw2 = 2 * A_KV_HEADS * HEAD_DIM
    ctx_blk0 = LAT_ROWS // A_BLOCK

    def qrow(b, n):
        return jnp.where(n < n_lat, b * n_lat + n, ctx_blk0 + b * n_ctx + (n - n_lat))

    rows = LAT_ROWS + (CTX_ROWS if with_ctx else 0)
    return pl.pallas_call(
        _win_attn_kernel,
        grid=(BATCH, steps),
        in_specs=[
            pl.BlockSpec(memory_space=pltpu.SMEM),
            pl.BlockSpec((A_BLOCK, qw), lambda b, n: (qrow(b, n), 0)),
            pl.BlockSpec((SEQ, kvw2), lambda b, n: (b, qw // kvw2)),
            pl.BlockSpec((CTX_LEN, kvw2), lambda b, n: (LAT_ROWS // CTX_LEN + b, qw // kvw2)),
        ],
        out_specs=pl.BlockSpec((A_BLOCK, D_MODEL), lambda b, n: (qrow(b, n), 0)),
        out_shape=jax.ShapeDtypeStruct((rows, D_MODEL), BF16),
        compiler_params=_params(dimension_semantics=("arbitrary", "arbitrary")),
        name="win_attn",
    )(sink, qkv, qkv, qkv)


def _gmlp_mid_kernel(u_ref, v_ref, g_ref, b_ref, ws_ref, bs_ref, o_ref, *, tm):
    v = v_ref[...].astype(F32)
    mu = jnp.mean(v, axis=-1, keepdims=True)
    vc = v - mu
    vn = vc * lax.rsqrt(jnp.mean(vc * vc, axis=-1, keepdims=True) + NORM_EPS)
    vn = (vn * g_ref[...] + b_ref[...]).astype(BF16)
    for g in range(B_GROUPS):
        cs = slice(g * B_GROUP_W, (g + 1) * B_GROUP_W)
        ws = ws_ref[g].astype(BF16)
        for c in range(tm // B_CHUNK):
            rs = slice(c * B_CHUNK, (c + 1) * B_CHUNK)
            mixed = jnp.dot(ws, vn[rs, cs], preferred_element_type=F32) + bs_ref[:, g:g + 1]
            o_ref[rs, cs] = (u_ref[rs, cs].astype(F32) * mixed).astype(BF16)


def _gmlp_mid(z, vn_g, vn_b, ws, bs_t, tm=256):
    rows = z.shape[0]
    return pl.pallas_call(
        functools.partial(_gmlp_mid_kernel, tm=tm),
        grid=(rows // tm,),
        in_specs=[
            pl.BlockSpec((tm, B_WIDTH), lambda i: (i, 0)),
            pl.BlockSpec((tm, B_WIDTH), lambda i: (i, 1)),
            pl.BlockSpec((1, B_WIDTH), lambda i: (0, 0)),
            pl.BlockSpec((1, B_WIDTH), lambda i: (0, 0)),
            pl.BlockSpec((B_GROUPS, B_CHUNK, B_CHUNK), lambda i: (0, 0, 0)),
            pl.BlockSpec((B_CHUNK, B_GROUPS), lambda i: (0, 0)),
        ],
        out_specs=pl.BlockSpec((tm, B_WIDTH), lambda i: (i, 0)),
        out_shape=jax.ShapeDtypeStruct((rows, B_WIDTH), BF16),
        compiler_params=_params(dimension_semantics=("arbitrary",)),
        name="gmlp_mid",
    )(z, z, vn_g, vn_b, ws, bs_t)


def _diff_attn_kernel(lq1_ref, lk1_ref, lq2_ref, lk2_ref, sg_ref, q_ref, k_ref, v_ref, kc_ref, vc_ref, o_ref, vt_ref,
                      *, lam_init, tq, tk):
    qi = pl.program_id(2)
    n_lat = SEQ // tq
    expo = (HEAD_DIM ** -0.5) * math.log2(math.e)
    lam = (jnp.exp(jnp.sum(lq1_ref[...] * lk1_ref[...], axis=-1, keepdims=True))
           - jnp.exp(jnp.sum(lq2_ref[...] * lk2_ref[...], axis=-1, keepdims=True)) + lam_init)

    @pl.when(qi == 0)
    def _():
        vt_ref[:, :CTX_LEN] = vc_ref[...].T
        for t in range(SEQ // tk):
            vt_ref[:, CTX_LEN + t * tk:CTX_LEN + (t + 1) * tk] = v_ref[t * tk:(t + 1) * tk, :].T

    def run(chunks):
        qs = [q_ref[:, c * HEAD_DIM:(c + 1) * HEAD_DIM] for c in range(2)]

        def scores(chunk):
            kref, lo, size, _ = chunk
            return [_dot_t(kref[lo:lo + size, c * HEAD_DIM:(c + 1) * HEAD_DIM], qs[c]) for c in range(2)]

        m = [None, None]
        l = [None, None]
        acc = [None, None]
        s_next = scores(chunks[0])
        for t, (_, _, size, vlo) in enumerate(chunks):
            s_cur = s_next
            if t + 1 < len(chunks):
                s_next = scores(chunks[t + 1])
            for c in range(2):
                st = s_cur[c]
                m_new = jnp.max(st, axis=0, keepdims=True)
                if m[c] is not None:
                    m_new = jnp.maximum(m[c], m_new)
                p = jnp.exp2((st - m_new) * expo)
                pv = jnp.dot(vt_ref[:, vlo:vlo + size], p.astype(BF16), preferred_element_type=F32)
                if m[c] is None:
                    l[c] = jnp.sum(p, axis=0, keepdims=True)
                    acc[c] = pv
                else:
                    alpha = jnp.exp2((m[c] - m_new) * expo)
                    l[c] = alpha * l[c] + jnp.sum(p, axis=0, keepdims=True)
                    acc[c] = alpha * acc[c] + pv
                m[c] = m_new
        o = acc[0] / l[0] - lam * (acc[1] / l[1])
        o = o * lax.rsqrt(jnp.mean(o * o, axis=0, keepdims=True) + NORM_EPS) * (sg_ref[...] * (1.0 - lam_init))
        o_ref[...] = o.T.astype(BF16)

    @pl.when(qi < n_lat)
    def _():
        run([(kc_ref, 0, CTX_LEN, 0)] + [(k_ref, t * tk, tk, CTX_LEN + t * tk) for t in range(SEQ // tk)])

    @pl.when(qi >= n_lat)
    def _():
        run([(kc_ref, 0, CTX_LEN, 0)])


def _diff_attn(qkv, lq1, lk1, lq2, lk2, subln_g_col, lam_init, tq=256, tk=1024):
    hw = 2 * HEAD_DIM
    n_lat = SEQ // tq
    n_ctx = CTX_LEN // tq
    kcol = D_MODEL // hw
    ctx_blk0 = LAT_ROWS // tq

    def qrow(b, qi):
        return jnp.where(qi < n_lat, b * n_lat + qi, ctx_blk0 + b * n_ctx + (qi - n_lat))

    vec = pl.BlockSpec((1, HEAD_DIM), lambda b, h, qi: (0, 0))
    kern = functools.partial(_diff_attn_kernel, lam_init=lam_init, tq=tq, tk=tk)
    return pl.pallas_call(
        kern,
        grid=(BATCH, C_HEADS, n_lat + n_ctx),
        in_specs=[
            vec, vec, vec, vec,
            pl.BlockSpec((hw, 1), lambda b, h, qi: (0, 0)),
            pl.BlockSpec((tq, hw), lambda b, h, qi: (qrow(b, qi), h)),
            pl.BlockSpec((SEQ, hw), lambda b, h, qi: (b, kcol + h)),
            pl.BlockSpec((SEQ, hw), lambda b, h, qi: (b, 2 * kcol + h)),
            pl.BlockSpec((CTX_LEN, hw), lambda b, h, qi: (LAT_ROWS // CTX_LEN + b, kcol + h)),
            pl.BlockSpec((CTX_LEN, hw), lambda b, h, qi: (LAT_ROWS // CTX_LEN + b, 2 * kcol + h)),
        ],
        out_specs=pl.BlockSpec((tq, hw), lambda b, h, qi: (qrow(b, qi), h)),
        out_shape=jax.ShapeDtypeStruct((ROWS, D_MODEL), BF16),
        scratch_shapes=[pltpu.VMEM((hw, CTX_LEN + SEQ), BF16)],
        compiler_params=_params(dimension_semantics=("arbitrary", "arbitrary", "arbitrary")),
        name="diff_attn",
    )(lq1, lk1, lq2, lk2, subln_g_col, qkv, qkv, qkv, qkv, qkv)


def _rope_tables():
    rows = SEQ // GRID_W
    row = jnp.repeat(jnp.arange(rows, dtype=F32), GRID_W)
    col = jnp.tile(jnp.arange(GRID_W, dtype=F32), rows)
    axis_dim = HEAD_DIM // 2
    inv = ROPE_THETA ** (-jnp.arange(0, axis_dim, 2, dtype=F32) / axis_dim)
    ang = jnp.concatenate([row[:, None] * inv, col[:, None] * inv], axis=-1)
    cos = jnp.repeat(jnp.cos(ang), 2, axis=-1)
    sin = jnp.repeat(jnp.sin(ang), 2, axis=-1) * jnp.tile(jnp.array([-1.0, 1.0], F32), HEAD_DIM // 2)
    heads_per_chunk = MXU_WIDTH // HEAD_DIM
    cos_t = jnp.concatenate([jnp.tile(cos, (BATCH, heads_per_chunk)), jnp.ones((CTX_ROWS, MXU_WIDTH), F32)], axis=0)
    sin_t = jnp.concatenate([jnp.tile(sin, (BATCH, heads_per_chunk)), jnp.zeros((CTX_ROWS, MXU_WIDTH), F32)], axis=0)
    return cos_t, sin_t


def _param_tiles(mods, norm_g):
    m = mods[:, :N_GROUPS].reshape(DEPTH, N_GROUPS, N_MOD, D_MODEL)
    zero = jnp.zeros((N_GROUPS, D_MODEL), F32)
    bcast = lambda v: jnp.broadcast_to(v, (N_GROUPS, D_MODEL))
    tiles = []
    for s in range(-1, 3 * DEPTH):
        rows = [zero] * 8
        if s >= 0:
            i, slot = divmod(s, 3)
            rows[P_POST_G] = bcast(norm_g[i, 2 * slot + 1])
            rows[P_GATE] = m[i, :, 3 * slot + 2]
        if s + 1 < 3 * DEPTH:
            i, slot = divmod(s + 1, 3)
            rows[P_PRE_G] = bcast(norm_g[i, 2 * slot])
            rows[P_SCALE] = m[i, :, 3 * slot + 1]
            rows[P_SHIFT] = m[i, :, 3 * slot]
        tiles.append(jnp.stack(rows, axis=1))
    return jnp.stack(tiles, axis=0)


def kernel(x, c, ctx, c_ctx, ada_w, ada_b, norm_g, ffn_w_in, ffn_w_out, a_w_in, a_w_out, a_sink, b_w_in, b_vnorm_g,
           b_vnorm_b, b_ws, b_bs, b_w_out, c_w_in, c_w_out, c_lq1, c_lk1, c_lq2, c_lk2, c_subln_g):
    xs = (x.reshape(LAT_ROWS, D_MODEL), ctx.reshape(CTX_ROWS, D_MODEL))
    cvec = jnp.concatenate([c, c_ctx[None, :], jnp.zeros((8 - BATCH - 1, D_MODEL), F32)], axis=0)
    pp = _param_tiles(_ada_mods(cvec, ada_w, ada_b), norm_g)
    cos_t, sin_t = _rope_tables()

    mixer_w = ((a_w_in, a_w_out), (b_w_in, b_w_out), (c_w_in, c_w_out))
    wgu = _convert(_ffn_split_job(ffn_w_in, (0, 0)))
    h = _prenorm(xs, pp, 0)
    for i in range(DEPTH):
        kind, j = i % N_MIXERS, i // N_MIXERS
        ctx_live = i < DEPTH - 1
        rows_out = ROWS if ctx_live else LAT_ROWS
        s0 = 3 * i + 1
        w_mix_in, w_mix_out = mixer_w[kind]

        ffn2_split = _ffn_split_job(ffn_w_in, (i, 1))
        jobs = [_cast_job(ffn_w_out, (i, 0), D_FF_PAD), _cast_job(w_mix_in, (j,))] + ([ffn2_split] if kind != 1 else [])
        a, conv = _ffn_in(h, wgu, jobs)
        xs, h = _mm_out(a, conv[0], xs, pp, s0, rows=ROWS, coef=0.5, want_h=True)
        w_i = conv[1]

        jobs = [_cast_job(w_mix_out, (j,))] + ([ffn2_split] if kind == 1 else [])
        if kind == 0:
            qkv, mix_conv = _qkv_rope(h, w_i, cos_t, sin_t, (A_HEADS + A_KV_HEADS) * HEAD_DIM, jobs)
            o = _win_attn(qkv, a_sink[j], ctx_live)
        elif kind == 1:
            z, mix_conv = _gmlp_in(h, w_i, jobs)
            o = _gmlp_mid(z, b_vnorm_g[j][None, :], b_vnorm_b[j][None, :], b_ws[j], b_bs[j].T)
        else:
            lam_init = 0.8 - 0.6 * math.exp(-0.3 * i)
            qkv, mix_conv = _qkv_rope(h, w_i, cos_t, sin_t, 2 * D_MODEL, jobs)
            o = _diff_attn(qkv, c_lq1[j][None, :], c_lk1[j][None, :], c_lq2[j][None, :], c_lk2[j][None, :],
                           c_subln_g[j][:, None], lam_init)
        w_o = mix_conv[0]
        wgu = mix_conv[1] if kind == 1 else conv[2]
        xs, h = _mm_out(o, w_o, xs, pp, s0 + 1, rows=rows_out, coef=1.0, want_h=True)

        jobs = [_cast_job(ffn_w_out, (i, 1), D_FF_PAD)]
        if i + 1 < DEPTH:
            jobs.append(_ffn_split_job(ffn_w_in, (i + 1, 0)))
        a, conv = _ffn_in(h, wgu, jobs)
        xs, h = _mm_out(a, conv[0], xs, pp, s0 + 2, rows=rows_out, coef=0.5, want_h=i + 1 < DEPTH)
        wgu = conv[1] if i + 1 < DEPTH else None
    return xs.reshape(BATCH, SEQ, D_MODEL)
```

```python
import functools
import math
from typing import Callable, NamedTuple

import jax
import jax.numpy as jnp
from jax import lax
from jax.experimental import pallas as pl
from jax.experimental.pallas import tpu as pltpu

D_MODEL = 2048
BATCH = 2
SEQ = 4096
DEPTH = 4
GRID_W = 64
CTX_LEN = 256
N_MIXERS = 3
N_MOD = 9
NORM_EPS = 1e-6
ROPE_THETA = 10000.0
NEG_INF = -1e30
D_FF = 5504
HEAD_DIM = 128
A_HEADS = 16
A_KV_HEADS = 4
A_GROUP = 4
A_BLOCK = 128
B_CHUNK = 128
B_WIDTH = 3 * D_MODEL
B_GROUPS = 8
B_GROUP_W = B_WIDTH // B_GROUPS
C_HEADS = 8

LAT_ROWS = BATCH * SEQ
CTX_ROWS = BATCH * CTX_LEN
ROWS = LAT_ROWS + CTX_ROWS
N_GROUPS = 3
LANES = 128
BF16_SUBLANES = 16
MXU_WIDTH = 256
IN_ROW_TILES = 8
FF_TILE = 512
D_FF_PAD = -(-D_FF // FF_TILE) * FF_TILE
VMEM_LIMIT = 56 * 1024 * 1024
VMEM_LIMIT_MM_OUT = 62 * 1024 * 1024

BF16 = jnp.bfloat16
F32 = jnp.float32

P_POST_G, P_GATE, P_PRE_G, P_SCALE, P_SHIFT = 0, 1, 2, 3, 4


def _params(vmem_limit=VMEM_LIMIT, **kw):
    return pltpu.CompilerParams(vmem_limit_bytes=vmem_limit, **kw)


def _group_of_block(i, tm):
    return jnp.minimum(i // (SEQ // tm), N_GROUPS - 1)


def _rms(x):
    return x * lax.rsqrt(jnp.mean(x * x, axis=-1, keepdims=True) + NORM_EPS)


def _ada_kernel(c_ref, w_ref, b_ref, o_ref):
    @pl.when(pl.program_id(1) == 0)
    def _():
        o_ref[...] = jnp.broadcast_to(b_ref[...], o_ref.shape)

    c = c_ref[...]
    a = (c * jax.nn.sigmoid(c)).astype(BF16)
    o_ref[...] += jnp.dot(a, w_ref[...].astype(BF16), preferred_element_type=F32)


def _ada_mods(cvec, ada_w, ada_b):
    kb = LANES
    n = N_MOD * D_MODEL
    return pl.pallas_call(
        _ada_kernel,
        grid=(DEPTH, D_MODEL // kb),
        in_specs=[
            pl.BlockSpec((8, kb), lambda l, k: (0, k)),
            pl.BlockSpec((None, kb, n), lambda l, k: (l, k, 0)),
            pl.BlockSpec((None, 1, n), lambda l, k: (l, 0, 0)),
        ],
        out_specs=pl.BlockSpec((None, 8, n), lambda l, k: (l, 0, 0)),
        out_shape=jax.ShapeDtypeStruct((DEPTH, 8, n), F32),
        compiler_params=_params(dimension_semantics=("arbitrary", "arbitrary")),
        name="ada_mods",
    )(cvec, ada_w, ada_b.reshape(DEPTH, 1, n))


def _pre(x, p_ref):
    gain = p_ref[P_PRE_G:P_PRE_G + 1, :] * (1.0 + p_ref[P_SCALE:P_SCALE + 1, :])
    return _rms(x) * gain + p_ref[P_SHIFT:P_SHIFT + 1, :]


def _stream_specs(x, tm):
    if not isinstance(x, tuple):
        return (x,), [pl.BlockSpec((tm, D_MODEL), lambda i: (i, 0))]
    n_lat = LAT_ROWS // tm
    return x, [pl.BlockSpec((tm, D_MODEL), lambda i: (jnp.minimum(i, n_lat - 1), 0)),
               pl.BlockSpec((tm, D_MODEL), lambda i: (jnp.maximum(i - n_lat, 0), 0))]


def _stream_tile(x_refs, rs=slice(None)):
    if len(x_refs) == 1:
        return x_refs[0][rs, :]
    n_lat = LAT_ROWS // x_refs[0].shape[0]
    return jnp.where(pl.program_id(0) < n_lat, x_refs[0][rs, :], x_refs[1][rs, :])


def _prenorm_kernel(*refs):
    *x_refs, p_ref, h_ref = refs
    h_ref[...] = _pre(_stream_tile(x_refs), p_ref).astype(BF16)


def _prenorm(x, pp, s, tm=512):
    x_arrays, x_specs = _stream_specs(x, tm)
    return pl.pallas_call(
        _prenorm_kernel,
        grid=(ROWS // tm,),
        in_specs=x_specs + [
            pl.BlockSpec((None, None, 8, D_MODEL), lambda i: (s, _group_of_block(i, tm), 0, 0)),
        ],
        out_specs=pl.BlockSpec((tm, D_MODEL), lambda i: (i, 0)),
        out_shape=jax.ShapeDtypeStruct((ROWS, D_MODEL), BF16),
        compiler_params=_params(dimension_semantics=("arbitrary",)),
        name="prenorm",
    )(*x_arrays, pp)


class _Job(NamedTuple):
    src: jax.Array
    in_spec: pl.BlockSpec
    out_spec: pl.BlockSpec
    out_shape: jax.ShapeDtypeStruct
    body: Callable
    n_blocks: int


def _row_block(rows, rows_out, steps):
    g = math.gcd(rows, rows_out)
    for rb in range(BF16_SUBLANES, g + 1, BF16_SUBLANES):
        if g % rb == 0 and rows_out // rb <= steps:
            return rb
    raise ValueError(f"no row block for {rows}->{rows_out} rows in {steps} steps")


def _cast_job(src, lead, rows_out=None):
    rows, cols = src.shape[-2:]
    rows_out = rows_out or rows

    def make(steps, lin):
        rb = _row_block(rows, rows_out, steps)
        nb_in, nb_out = rows // rb, rows_out // rb

        def body(src_ref, dst_ref, blk):
            v = src_ref[...].astype(BF16)
            if nb_out > nb_in:
                v = jnp.where(blk < nb_in, v, jnp.zeros_like(v))
            dst_ref[...] = v

        return _Job(
            src,
            pl.BlockSpec((None,) * len(lead) + (rb, cols),
                         lambda *g: lead + (jnp.minimum(lin(*g), nb_in - 1), 0)),
            pl.BlockSpec((rb, cols), lambda *g: (jnp.minimum(lin(*g), nb_out - 1), 0)),
            jax.ShapeDtypeStruct((rows_out, cols), BF16), body, nb_out)

    return make


def _ffn_split_job(ffn_w_in, lead):
    def make(steps, lin):
        rb = _row_block(D_MODEL, D_MODEL, steps)
        nb = D_MODEL // rb

        def body(src_ref, dst_ref, blk):
            del blk
            for part in range(2):
                dst_ref[part, :, :D_FF] = src_ref[:, part * D_FF:(part + 1) * D_FF].astype(BF16)
                dst_ref[part, :, D_FF:] = jnp.zeros((rb, D_FF_PAD - D_FF), BF16)

        return _Job(
            ffn_w_in,
            pl.BlockSpec((None,) * len(lead) + (rb, 2 * D_FF), lambda *g: lead + (jnp.minimum(lin(*g), nb - 1), 0)),
            pl.BlockSpec((2, rb, D_FF_PAD), lambda *g: (0, jnp.minimum(lin(*g), nb - 1), 0)),
            jax.ShapeDtypeStruct((2, D_MODEL, D_FF_PAD), BF16), body, nb)

    return make


def _hosted_call(main, main_args, main_in_specs, out_spec, out_shape, grid, job_makers, name):
    steps = math.prod(grid)
    strides = [math.prod(grid[d + 1:]) for d in range(len(grid))]
    lin = lambda *g: sum(gi * st for gi, st in zip(g, strides))
    jobs = [mk(steps, lin) for mk in job_makers]
    n_in, n_jobs = len(main_args), len(jobs)

    def kern(*refs):
        if main is not None:
            main(*refs[:n_in], refs[n_in + n_jobs])
        t = lin(*[pl.program_id(d) for d in range(len(grid))])
        n_main_out = 0 if main is None else 1
        for q, jb in enumerate(jobs):
            jb.body(refs[n_in + q], refs[n_in + n_jobs + n_main_out + q], jnp.minimum(t, jb.n_blocks - 1))

    main_out = [] if main is None else [(out_spec, out_shape)]
    res = pl.pallas_call(
        kern,
        grid=grid,
        in_specs=list(main_in_specs) + [jb.in_spec for jb in jobs],
        out_specs=[s for s, _ in main_out] + [jb.out_spec for jb in jobs],
        out_shape=[s for _, s in main_out] + [jb.out_shape for jb in jobs],
        compiler_params=_params(dimension_semantics=("arbitrary",) * len(grid)),
        name=name,
    )(*main_args, *[jb.src for jb in jobs])
    return (None, list(res)) if main is None else (res[0], list(res[1:]))


def _convert(job_maker, steps=32):
    return _hosted_call(None, (), (), None, None, (steps,), [job_maker], "convert")[1][0]


def _swiglu_kernel(h_ref, wg_ref, wu_ref, o_ref):
    h = h_ref[...]
    half = o_ref.shape[1] // 2
    parts = []
    for cs in (slice(0, half), slice(half, 2 * half)):
        parts.append((cs, jnp.dot(h, wg_ref[:, cs], preferred_element_type=F32),
                      jnp.dot(h, wu_ref[:, cs], preferred_element_type=F32)))
    for cs, g, u in parts:
        o_ref[:, cs] = (g * jax.nn.sigmoid(g) * u).astype(BF16)


def _ffn_in(h, wgu, jobs, tn=FF_TILE):
    rows = h.shape[0]
    tm = rows // IN_ROW_TILES
    return _hosted_call(
        _swiglu_kernel, (h, wgu, wgu),
        [pl.BlockSpec((tm, D_MODEL), lambda j, i: (i, 0)),
         pl.BlockSpec((None, D_MODEL, tn), lambda j, i: (0, 0, j)),
         pl.BlockSpec((None, D_MODEL, tn), lambda j, i: (1, 0, j))],
        pl.BlockSpec((tm, tn), lambda j, i: (i, j)),
        jax.ShapeDtypeStruct((rows, D_FF_PAD), BF16),
        (D_FF_PAD // tn, rows // tm), jobs, "ffn_in")


def _gelu_tanh(y):
    c0 = math.sqrt(2.0 / math.pi)
    half_y = 0.5 * y
    return half_y + half_y * jnp.tanh(y * (c0 + (c0 * 0.044715) * (y * y)))


def _gelu_kernel(h_ref, w_ref, o_ref):
    half = h_ref.shape[0] // 2
    halves = (slice(0, half), slice(half, 2 * half))
    ys = [jnp.dot(h_ref[rs, :], w_ref[...], preferred_element_type=F32) for rs in halves]
    for rs, y in zip(halves, ys):
        o_ref[rs, :] = _gelu_tanh(y).astype(BF16)


def _gmlp_in(h, w, jobs, tn=1024):
    rows = h.shape[0]
    tm = rows // IN_ROW_TILES
    n = w.shape[1]
    return _hosted_call(
        _gelu_kernel, (h, w),
        [pl.BlockSpec((tm, D_MODEL), lambda j, i: (i, 0)),
         pl.BlockSpec((D_MODEL, tn), lambda j, i: (0, j))],
        pl.BlockSpec((tm, tn), lambda j, i: (i, j)),
        jax.ShapeDtypeStruct((rows, n), BF16),
        (n // tn, rows // tm), jobs, "gmlp_in")


def _qkv_rope_kernel(h_ref, w_ref, cos_ref, sin_ref, swap_ref, o_ref, *, n_rope_tiles, tn):
    j = pl.program_id(1)

    @pl.when(j < n_rope_tiles)
    def _():
        h = h_ref[...]
        c = cos_ref[...]
        s = sin_ref[...]
        swap = swap_ref[...]
        y = jnp.dot(h, w_ref[...], preferred_element_type=F32)
        hi = y.astype(BF16)
        lo = (y - hi.astype(F32)).astype(BF16)
        for q0 in range(0, tn, MXU_WIDTH):
            cs = slice(q0, q0 + MXU_WIDTH)
            partner = (jnp.dot(hi[:, cs], swap, preferred_element_type=F32)
                       + jnp.dot(lo[:, cs], swap, preferred_element_type=F32))
            o_ref[:, cs] = (y[:, cs] * c + partner * s).astype(BF16)

    @pl.when(j >= n_rope_tiles)
    def _():
        o_ref[...] = jnp.dot(h_ref[...], w_ref[...], preferred_element_type=F32).astype(BF16)


def _qkv_rope(h, w, cos_t, sin_t, rope_width, jobs, tn=512):
    rows = h.shape[0]
    tm = rows // IN_ROW_TILES
    n = w.shape[1]
    kern = functools.partial(_qkv_rope_kernel, n_rope_tiles=rope_width // tn, tn=tn)
    lane = jnp.arange(MXU_WIDTH)
    swap = (lane[:, None] == (lane[None, :] ^ 1)).astype(BF16)
    return _hosted_call(
        kern, (h, w, cos_t, sin_t, swap),
        [pl.BlockSpec((tm, D_MODEL), lambda i, j: (i, 0)),
         pl.BlockSpec((D_MODEL, tn), lambda i, j: (0, j)),
         pl.BlockSpec((tm, MXU_WIDTH), lambda i, j: (i, 0)),
         pl.BlockSpec((tm, MXU_WIDTH), lambda i, j: (i, 0)),
         pl.BlockSpec((MXU_WIDTH, MXU_WIDTH), lambda i, j: (0, 0))],
        pl.BlockSpec((tm, tn), lambda i, j: (i, j)),
        jax.ShapeDtypeStruct((rows, n), BF16),
        (rows // tm, n // tn), jobs, "qkv_rope")


def _mm_out_kernel(a_ref, w_ref, p_ref, *refs, coef, n_x, want_h):
    x_refs, xo_ref, maybe_h_ref = refs[:n_x], refs[n_x], refs[n_x + 1:]
    post = coef * (p_ref[P_GATE:P_GATE + 1, :] * p_ref[P_POST_G:P_POST_G + 1, :])
    half = a_ref.shape[0] // 2
    for rs in (slice(0, half), slice(half, 2 * half)):
        y = jnp.dot(a_ref[rs, :], w_ref[...], preferred_element_type=F32)
        xn = _stream_tile(x_refs, rs) + _rms(y) * post
        xo_ref[rs, :] = xn
        if want_h:
            maybe_h_ref[0][rs, :] = _pre(xn, p_ref).astype(BF16)


def _mm_out(a, w, x, pp, s, *, rows, coef, want_h):
    k = a.shape[1]

    n_x = len(x) if isinstance(x, tuple) else 1

    def vmem_bytes(tm):
        return k * D_MODEL * 2 + 2 * tm * (k * 2 + D_MODEL * (4 * n_x + 4 + 2)) + tm * D_MODEL * 4

    tm = 512 if vmem_bytes(512) <= VMEM_LIMIT_MM_OUT - (2 << 20) else 256
    x_arrays, x_specs = _stream_specs(x, tm)
    out_shape = [jax.ShapeDtypeStruct((rows, D_MODEL), F32)]
    out_specs = [pl.BlockSpec((tm, D_MODEL), lambda i: (i, 0))]
    if want_h:
        out_shape.append(jax.ShapeDtypeStruct((rows, D_MODEL), BF16))
        out_specs.append(pl.BlockSpec((tm, D_MODEL), lambda i: (i, 0)))
    res = pl.pallas_call(
        functools.partial(_mm_out_kernel, coef=coef, n_x=len(x_arrays), want_h=want_h),
        grid=(rows // tm,),
        in_specs=[
            pl.BlockSpec((tm, k), lambda i: (i, 0)),
            pl.BlockSpec((k, D_MODEL), lambda i: (0, 0), pipeline_mode=pl.Buffered(1)),
            pl.BlockSpec((None, None, 8, D_MODEL), lambda i: (s, _group_of_block(i, tm), 0, 0)),
        ] + x_specs,
        out_specs=out_specs,
        out_shape=out_shape,
        compiler_params=_params(vmem_limit=VMEM_LIMIT_MM_OUT, dimension_semantics=("arbitrary",)),
        name="mm_out",
    )(a, w, pp, *x_arrays)
    return (res[0], res[1]) if want_h else (res[0], None)


def _dot_t(a, b):
    return lax.dot_general(a, b, (((1,), (1,)), ((), ())), preferred_element_type=F32)


def _dot_tn(a, b):
    return lax.dot_general(a, b, (((0,), (0,)), ((), ())), preferred_element_type=F32)


def _win_attn_kernel(sink_ref, q_ref, kv_ref, kvc_ref, o_ref):
    n = pl.program_id(1)
    n_lat = SEQ // A_BLOCK
    kvw = A_KV_HEADS * HEAD_DIM
    scale = HEAD_DIM ** -0.5
    expo = scale * math.log2(math.e)
    gq = A_GROUP * A_BLOCK
    lane = lax.broadcasted_iota(jnp.int32, (1, gq), 1)
    qi = lax.broadcasted_iota(jnp.int32, (A_BLOCK, gq), 1) & (A_BLOCK - 1)
    mi = lax.broadcasted_iota(jnp.int32, (A_BLOCK, gq), 0)

    def heads(kk):
        q4 = jnp.concatenate(
            [q_ref[:, (kk * A_GROUP + g) * HEAD_DIM:(kk * A_GROUP + g + 1) * HEAD_DIM] for g in range(A_GROUP)],
            axis=0)
        sink = jnp.full((1, gq), sink_ref[kk * A_GROUP] / scale, F32)
        for g in range(1, A_GROUP):
            sink = jnp.where(lane >= g * A_BLOCK, sink_ref[kk * A_GROUP + g] / scale, sink)
        return q4, sink, slice(kk * HEAD_DIM, (kk + 1) * HEAD_DIM), slice(kvw + kk * HEAD_DIM, kvw + (kk + 1) * HEAD_DIM)

    def finish(kk, pieces, sink):
        m = sink
        for s, _ in pieces:
            m = jnp.maximum(m, jnp.max(s, axis=0, keepdims=True))
        den = jnp.exp2((sink - m) * expo)
        ot = None
        for s, v in pieces:
            p = jnp.exp2((s - m) * expo)
            den = den + jnp.sum(p, axis=0, keepdims=True)
            pv = _dot_tn(v, p.astype(BF16))
            ot = pv if ot is None else ot + pv
        ot = ot / den
        for g in range(A_GROUP):
            h = kk * A_GROUP + g
            o_ref[:, h * HEAD_DIM:(h + 1) * HEAD_DIM] = ot[:, g * A_BLOCK:(g + 1) * A_BLOCK].T.astype(BF16)

    @pl.when(n < n_lat)
    def _():
        off_l = jnp.where(n > 0, 0, 2 * A_BLOCK)
        off_r = jnp.where(n < n_lat - 1, 0, 2 * A_BLOCK)
        ok_l = mi >= qi + off_l
        ok_r = mi <= qi - off_r
        rows_l, rows_m, rows_r = [pl.ds(pl.multiple_of(jnp.clip(n + d, 0, n_lat - 1) * A_BLOCK, A_BLOCK), A_BLOCK)
                                  for d in (-1, 0, 1)]
        def scores(kk):
            q4, sink, kc, vc = heads(kk)
            s_l = jnp.where(ok_l, _dot_t(kv_ref[rows_l, kc], q4), NEG_INF)
            s_r = jnp.where(ok_r, _dot_t(kv_ref[rows_r, kc], q4), NEG_INF)
            return ([(_dot_t(kvc_ref[:, kc], q4), kvc_ref[:, vc]), (s_l, kv_ref[rows_l, vc]),
                     (_dot_t(kv_ref[rows_m, kc], q4), kv_ref[rows_m, vc]), (s_r, kv_ref[rows_r, vc])], sink)

        nxt = scores(0)
        for kk in range(A_KV_HEADS):
            cur = nxt
            if kk + 1 < A_KV_HEADS:
                nxt = scores(kk + 1)
            finish(kk, *cur)

    @pl.when(n >= n_lat)
    def _():
        for kk in range(A_KV_HEADS):
            q4, sink, kc, vc = heads(kk)
            finish(kk, [(_dot_t(kvc_ref[:, kc], q4), kvc_ref[:, vc])], sink)


def _win_attn(qkv, sink, with_ctx):
    n_lat = SEQ // A_BLOCK
    n_ctx = CTX_LEN // A_BLOCK
    steps = n_lat + (n_ctx if with_ctx else 0)
    qw = A_HEADS * HEAD_DIM
    kvw2 = 2 * A_KV_HEADS * HEAD_DIM
    ctx_blk0 = LAT_ROWS // A_BLOCK

    def qrow(b, n):
        return jnp.where(n < n_lat, b * n_lat + n, ctx_blk0 + b * n_ctx + (n - n_lat))

    rows = LAT_ROWS + (CTX_ROWS if with_ctx else 0)
    return pl.pallas_call(
        _win_attn_kernel,
        grid=(BATCH, steps),
        in_specs=[
            pl.BlockSpec(memory_space=pltpu.SMEM),
            pl.BlockSpec((A_BLOCK, qw), lambda b, n: (qrow(b, n), 0)),
            pl.BlockSpec((SEQ, kvw2), lambda b, n: (b, qw // kvw2)),
            pl.BlockSpec((CTX_LEN, kvw2), lambda b, n: (LAT_ROWS // CTX_LEN + b, qw // kvw2)),
        ],
        out_specs=pl.BlockSpec((A_BLOCK, D_MODEL), lambda b, n: (qrow(b, n), 0)),
        out_shape=jax.ShapeDtypeStruct((rows, D_MODEL), BF16),
        compiler_params=_params(dimension_semantics=("arbitrary", "arbitrary")),
        name="win_attn",
    )(sink, qkv, qkv, qkv)


def _gmlp_mid_kernel(u_ref, v_ref, g_ref, b_ref, ws_ref, bs_ref, o_ref, *, tm):
    v = v_ref[...].astype(F32)
    mu = jnp.mean(v, axis=-1, keepdims=True)
    vc = v - mu
    vn = vc * lax.rsqrt(jnp.mean(vc * vc, axis=-1, keepdims=True) + NORM_EPS)
    vn = (vn * g_ref[...] + b_ref[...]).astype(BF16)
    for g in range(B_GROUPS):
        cs = slice(g * B_GROUP_W, (g + 1) * B_GROUP_W)
        ws = ws_ref[g].astype(BF16)
        for c in range(tm // B_CHUNK):
            rs = slice(c * B_CHUNK, (c + 1) * B_CHUNK)
            mixed = jnp.dot(ws, vn[rs, cs], preferred_element_type=F32) + bs_ref[:, g:g + 1]
            o_ref[rs, cs] = (u_ref[rs, cs].astype(F32) * mixed).astype(BF16)


def _gmlp_mid(z, vn_g, vn_b, ws, bs_t, tm=256):
    rows = z.shape[0]
    return pl.pallas_call(
        functools.partial(_gmlp_mid_kernel, tm=tm),
        grid=(rows // tm,),
        in_specs=[
            pl.BlockSpec((tm, B_WIDTH), lambda i: (i, 0)),
            pl.BlockSpec((tm, B_WIDTH), lambda i: (i, 1)),
            pl.BlockSpec((1, B_WIDTH), lambda i: (0, 0)),
            pl.BlockSpec((1, B_WIDTH), lambda i: (0, 0)),
            pl.BlockSpec((B_GROUPS, B_CHUNK, B_CHUNK), lambda i: (0, 0, 0)),
            pl.BlockSpec((B_CHUNK, B_GROUPS), lambda i: (0, 0)),
        ],
        out_specs=pl.BlockSpec((tm, B_WIDTH), lambda i: (i, 0)),
        out_shape=jax.ShapeDtypeStruct((rows, B_WIDTH), BF16),
        compiler_params=_params(dimension_semantics=("arbitrary",)),
        name="gmlp_mid",
    )(z, z, vn_g, vn_b, ws, bs_t)


def _diff_attn_kernel(lq1_ref, lk1_ref, lq2_ref, lk2_ref, sg_ref, q_ref, k_ref, v_ref, kc_ref, vc_ref, o_ref, vt_ref,
                      *, lam_init, tq, tk):
    qi = pl.program_id(2)
    n_lat = SEQ // tq
    expo = (HEAD_DIM ** -0.5) * math.log2(math.e)
    lam = (jnp.exp(jnp.sum(lq1_ref[...] * lk1_ref[...], axis=-1, keepdims=True))
           - jnp.exp(jnp.sum(lq2_ref[...] * lk2_ref[...], axis=-1, keepdims=True)) + lam_init)

    @pl.when(qi == 0)
    def _():
        vt_ref[:, :CTX_LEN] = vc_ref[...].T
        for t in range(SEQ // tk):
            vt_ref[:, CTX_LEN + t * tk:CTX_LEN + (t + 1) * tk] = v_ref[t * tk:(t + 1) * tk, :].T

    def run(chunks):
        qs = [q_ref[:, c * HEAD_DIM:(c + 1) * HEAD_DIM] for c in range(2)]

        def scores(chunk):
            kref, lo, size, _ = chunk
            return [_dot_t(kref[lo:lo + size, c * HEAD_DIM:(c + 1) * HEAD_DIM], qs[c]) for c in range(2)]

        m = [None, None]
        l = [None, None]
        acc = [None, None]
        s_next = scores(chunks[0])
        for t, (_, _, size, vlo) in enumerate(chunks):
            s_cur = s_next
            if t + 1 < len(chunks):
                s_next = scores(chunks[t + 1])
            for c in range(2):
                st = s_cur[c]
                m_new = jnp.max(st, axis=0, keepdims=True)
                if m[c] is not None:
                    m_new = jnp.maximum(m[c], m_new)
                p = jnp.exp2((st - m_new) * expo)
                pv = jnp.dot(vt_ref[:, vlo:vlo + size], p.astype(BF16), preferred_element_type=F32)
                if m[c] is None:
                    l[c] = jnp.sum(p, axis=0, keepdims=True)
                    acc[c] = pv
                else:
                    alpha = jnp.exp2((m[c] - m_new) * expo)
                    l[c] = alpha * l[c] + jnp.sum(p, axis=0, keepdims=True)
                    acc[c] = alpha * acc[c] + pv
                m[c] = m_new
        o = acc[0] / l[0] - lam * (acc[1] / l[1])
        o = o * lax.rsqrt(jnp.mean(o * o, axis=0, keepdims=True) + NORM_EPS) * (sg_ref[...] * (1.0 - lam_init))
        o_ref[...] = o.T.astype(BF16)

    @pl.when(qi < n_lat)
    def _():
        run([(kc_ref, 0, CTX_LEN, 0)] + [(k_ref, t * tk, tk, CTX_LEN + t * tk) for t in range(SEQ // tk)])

    @pl.when(qi >= n_lat)
    def _():
        run([(kc_ref, 0, CTX_LEN, 0)])


def _diff_attn(qkv, lq1, lk1, lq2, lk2, subln_g_col, lam_init, tq=256, tk=1024):
    hw = 2 * HEAD_DIM
    n_lat = SEQ // tq
    n_ctx = CTX_LEN // tq
    kcol = D_MODEL // hw
    ctx_blk0 = LAT_ROWS // tq

    def qrow(b, qi):
        return jnp.where(qi < n_lat, b * n_lat + qi, ctx_blk0 + b * n_ctx + (qi - n_lat))

    vec = pl.BlockSpec((1, HEAD_DIM), lambda b, h, qi: (0, 0))
    kern = functools.partial(_diff_attn_kernel, lam_init=lam_init, tq=tq, tk=tk)
    return pl.pallas_call(
        kern,
        grid=(BATCH, C_HEADS, n_lat + n_ctx),
        in_specs=[
            vec, vec, vec, vec,
            pl.BlockSpec((hw, 1), lambda b, h, qi: (0, 0)),
            pl.BlockSpec((tq, hw), lambda b, h, qi: (qrow(b, qi), h)),
            pl.BlockSpec((SEQ, hw), lambda b, h, qi: (b, kcol + h)),
            pl.BlockSpec((SEQ, hw), lambda b, h, qi: (b, 2 * kcol + h)),
            pl.BlockSpec((CTX_LEN, hw), lambda b, h, qi: (LAT_ROWS // CTX_LEN + b, kcol + h)),
            pl.BlockSpec((CTX_LEN, hw), lambda b, h, qi: (LAT_ROWS // CTX_LEN + b, 2 * kcol + h)),
        ],
        out_specs=pl.BlockSpec((tq, hw), lambda b, h, qi: (qrow(b, qi), h)),
        out_shape=jax.ShapeDtypeStruct((ROWS, D_MODEL), BF16),
        scratch_shapes=[pltpu.VMEM((hw, CTX_LEN + SEQ), BF16)],
        compiler_params=_params(dimension_semantics=("arbitrary", "arbitrary", "arbitrary")),
        name="diff_attn",
    )(lq1, lk1, lq2, lk2, subln_g_col, qkv, qkv, qkv, qkv, qkv)


def _rope_tables():
    rows = SEQ // GRID_W
    row = jnp.repeat(jnp.arange(rows, dtype=F32), GRID_W)
    col = jnp.tile(jnp.arange(GRID_W, dtype=F32), rows)
    axis_dim = HEAD_DIM // 2
    inv = ROPE_THETA ** (-jnp.arange(0, axis_dim, 2, dtype=F32) / axis_dim)
    ang = jnp.concatenate([row[:, None] * inv, col[:, None] * inv], axis=-1)
    cos = jnp.repeat(jnp.cos(ang), 2, axis=-1)
    sin = jnp.repeat(jnp.sin(ang), 2, axis=-1) * jnp.tile(jnp.array([-1.0, 1.0], F32), HEAD_DIM // 2)
    heads_per_chunk = MXU_WIDTH // HEAD_DIM
    cos_t = jnp.concatenate([jnp.tile(cos, (BATCH, heads_per_chunk)), jnp.ones((CTX_ROWS, MXU_WIDTH), F32)], axis=0)
    sin_t = jnp.concatenate([jnp.tile(sin, (BATCH, heads_per_chunk)), jnp.zeros((CTX_ROWS, MXU_WIDTH), F32)], axis=0)
    return cos_t, sin_t


def _param_tiles(mods, norm_g):
    m = mods[:, :N_GROUPS].reshape(DEPTH, N_GROUPS, N_MOD, D_MODEL)
    zero = jnp.zeros((N_GROUPS, D_MODEL), F32)
    bcast = lambda v: jnp.broadcast_to(v, (N_GROUPS, D_MODEL))
    tiles = []
    for s in range(-1, 3 * DEPTH):
        rows = [zero] * 8
        if s >= 0:
            i, slot = divmod(s, 3)
            rows[P_POST_G] = bcast(norm_g[i, 2 * slot + 1])
            rows[P_GATE] = m[i, :, 3 * slot + 2]
        if s + 1 < 3 * DEPTH:
            i, slot = divmod(s + 1, 3)
            rows[P_PRE_G] = bcast(norm_g[i, 2 * slot])
            rows[P_SCALE] = m[i, :, 3 * slot + 1]
            rows[P_SHIFT] = m[i, :, 3 * slot]
        tiles.append(jnp.stack(rows, axis=1))
    return jnp.stack(tiles, axis=0)


def kernel(x, c, ctx, c_ctx, ada_w, ada_b, norm_g, ffn_w_in, ffn_w_out, a_w_in, a_w_out, a_sink, b_w_in, b_vnorm_g,
           b_vnorm_b, b_ws, b_bs, b_w_out, c_w_in, c_w_out, c_lq1, c_lk1, c_lq2, c_lk2, c_subln_g):
    xs = (x.reshape(LAT_ROWS, D_MODEL), ctx.reshape(CTX_ROWS, D_MODEL))
    cvec = jnp.concatenate([c, c_ctx[None, :], jnp.zeros((8 - BATCH - 1, D_MODEL), F32)], axis=0)
    pp = _param_tiles(_ada_mods(cvec, ada_w, ada_b), norm_g)
    cos_t, sin_t = _rope_tables()

    mixer_w = ((a_w_in, a_w_out), (b_w_in, b_w_out), (c_w_in, c_w_out))
    wgu = _convert(_ffn_split_job(ffn_w_in, (0, 0)))
    h = _prenorm(xs, pp, 0)
    for i in range(DEPTH):
        kind, j = i % N_MIXERS, i // N_MIXERS
        ctx_live = i < DEPTH - 1
        rows_out = ROWS if ctx_live else LAT_ROWS
        s0 = 3 * i + 1
        w_mix_in, w_mix_out = mixer_w[kind]

        ffn2_split = _ffn_split_job(ffn_w_in, (i, 1))
        jobs = [_cast_job(ffn_w_out, (i, 0), D_FF_PAD), _cast_job(w_mix_in, (j,))] + ([ffn2_split] if kind != 1 else [])
        a, conv = _ffn_in(h, wgu, jobs)
        xs, h = _mm_out(a, conv[0], xs, pp, s0, rows=ROWS, coef=0.5, want_h=True)
        w_i = conv[1]

        jobs = [_cast_job(w_mix_out, (j,))] + ([ffn2_split] if kind == 1 else [])
        if kind == 0:
            qkv, mix_conv = _qkv_rope(h, w_i, cos_t, sin_t, (A_HEADS + A_KV_HEADS) * HEAD_DIM, jobs)
            o = _win_attn(qkv, a_sink[j], ctx_live)
        elif kind == 1:
            z, mix_conv = _gmlp_in(h, w_i, jobs)
            o = _gmlp_mid(z, b_vnorm_g[j][None, :], b_vnorm_b[j][None, :], b_ws[j], b_bs[j].T)
        else:
            lam_init = 0.8 - 0.6 * math.exp(-0.3 * i)
            qkv, mix_conv = _qkv_rope(h, w_i, cos_t, sin_t, 2 * D_MODEL, jobs)
            o = _diff_attn(qkv, c_lq1[j][None, :], c_lk1[j][None, :], c_lq2[j][None, :], c_lk2[j][None, :],
                           c_subln_g[j][:, None], lam_init)
        w_o = mix_conv[0]
        wgu = mix_conv[1] if kind == 1 else conv[2]
        xs, h = _mm_out(o, w_o, xs, pp, s0 + 1, rows=rows_out, coef=1.0, want_h=True)

        jobs = [_cast_job(ffn_w_out, (i, 1), D_FF_PAD)]
        if i + 1 < DEPTH:
            jobs.append(_ffn_split_job(ffn_w_in, (i + 1, 0)))
        a, conv = _ffn_in(h, wgu, jobs)
        xs, h = _mm_out(a, conv[0], xs, pp, s0 + 2, rows=rows_out, coef=0.5, want_h=i + 1 < DEPTH)
        wgu = conv[1] if i + 1 < DEPTH else None
    return xs.reshape(BATCH, SEQ, D_MODEL)
```

```python
import functools
import math
from typing import Callable, NamedTuple

import jax
import jax.numpy as jnp
from jax import lax
from jax.experimental import pallas as pl
from jax.experimental.pallas import tpu as pltpu

D_MODEL = 2048
BATCH = 2
SEQ = 4096
DEPTH = 4
GRID_W = 64
CTX_LEN = 256
N_MIXERS = 3
N_MOD = 9
NORM_EPS = 1e-6
ROPE_THETA = 10000.0
NEG_INF = -1e30
D_FF = 5504
HEAD_DIM = 128
A_HEADS = 16
A_KV_HEADS = 4
A_GROUP = 4
A_BLOCK = 128
B_CHUNK = 128
B_WIDTH = 3 * D_MODEL
B_GROUPS = 8
B_GROUP_W = B_WIDTH // B_GROUPS
C_HEADS = 8

LAT_ROWS = BATCH * SEQ
CTX_ROWS = BATCH * CTX_LEN
ROWS = LAT_ROWS + CTX_ROWS
N_GROUPS = 3
LANES = 128
BF16_SUBLANES = 16
MXU_WIDTH = 256
IN_ROW_TILES = 8
FF_TILE = 512
D_FF_PAD = -(-D_FF // FF_TILE) * FF_TILE
VMEM_LIMIT = 62 * 1024 * 1024

BF16 = jnp.bfloat16
F32 = jnp.float32

P_POST_G, P_GATE, P_PRE_G, P_SCALE, P_SHIFT = 0, 1, 2, 3, 4


def _params(**kw):
    return pltpu.CompilerParams(vmem_limit_bytes=VMEM_LIMIT, **kw)


def _group_of_block(i, tm):
    return jnp.minimum(i // (SEQ // tm), N_GROUPS - 1)


def _rms(x):
    return x * lax.rsqrt(jnp.mean(x * x, axis=-1, keepdims=True) + NORM_EPS)


def _ada_kernel(c_ref, w_ref, b_ref, o_ref):
    @pl.when(pl.program_id(1) == 0)
    def _():
        o_ref[...] = jnp.broadcast_to(b_ref[...], o_ref.shape)

    c = c_ref[...]
    a = (c * jax.nn.sigmoid(c)).astype(BF16)
    o_ref[...] += jnp.dot(a, w_ref[...].astype(BF16), preferred_element_type=F32)


def _ada_mods(cvec, ada_w, ada_b):
    kb = LANES
    n = N_MOD * D_MODEL
    return pl.pallas_call(
        _ada_kernel,
        grid=(DEPTH, D_MODEL // kb),
        in_specs=[
            pl.BlockSpec((8, kb), lambda l, k: (0, k)),
            pl.BlockSpec((None, kb, n), lambda l, k: (l, k, 0)),
            pl.BlockSpec((None, 1, n), lambda l, k: (l, 0, 0)),
        ],
        out_specs=pl.BlockSpec((None, 8, n), lambda l, k: (l, 0, 0)),
        out_shape=jax.ShapeDtypeStruct((DEPTH, 8, n), F32),
        compiler_params=_params(dimension_semantics=("arbitrary", "arbitrary")),
        name="ada_mods",
    )(cvec, ada_w, ada_b.reshape(DEPTH, 1, n))


def _pre(x, p_ref):
    gain = p_ref[P_PRE_G:P_PRE_G + 1, :] * (1.0 + p_ref[P_SCALE:P_SCALE + 1, :])
    return _rms(x) * gain + p_ref[P_SHIFT:P_SHIFT + 1, :]


def _stream_specs(x, tm):
    if not isinstance(x, tuple):
        return (x,), [pl.BlockSpec((tm, D_MODEL), lambda i: (i, 0))]
    n_lat = LAT_ROWS // tm
    return x, [pl.BlockSpec((tm, D_MODEL), lambda i: (jnp.minimum(i, n_lat - 1), 0)),
               pl.BlockSpec((tm, D_MODEL), lambda i: (jnp.maximum(i - n_lat, 0), 0))]


def _stream_tile(x_refs, rs=slice(None)):
    if len(x_refs) == 1:
        return x_refs[0][rs, :]
    n_lat = LAT_ROWS // x_refs[0].shape[0]
    return jnp.where(pl.program_id(0) < n_lat, x_refs[0][rs, :], x_refs[1][rs, :])


def _prenorm_kernel(*refs):
    *x_refs, p_ref, h_ref = refs
    h_ref[...] = _pre(_stream_tile(x_refs), p_ref).astype(BF16)


def _prenorm(x, pp, s, tm=512):
    x_arrays, x_specs = _stream_specs(x, tm)
    return pl.pallas_call(
        _prenorm_kernel,
        grid=(ROWS // tm,),
        in_specs=x_specs + [
            pl.BlockSpec((None, None, 8, D_MODEL), lambda i: (s, _group_of_block(i, tm), 0, 0)),
        ],
        out_specs=pl.BlockSpec((tm, D_MODEL), lambda i: (i, 0)),
        out_shape=jax.ShapeDtypeStruct((ROWS, D_MODEL), BF16),
        compiler_params=_params(dimension_semantics=("arbitrary",)),
        name="prenorm",
    )(*x_arrays, pp)


class _Job(NamedTuple):
    src: jax.Array
    in_spec: pl.BlockSpec
    out_spec: pl.BlockSpec
    out_shape: jax.ShapeDtypeStruct
    body: Callable
    n_blocks: int


def _row_block(rows, rows_out, steps):
    g = math.gcd(rows, rows_out)
    for rb in range(BF16_SUBLANES, g + 1, BF16_SUBLANES):
        if g % rb == 0 and rows_out // rb <= steps:
            return rb
    raise ValueError(f"no row block for {rows}->{rows_out} rows in {steps} steps")


def _cast_job(src, lead, rows_out=None):
    rows, cols = src.shape[-2:]
    rows_out = rows_out or rows

    def make(steps, lin):
        rb = _row_block(rows, rows_out, steps)
        nb_in, nb_out = rows // rb, rows_out // rb

        def body(src_ref, dst_ref, blk):
            v = src_ref[...].astype(BF16)
            if nb_out > nb_in:
                v = jnp.where(blk < nb_in, v, jnp.zeros_like(v))
            dst_ref[...] = v

        return _Job(
            src,
            pl.BlockSpec((None,) * len(lead) + (rb, cols),
                         lambda *g: lead + (jnp.minimum(lin(*g), nb_in - 1), 0)),
            pl.BlockSpec((rb, cols), lambda *g: (jnp.minimum(lin(*g), nb_out - 1), 0)),
            jax.ShapeDtypeStruct((rows_out, cols), BF16), body, nb_out)

    return make


def _ffn_split_job(ffn_w_in, lead):
    def make(steps, lin):
        rb = _row_block(D_MODEL, D_MODEL, steps)
        nb = D_MODEL // rb

        def body(src_ref, dst_ref, blk):
            del blk
            for part in range(2):
                dst_ref[part, :, :D_FF] = src_ref[:, part * D_FF:(part + 1) * D_FF].astype(BF16)
                dst_ref[part, :, D_FF:] = jnp.zeros((rb, D_FF_PAD - D_FF), BF16)

        return _Job(
            ffn_w_in,
            pl.BlockSpec((None,) * len(lead) + (rb, 2 * D_FF), lambda *g: lead + (jnp.minimum(lin(*g), nb - 1), 0)),
            pl.BlockSpec((2, rb, D_FF_PAD), lambda *g: (0, jnp.minimum(lin(*g), nb - 1), 0)),
            jax.ShapeDtypeStruct((2, D_MODEL, D_FF_PAD), BF16), body, nb)

    return make


def _hosted_call(main, main_args, main_in_specs, out_spec, out_shape, grid, job_makers, name):
    steps = math.prod(grid)
    strides = [math.prod(grid[d + 1:]) for d in range(len(grid))]
    lin = lambda *g: sum(gi * st for gi, st in zip(g, strides))
    jobs = [mk(steps, lin) for mk in job_makers]
    n_in, n_jobs = len(main_args), len(jobs)

    def kern(*refs):
        if main is not None:
            main(*refs[:n_in], refs[n_in + n_jobs])
        t = lin(*[pl.program_id(d) for d in range(len(grid))])
        n_main_out = 0 if main is None else 1
        for q, jb in enumerate(jobs):
            jb.body(refs[n_in + q], refs[n_in + n_jobs + n_main_out + q], jnp.minimum(t, jb.n_blocks - 1))

    main_out = [] if main is None else [(out_spec, out_shape)]
    res = pl.pallas_call(
        kern,
        grid=grid,
        in_specs=list(main_in_specs) + [jb.in_spec for jb in jobs],
        out_specs=[s for s, _ in main_out] + [jb.out_spec for jb in jobs],
        out_shape=[s for _, s in main_out] + [jb.out_shape for jb in jobs],
        compiler_params=_params(dimension_semantics=("arbitrary",) * len(grid)),
        name=name,
    )(*main_args, *[jb.src for jb in jobs])
    return (None, list(res)) if main is None else (res[0], list(res[1:]))


def _convert(job_maker, steps=32):
    return _hosted_call(None, (), (), None, None, (steps,), [job_maker], "convert")[1][0]


def _swiglu_kernel(h_ref, wg_ref, wu_ref, o_ref):
    h = h_ref[...]
    half = o_ref.shape[1] // 2
    parts = []
    for cs in (slice(0, half), slice(half, 2 * half)):
        parts.append((cs, jnp.dot(h, wg_ref[:, cs], preferred_element_type=F32),
                      jnp.dot(h, wu_ref[:, cs], preferred_element_type=F32)))
    for cs, g, u in parts:
        o_ref[:, cs] = (g * jax.nn.sigmoid(g) * u).astype(BF16)


def _ffn_in(h, wgu, jobs, tn=FF_TILE):
    rows = h.shape[0]
    tm = rows // IN_ROW_TILES
    return _hosted_call(
        _swiglu_kernel, (h, wgu, wgu),
        [pl.BlockSpec((tm, D_MODEL), lambda j, i: (i, 0)),
         pl.BlockSpec((None, D_MODEL, tn), lambda j, i: (0, 0, j)),
         pl.BlockSpec((None, D_MODEL, tn), lambda j, i: (1, 0, j))],
        pl.BlockSpec((tm, tn), lambda j, i: (i, j)),
        jax.ShapeDtypeStruct((rows, D_FF_PAD), BF16),
        (D_FF_PAD // tn, rows // tm), jobs, "ffn_in")


def _gelu_tanh(y):
    c0 = math.sqrt(2.0 / math.pi)
    half_y = 0.5 * y
    return half_y + half_y * jnp.tanh(y * (c0 + (c0 * 0.044715) * (y * y)))


def _gelu_kernel(h_ref, w_ref, o_ref):
    half = h_ref.shape[0] // 2
    halves = (slice(0, half), slice(half, 2 * half))
    ys = [jnp.dot(h_ref[rs, :], w_ref[...], preferred_element_type=F32) for rs in halves]
    for rs, y in zip(halves, ys):
        o_ref[rs, :] = _gelu_tanh(y).astype(BF16)


def _gmlp_in(h, w, jobs, tn=2048):
    rows = h.shape[0]
    tm = rows // IN_ROW_TILES
    n = w.shape[1]
    return _hosted_call(
        _gelu_kernel, (h, w),
        [pl.BlockSpec((tm, D_MODEL), lambda j, i: (i, 0)),
         pl.BlockSpec((D_MODEL, tn), lambda j, i: (0, j))],
        pl.BlockSpec((tm, tn), lambda j, i: (i, j)),
        jax.ShapeDtypeStruct((rows, n), BF16),
        (n // tn, rows // tm), jobs, "gmlp_in")


def _qkv_rope_kernel(h_ref, w_ref, cos_ref, sin_ref, swap_ref, o_ref, *, n_rope_tiles, tn):
    j = pl.program_id(1)

    @pl.when(j < n_rope_tiles)
    def _():
        h = h_ref[...]
        c = cos_ref[...]
        s = sin_ref[...]
        swap = swap_ref[...]
        y = jnp.dot(h, w_ref[...], preferred_element_type=F32)
        hi = y.astype(BF16)
        lo = (y - hi.astype(F32)).astype(BF16)
        for q0 in range(0, tn, MXU_WIDTH):
            cs = slice(q0, q0 + MXU_WIDTH)
            partner = (jnp.dot(hi[:, cs], swap, preferred_element_type=F32)
                       + jnp.dot(lo[:, cs], swap, preferred_element_type=F32))
            o_ref[:, cs] = (y[:, cs] * c + partner * s).astype(BF16)

    @pl.when(j >= n_rope_tiles)
    def _():
        o_ref[...] = jnp.dot(h_ref[...], w_ref[...], preferred_element_type=F32).astype(BF16)


def _qkv_rope(h, w, cos_t, sin_t, rope_width, jobs, tn=512):
    rows = h.shape[0]
    tm = rows // IN_ROW_TILES
    n = w.shape[1]
    kern = functools.partial(_qkv_rope_kernel, n_rope_tiles=rope_width // tn, tn=tn)
    lane = jnp.arange(MXU_WIDTH)
    swap = (lane[:, None] == (lane[None, :] ^ 1)).astype(BF16)
    return _hosted_call(
        kern, (h, w, cos_t, sin_t, swap),
        [pl.BlockSpec((tm, D_MODEL), lambda i, j: (i, 0)),
         pl.BlockSpec((D_MODEL, tn), lambda i, j: (0, j)),
         pl.BlockSpec((tm, MXU_WIDTH), lambda i, j: (i, 0)),
         pl.BlockSpec((tm, MXU_WIDTH), lambda i, j: (i, 0)),
         pl.BlockSpec((MXU_WIDTH, MXU_WIDTH), lambda i, j: (0, 0))],
        pl.BlockSpec((tm, tn), lambda i, j: (i, j)),
        jax.ShapeDtypeStruct((rows, n), BF16),
        (rows // tm, n // tn), jobs, "qkv_rope")


def _mm_out_kernel(a_ref, w_ref, p_ref, *refs, coef, n_x, want_h):
    x_refs, xo_ref, maybe_h_ref = refs[:n_x], refs[n_x], refs[n_x + 1:]
    post = coef * (p_ref[P_GATE:P_GATE + 1, :] * p_ref[P_POST_G:P_POST_G + 1, :])
    half = a_ref.shape[0] // 2
    for rs in (slice(0, half), slice(half, 2 * half)):
        y = jnp.dot(a_ref[rs, :], w_ref[...], preferred_element_type=F32)
        xn = _stream_tile(x_refs, rs) + _rms(y) * post
        xo_ref[rs, :] = xn
        if want_h:
            maybe_h_ref[0][rs, :] = _pre(xn, p_ref).astype(BF16)


def _mm_out(a, w, x, pp, s, *, rows, coef, want_h):
    k = a.shape[1]

    n_x = len(x) if isinstance(x, tuple) else 1

    def vmem_bytes(tm):
        return k * D_MODEL * 2 + 2 * tm * (k * 2 + D_MODEL * (4 * n_x + 4 + 2)) + tm * D_MODEL * 4

    tm = 512 if vmem_bytes(512) <= VMEM_LIMIT - (2 << 20) else 256
    x_arrays, x_specs = _stream_specs(x, tm)
    out_shape = [jax.ShapeDtypeStruct((rows, D_MODEL), F32)]
    out_specs = [pl.BlockSpec((tm, D_MODEL), lambda i: (i, 0))]
    if want_h:
        out_shape.append(jax.ShapeDtypeStruct((rows, D_MODEL), BF16))
        out_specs.append(pl.BlockSpec((tm, D_MODEL), lambda i: (i, 0)))
    res = pl.pallas_call(
        functools.partial(_mm_out_kernel, coef=coef, n_x=len(x_arrays), want_h=want_h),
        grid=(rows // tm,),
        in_specs=[
            pl.BlockSpec((tm, k), lambda i: (i, 0)),
            pl.BlockSpec((k, D_MODEL), lambda i: (0, 0), pipeline_mode=pl.Buffered(1)),
            pl.BlockSpec((None, None, 8, D_MODEL), lambda i: (s, _group_of_block(i, tm), 0, 0)),
        ] + x_specs,
        out_specs=out_specs,
        out_shape=out_shape,
        compiler_params=_params(dimension_semantics=("arbitrary",)),
        name="mm_out",
    )(a, w, pp, *x_arrays)
    return (res[0], res[1]) if want_h else (res[0], None)


def _dot_t(a, b):
    return lax.dot_general(a, b, (((1,), (1,)), ((), ())), preferred_element_type=F32)


def _dot_tn(a, b):
    return lax.dot_general(a, b, (((0,), (0,)), ((), ())), preferred_element_type=F32)


def _win_attn_kernel(sink_ref, q_ref, kv_ref, kvc_ref, o_ref):
    n = pl.program_id(1)
    n_lat = SEQ // A_BLOCK
    kvw = A_KV_HEADS * HEAD_DIM
    scale = HEAD_DIM ** -0.5
    expo = scale * math.log2(math.e)
    gq = A_GROUP * A_BLOCK
    lane = lax.broadcasted_iota(jnp.int32, (1, gq), 1)
    qi = lax.broadcasted_iota(jnp.int32, (A_BLOCK, gq), 1) & (A_BLOCK - 1)
    mi = lax.broadcasted_iota(jnp.int32, (A_BLOCK, gq), 0)

    def heads(kk):
        q4 = jnp.concatenate(
            [q_ref[:, (kk * A_GROUP + g) * HEAD_DIM:(kk * A_GROUP + g + 1) * HEAD_DIM] for g in range(A_GROUP)],
            axis=0)
        sink = jnp.full((1, gq), sink_ref[kk * A_GROUP] / scale, F32)
        for g in range(1, A_GROUP):
            sink = jnp.where(lane >= g * A_BLOCK, sink_ref[kk * A_GROUP + g] / scale, sink)
        return q4, sink, slice(kk * HEAD_DIM, (kk + 1) * HEAD_DIM), slice(kvw + kk * HEAD_DIM, kvw + (kk + 1) * HEAD_DIM)

    def finish(kk, pieces, sink):
        m = sink
        for s, _ in pieces:
            m = jnp.maximum(m, jnp.max(s, axis=0, keepdims=True))
        den = jnp.exp2((sink - m) * expo)
        ot = None
        for s, v in pieces:
            p = jnp.exp2((s - m) * expo)
            den = den + jnp.sum(p, axis=0, keepdims=True)
            pv = _dot_tn(v, p.astype(BF16))
            ot = pv if ot is None else ot + pv
        ot = ot / den
        for g in range(A_GROUP):
            h = kk * A_GROUP + g
            o_ref[:, h * HEAD_DIM:(h + 1) * HEAD_DIM] = ot[:, g * A_BLOCK:(g + 1) * A_BLOCK].T.astype(BF16)

    @pl.when(n < n_lat)
    def _():
        off_l = jnp.where(n > 0, 0, 2 * A_BLOCK)
        off_r = jnp.where(n < n_lat - 1, 0, 2 * A_BLOCK)
        ok_l = mi >= qi + off_l
        ok_r = mi <= qi - off_r
        rows_l, rows_m, rows_r = [pl.ds(pl.multiple_of(jnp.clip(n + d, 0, n_lat - 1) * A_BLOCK, A_BLOCK), A_BLOCK)
                                  for d in (-1, 0, 1)]
        def scores(kk):
            q4, sink, kc, vc = heads(kk)
            s_l = jnp.where(ok_l, _dot_t(kv_ref[rows_l, kc], q4), NEG_INF)
            s_r = jnp.where(ok_r, _dot_t(kv_ref[rows_r, kc], q4), NEG_INF)
            return ([(_dot_t(kvc_ref[:, kc], q4), kvc_ref[:, vc]), (s_l, kv_ref[rows_l, vc]),
                     (_dot_t(kv_ref[rows_m, kc], q4), kv_ref[rows_m, vc]), (s_r, kv_ref[rows_r, vc])], sink)

        nxt = scores(0)
        for kk in range(A_KV_HEADS):
            cur = nxt
            if kk + 1 < A_KV_HEADS:
                nxt = scores(kk + 1)
            finish(kk, *cur)

    @pl.when(n >= n_lat)
    def _():
        for kk in range(A_KV_HEADS):
            q4, sink, kc, vc = heads(kk)
            finish(kk, [(_dot_t(kvc_ref[:, kc], q4), kvc_ref[:, vc])], sink)


def _win_attn(qkv, sink, with_ctx):
    n_lat = SEQ // A_BLOCK
    n_ctx = CTX_LEN // A_BLOCK
    steps = n_lat + (n_ctx if with_ctx else 0)
    qw = A_HEADS * HEAD_DIM
    kvw2 = 2 * A_KV_HEADS * HEAD_DIM
    ctx_blk0 = LAT_ROWS // A_BLOCK

    def qrow(b, n):
        return jnp.where(n < n_lat, b * n_lat + n, ctx_blk0 + b * n_ctx + (n - n_lat))

    rows = LAT_ROWS + (CTX_ROWS if with_ctx else 0)
    return pl.pallas_call(
        _win_attn_kernel,
        grid=(BATCH, steps),
        in_specs=[
            pl.BlockSpec(memory_space=pltpu.SMEM),
            pl.BlockSpec((A_BLOCK, qw), lambda b, n: (qrow(b, n), 0)),
            pl.BlockSpec((SEQ, kvw2), lambda b, n: (b, qw // kvw2)),
            pl.BlockSpec((CTX_LEN, kvw2), lambda b, n: (LAT_ROWS // CTX_LEN + b, qw // kvw2)),
        ],
        out_specs=pl.BlockSpec((A_BLOCK, D_MODEL), lambda b, n: (qrow(b, n), 0)),
        out_shape=jax.ShapeDtypeStruct((rows, D_MODEL), BF16),
        compiler_params=_params(dimension_semantics=("arbitrary", "arbitrary")),
        name="win_attn",
    )(sink, qkv, qkv, qkv)


def _gmlp_mid_kernel(u_ref, v_ref, g_ref, b_ref, ws_ref, bs_ref, o_ref, *, tm):
    v = v_ref[...].astype(F32)
    mu = jnp.mean(v, axis=-1, keepdims=True)
    vc = v - mu
    vn = vc * lax.rsqrt(jnp.mean(vc * vc, axis=-1, keepdims=True) + NORM_EPS)
    vn = (vn * g_ref[...] + b_ref[...]).astype(BF16)
    for g in range(B_GROUPS):
        cs = slice(g * B_GROUP_W, (g + 1) * B_GROUP_W)
        ws = ws_ref[g].astype(BF16)
        for c in range(tm // B_CHUNK):
            rs = slice(c * B_CHUNK, (c + 1) * B_CHUNK)
            mixed = jnp.dot(ws, vn[rs, cs], preferred_element_type=F32) + bs_ref[:, g:g + 1]
            o_ref[rs, cs] = (u_ref[rs, cs].astype(F32) * mixed).astype(BF16)


def _gmlp_mid(z, vn_g, vn_b, ws, bs_t, tm=256):
    rows = z.shape[0]
    return pl.pallas_call(
        functools.partial(_gmlp_mid_kernel, tm=tm),
        grid=(rows // tm,),
        in_specs=[
            pl.BlockSpec((tm, B_WIDTH), lambda i: (i, 0)),
            pl.BlockSpec((tm, B_WIDTH), lambda i: (i, 1)),
            pl.BlockSpec((1, B_WIDTH), lambda i: (0, 0)),
            pl.BlockSpec((1, B_WIDTH), lambda i: (0, 0)),
            pl.BlockSpec((B_GROUPS, B_CHUNK, B_CHUNK), lambda i: (0, 0, 0)),
            pl.BlockSpec((B_CHUNK, B_GROUPS), lambda i: (0, 0)),
        ],
        out_specs=pl.BlockSpec((tm, B_WIDTH), lambda i: (i, 0)),
        out_shape=jax.ShapeDtypeStruct((rows, B_WIDTH), BF16),
        compiler_params=_params(dimension_semantics=("arbitrary",)),
        name="gmlp_mid",
    )(z, z, vn_g, vn_b, ws, bs_t)


def _diff_attn_kernel(lq1_ref, lk1_ref, lq2_ref, lk2_ref, sg_ref, q_ref, k_ref, v_ref, kc_ref, vc_ref, o_ref, vt_ref,
                      *, lam_init, tq, tk):
    qi = pl.program_id(2)
    n_lat = SEQ // tq
    expo = (HEAD_DIM ** -0.5) * math.log2(math.e)
    lam = (jnp.exp(jnp.sum(lq1_ref[...] * lk1_ref[...], axis=-1, keepdims=True))
           - jnp.exp(jnp.sum(lq2_ref[...] * lk2_ref[...], axis=-1, keepdims=True)) + lam_init)

    @pl.when(qi == 0)
    def _():
        vt_ref[:, :CTX_LEN] = vc_ref[...].T
        for t in range(SEQ // tk):
            vt_ref[:, CTX_LEN + t * tk:CTX_LEN + (t + 1) * tk] = v_ref[t * tk:(t + 1) * tk, :].T

    def run(chunks):
        qs = [q_ref[:, c * HEAD_DIM:(c + 1) * HEAD_DIM] for c in range(2)]

        def scores(chunk):
            kref, lo, size, _ = chunk
            return [_dot_t(kref[lo:lo + size, c * HEAD_DIM:(c + 1) * HEAD_DIM], qs[c]) for c in range(2)]

        m = [None, None]
        l = [None, None]
        acc = [None, None]
        s_next = scores(chunks[0])
        for t, (_, _, size, vlo) in enumerate(chunks):
            s_cur = s_next
            if t + 1 < len(chunks):
                s_next = scores(chunks[t + 1])
            for c in range(2):
                st = s_cur[c]
                m_new = jnp.max(st, axis=0, keepdims=True)
                if m[c] is not None:
                    m_new = jnp.maximum(m[c], m_new)
                p = jnp.exp2((st - m_new) * expo)
                pv = jnp.dot(vt_ref[:, vlo:vlo + size], p.astype(BF16), preferred_element_type=F32)
                if m[c] is None:
                    l[c] = jnp.sum(p, axis=0, keepdims=True)
                    acc[c] = pv
                else:
                    alpha = jnp.exp2((m[c] - m_new) * expo)
                    l[c] = alpha * l[c] + jnp.sum(p, axis=0, keepdims=True)
                    acc[c] = alpha * acc[c] + pv
                m[c] = m_new
        o = acc[0] / l[0] - lam * (acc[1] / l[1])
        o = o * lax.rsqrt(jnp.mean(o * o, axis=0, keepdims=True) + NORM_EPS) * (sg_ref[...] * (1.0 - lam_init))
        o_ref[...] = o.T.astype(BF16)

    @pl.when(qi < n_lat)
    def _():
        run([(kc_ref, 0, CTX_LEN, 0)] + [(k_ref, t * tk, tk, CTX_LEN + t * tk) for t in range(SEQ // tk)])

    @pl.when(qi >= n_lat)
    def _():
        run([(kc_ref, 0, CTX_LEN, 0)])


def _diff_attn(qkv, lq1, lk1, lq2, lk2, subln_g_col, lam_init, tq=256, tk=1024):
    hw = 2 * HEAD_DIM
    n_lat = SEQ // tq
    n_ctx = CTX_LEN // tq
    kcol = D_MODEL // hw
    ctx_blk0 = LAT_ROWS // tq

    def qrow(b, qi):
        return jnp.where(qi < n_lat, b * n_lat + qi, ctx_blk0 + b * n_ctx + (qi - n_lat))

    vec = pl.BlockSpec((1, HEAD_DIM), lambda b, h, qi: (0, 0))
    kern = functools.partial(_diff_attn_kernel, lam_init=lam_init, tq=tq, tk=tk)
    return pl.pallas_call(
        kern,
        grid=(BATCH, C_HEADS, n_lat + n_ctx),
        in_specs=[
            vec, vec, vec, vec,
            pl.BlockSpec((hw, 1), lambda b, h, qi: (0, 0)),
            pl.BlockSpec((tq, hw), lambda b, h, qi: (qrow(b, qi), h)),
            pl.BlockSpec((SEQ, hw), lambda b, h, qi: (b, kcol + h)),
            pl.BlockSpec((SEQ, hw), lambda b, h, qi: (b, 2 * kcol + h)),
            pl.BlockSpec((CTX_LEN, hw), lambda b, h, qi: (LAT_ROWS // CTX_LEN + b, kcol + h)),
            pl.BlockSpec((CTX_LEN, hw), lambda b, h, qi: (LAT_ROWS // CTX_LEN + b, 2 * kcol + h)),
        ],
        out_specs=pl.BlockSpec((tq, hw), lambda b, h, qi: (qrow(b, qi), h)),
        out_shape=jax.ShapeDtypeStruct((ROWS, D_MODEL), BF16),
        scratch_shapes=[pltpu.VMEM((hw, CTX_LEN + SEQ), BF16)],
        compiler_params=_params(dimension_semantics=("arbitrary", "arbitrary", "arbitrary")),
        name="diff_attn",
    )(lq1, lk1, lq2, lk2, subln_g_col, qkv, qkv, qkv, qkv, qkv)


def _rope_tables():
    rows = SEQ // GRID_W
    row = jnp.repeat(jnp.arange(rows, dtype=F32), GRID_W)
    col = jnp.tile(jnp.arange(GRID_W, dtype=F32), rows)
    axis_dim = HEAD_DIM // 2
    inv = ROPE_THETA ** (-jnp.arange(0, axis_dim, 2, dtype=F32) / axis_dim)
    ang = jnp.concatenate([row[:, None] * inv, col[:, None] * inv], axis=-1)
    cos = jnp.repeat(jnp.cos(ang), 2, axis=-1)
    sin = jnp.repeat(jnp.sin(ang), 2, axis=-1) * jnp.tile(jnp.array([-1.0, 1.0], F32), HEAD_DIM // 2)
    heads_per_chunk = MXU_WIDTH // HEAD_DIM
    cos_t = jnp.concatenate([jnp.tile(cos, (BATCH, heads_per_chunk)), jnp.ones((CTX_ROWS, MXU_WIDTH), F32)], axis=0)
    sin_t = jnp.concatenate([jnp.tile(sin, (BATCH, heads_per_chunk)), jnp.zeros((CTX_ROWS, MXU_WIDTH), F32)], axis=0)
    return cos_t, sin_t


def _param_tiles(mods, norm_g):
    m = mods[:, :N_GROUPS].reshape(DEPTH, N_GROUPS, N_MOD, D_MODEL)
    zero = jnp.zeros((N_GROUPS, D_MODEL), F32)
    bcast = lambda v: jnp.broadcast_to(v, (N_GROUPS, D_MODEL))
    tiles = []
    for s in range(-1, 3 * DEPTH):
        rows = [zero] * 8
        if s >= 0:
            i, slot = divmod(s, 3)
            rows[P_POST_G] = bcast(norm_g[i, 2 * slot + 1])
            rows[P_GATE] = m[i, :, 3 * slot + 2]
        if s + 1 < 3 * DEPTH:
            i, slot = divmod(s + 1, 3)
            rows[P_PRE_G] = bcast(norm_g[i, 2 * slot])
            rows[P_SCALE] = m[i, :, 3 * slot + 1]
            rows[P_SHIFT] = m[i, :, 3 * slot]
        tiles.append(jnp.stack(rows, axis=1))
    return jnp.stack(tiles, axis=0)


def kernel(x, c, ctx, c_ctx, ada_w, ada_b, norm_g, ffn_w_in, ffn_w_out, a_w_in, a_w_out, a_sink, b_w_in, b_vnorm_g,
           b_vnorm_b, b_ws, b_bs, b_w_out, c_w_in, c_w_out, c_lq1, c_lk1, c_lq2, c_lk2, c_subln_g):
    xs = (x.reshape(LAT_ROWS, D_MODEL), ctx.reshape(CTX_ROWS, D_MODEL))
    cvec = jnp.concatenate([c, c_ctx[None, :], jnp.zeros((8 - BATCH - 1, D_MODEL), F32)], axis=0)
    pp = _param_tiles(_ada_mods(cvec, ada_w, ada_b), norm_g)
    cos_t, sin_t = _rope_tables()

    mixer_w = ((a_w_in, a_w_out), (b_w_in, b_w_out), (c_w_in, c_w_out))
    wgu = _convert(_ffn_split_job(ffn_w_in, (0, 0)))
    h = _prenorm(xs, pp, 0)
    for i in range(DEPTH):
        kind, j = i % N_MIXERS, i // N_MIXERS
        ctx_live = i < DEPTH - 1
        rows_out = ROWS if ctx_live else LAT_ROWS
        s0 = 3 * i + 1
        w_mix_in, w_mix_out = mixer_w[kind]

        ffn2_split = _ffn_split_job(ffn_w_in, (i, 1))
        jobs = [_cast_job(ffn_w_out, (i, 0), D_FF_PAD), _cast_job(w_mix_in, (j,))] + ([ffn2_split] if kind != 1 else [])
        a, conv = _ffn_in(h, wgu, jobs)
        xs, h = _mm_out(a, conv[0], xs, pp, s0, rows=ROWS, coef=0.5, want_h=True)
        w_i = conv[1]

        jobs = [_cast_job(w_mix_out, (j,))] + ([ffn2_split] if kind == 1 else [])
        if kind == 0:
            qkv, mix_conv = _qkv_rope(h, w_i, cos_t, sin_t, (A_HEADS + A_KV_HEADS) * HEAD_DIM, jobs)
            o = _win_attn(qkv, a_sink[j], ctx_live)
        elif kind == 1:
            z, mix_conv = _gmlp_in(h, w_i, jobs)
            o = _gmlp_mid(z, b_vnorm_g[j][None, :], b_vnorm_b[j][None, :], b_ws[j], b_bs[j].T)
        else:
            lam_init = 0.8 - 0.6 * math.exp(-0.3 * i)
            qkv, mix_conv = _qkv_rope(h, w_i, cos_t, sin_t, 2 * D_MODEL, jobs)
            o = _diff_attn(qkv, c_lq1[j][None, :], c_lk1[j][None, :], c_lq2[j][None, :], c_lk2[j][None, :],
                           c_subln_g[j][:, None], lam_init)
        w_o = mix_conv[0]
        wgu = mix_conv[1] if kind == 1 else conv[2]
        xs, h = _mm_out(o, w_o, xs, pp, s0 + 1, rows=rows_out, coef=1.0, want_h=True)

        jobs = [_cast_job(ffn_w_out, (i, 1), D_FF_PAD)]
        if i + 1 < DEPTH:
            jobs.append(_ffn_split_job(ffn_w_in, (i + 1, 0)))
        a, conv = _ffn_in(h, wgu, jobs)
        xs, h = _mm_out(a, conv[0], xs, pp, s0 + 2, rows=rows_out, coef=0.5, want_h=i + 1 < DEPTH)
        wgu = conv[1] if i + 1 < DEPTH else None
    return xs.reshape(BATCH, SEQ, D_MODEL)
```

```python
import functools
import math
from typing import Callable, NamedTuple

import jax
import jax.numpy as jnp
from jax import lax
from jax.experimental import pallas as pl
from jax.experimental.pallas import tpu as pltpu

D_MODEL = 2048
BATCH = 2
SEQ = 4096
DEPTH = 4
GRID_W = 64
CTX_LEN = 256
N_MIXERS = 3
N_MOD = 9
NORM_EPS = 1e-6
ROPE_THETA = 10000.0
NEG_INF = -1e30
D_FF = 5504
HEAD_DIM = 128
A_HEADS = 16
A_KV_HEADS = 4
A_GROUP = 4
A_BLOCK = 128
B_CHUNK = 128
B_WIDTH = 3 * D_MODEL
B_GROUPS = 8
B_GROUP_W = B_WIDTH // B_GROUPS
C_HEADS = 8

LAT_ROWS = BATCH * SEQ
CTX_ROWS = BATCH * CTX_LEN
ROWS = LAT_ROWS + CTX_ROWS
N_GROUPS = 3
LANES = 128
BF16_SUBLANES = 16
MXU_WIDTH = 256
IN_ROW_TILES = 8
FF_TILE = 512
D_FF_PAD = -(-D_FF // FF_TILE) * FF_TILE
VMEM_LIMIT = 62 * 1024 * 1024

BF16 = jnp.bfloat16
F32 = jnp.float32

P_POST_G, P_GATE, P_PRE_G, P_SCALE, P_SHIFT = 0, 1, 2, 3, 4


def _params(**kw):
    return pltpu.CompilerParams(vmem_limit_bytes=VMEM_LIMIT, **kw)


def _group_of_block(i, tm):
    return jnp.minimum(i // (SEQ // tm), N_GROUPS - 1)


def _rms(x):
    return x * lax.rsqrt(jnp.mean(x * x, axis=-1, keepdims=True) + NORM_EPS)


def _ada_kernel(c_ref, w_ref, b_ref, o_ref):
    @pl.when(pl.program_id(1) == 0)
    def _():
        o_ref[...] = jnp.broadcast_to(b_ref[...], o_ref.shape)

    c = c_ref[...]
    a = (c * jax.nn.sigmoid(c)).astype(BF16)
    o_ref[...] += jnp.dot(a, w_ref[...].astype(BF16), preferred_element_type=F32)


def _ada_mods(cvec, ada_w, ada_b):
    kb = LANES
    n = N_MOD * D_MODEL
    return pl.pallas_call(
        _ada_kernel,
        grid=(DEPTH, D_MODEL // kb),
        in_specs=[
            pl.BlockSpec((8, kb), lambda l, k: (0, k)),
            pl.BlockSpec((None, kb, n), lambda l, k: (l, k, 0)),
            pl.BlockSpec((None, 1, n), lambda l, k: (l, 0, 0)),
        ],
        out_specs=pl.BlockSpec((None, 8, n), lambda l, k: (l, 0, 0)),
        out_shape=jax.ShapeDtypeStruct((DEPTH, 8, n), F32),
        compiler_params=_params(dimension_semantics=("arbitrary", "arbitrary")),
        name="ada_mods",
    )(cvec, ada_w, ada_b.reshape(DEPTH, 1, n))


def _pre(x, p_ref):
    gain = p_ref[P_PRE_G:P_PRE_G + 1, :] * (1.0 + p_ref[P_SCALE:P_SCALE + 1, :])
    return _rms(x) * gain + p_ref[P_SHIFT:P_SHIFT + 1, :]


def _stream_specs(x, tm):
    if not isinstance(x, tuple):
        return (x,), [pl.BlockSpec((tm, D_MODEL), lambda i: (i, 0))]
    n_lat = LAT_ROWS // tm
    return x, [pl.BlockSpec((tm, D_MODEL), lambda i: (jnp.minimum(i, n_lat - 1), 0)),
               pl.BlockSpec((tm, D_MODEL), lambda i: (jnp.maximum(i - n_lat, 0), 0))]


def _stream_tile(x_refs, rs=slice(None)):
    if len(x_refs) == 1:
        return x_refs[0][rs, :]
    n_lat = LAT_ROWS // x_refs[0].shape[0]
    return jnp.where(pl.program_id(0) < n_lat, x_refs[0][rs, :], x_refs[1][rs, :])


def _prenorm_kernel(*refs):
    *x_refs, p_ref, h_ref = refs
    h_ref[...] = _pre(_stream_tile(x_refs), p_ref).astype(BF16)


def _prenorm(x, pp, s, tm=512):
    x_arrays, x_specs = _stream_specs(x, tm)
    return pl.pallas_call(
        _prenorm_kernel,
        grid=(ROWS // tm,),
        in_specs=x_specs + [
            pl.BlockSpec((None, None, 8, D_MODEL), lambda i: (s, _group_of_block(i, tm), 0, 0)),
        ],
        out_specs=pl.BlockSpec((tm, D_MODEL), lambda i: (i, 0)),
        out_shape=jax.ShapeDtypeStruct((ROWS, D_MODEL), BF16),
        compiler_params=_params(dimension_semantics=("arbitrary",)),
        name="prenorm",
    )(*x_arrays, pp)


class _Job(NamedTuple):
    src: jax.Array
    in_spec: pl.BlockSpec
    out_spec: pl.BlockSpec
    out_shape: jax.ShapeDtypeStruct
    body: Callable
    n_blocks: int


def _row_block(rows, rows_out, steps):
    g = math.gcd(rows, rows_out)
    for rb in range(BF16_SUBLANES, g + 1, BF16_SUBLANES):
        if g % rb == 0 and rows_out // rb <= steps:
            return rb
    raise ValueError(f"no row block for {rows}->{rows_out} rows in {steps} steps")


def _cast_job(src, lead, rows_out=None):
    rows, cols = src.shape[-2:]
    rows_out = rows_out or rows

    def make(steps, lin):
        rb = _row_block(rows, rows_out, steps)
        nb_in, nb_out = rows // rb, rows_out // rb

        def body(src_ref, dst_ref, blk):
            v = src_ref[...].astype(BF16)
            if nb_out > nb_in:
                v = jnp.where(blk < nb_in, v, jnp.zeros_like(v))
            dst_ref[...] = v

        return _Job(
            src,
            pl.BlockSpec((None,) * len(lead) + (rb, cols),
                         lambda *g: lead + (jnp.minimum(lin(*g), nb_in - 1), 0)),
            pl.BlockSpec((rb, cols), lambda *g: (jnp.minimum(lin(*g), nb_out - 1), 0)),
            jax.ShapeDtypeStruct((rows_out, cols), BF16), body, nb_out)

    return make


def _ffn_split_job(ffn_w_in, lead):
    def make(steps, lin):
        rb = _row_block(D_MODEL, D_MODEL, steps)
        nb = D_MODEL // rb

        def body(src_ref, dst_ref, blk):
            del blk
            for part in range(2):
                dst_ref[part, :, :D_FF] = src_ref[:, part * D_FF:(part + 1) * D_FF].astype(BF16)
                dst_ref[part, :, D_FF:] = jnp.zeros((rb, D_FF_PAD - D_FF), BF16)

        return _Job(
            ffn_w_in,
            pl.BlockSpec((None,) * len(lead) + (rb, 2 * D_FF), lambda *g: lead + (jnp.minimum(lin(*g), nb - 1), 0)),
            pl.BlockSpec((2, rb, D_FF_PAD), lambda *g: (0, jnp.minimum(lin(*g), nb - 1), 0)),
            jax.ShapeDtypeStruct((2, D_MODEL, D_FF_PAD), BF16), body, nb)

    return make


def _hosted_call(main, main_args, main_in_specs, out_spec, out_shape, grid, job_makers, name):
    steps = math.prod(grid)
    strides = [math.prod(grid[d + 1:]) for d in range(len(grid))]
    lin = lambda *g: sum(gi * st for gi, st in zip(g, strides))
    jobs = [mk(steps, lin) for mk in job_makers]
    n_in, n_jobs = len(main_args), len(jobs)

    def kern(*refs):
        if main is not None:
            main(*refs[:n_in], refs[n_in + n_jobs])
        t = lin(*[pl.program_id(d) for d in range(len(grid))])
        n_main_out = 0 if main is None else 1
        for q, jb in enumerate(jobs):
            jb.body(refs[n_in + q], refs[n_in + n_jobs + n_main_out + q], jnp.minimum(t, jb.n_blocks - 1))

    main_out = [] if main is None else [(out_spec, out_shape)]
    res = pl.pallas_call(
        kern,
        grid=grid,
        in_specs=list(main_in_specs) + [jb.in_spec for jb in jobs],
        out_specs=[s for s, _ in main_out] + [jb.out_spec for jb in jobs],
        out_shape=[s for _, s in main_out] + [jb.out_shape for jb in jobs],
        compiler_params=_params(dimension_semantics=("arbitrary",) * len(grid)),
        name=name,
    )(*main_args, *[jb.src for jb in jobs])
    return (None, list(res)) if main is None else (res[0], list(res[1:]))


def _convert(job_maker, steps=32):
    return _hosted_call(None, (), (), None, None, (steps,), [job_maker], "convert")[1][0]


def _swiglu_kernel(h_ref, wg_ref, wu_ref, o_ref):
    h = h_ref[...]
    half = o_ref.shape[1] // 2
    parts = []
    for cs in (slice(0, half), slice(half, 2 * half)):
        parts.append((cs, jnp.dot(h, wg_ref[:, cs], preferred_element_type=F32),
                      jnp.dot(h, wu_ref[:, cs], preferred_element_type=F32)))
    for cs, g, u in parts:
        o_ref[:, cs] = (g * jax.nn.sigmoid(g) * u).astype(BF16)


def _ffn_in(h, wgu, jobs, tn=FF_TILE):
    rows = h.shape[0]
    tm = rows // IN_ROW_TILES
    return _hosted_call(
        _swiglu_kernel, (h, wgu, wgu),
        [pl.BlockSpec((tm, D_MODEL), lambda j, i: (i, 0)),
         pl.BlockSpec((None, D_MODEL, tn), lambda j, i: (0, 0, j)),
         pl.BlockSpec((None, D_MODEL, tn), lambda j, i: (1, 0, j))],
        pl.BlockSpec((tm, tn), lambda j, i: (i, j)),
        jax.ShapeDtypeStruct((rows, D_FF_PAD), BF16),
        (D_FF_PAD // tn, rows // tm), jobs, "ffn_in")


def _gelu_tanh(y):
    c0 = math.sqrt(2.0 / math.pi)
    half_y = 0.5 * y
    return half_y + half_y * jnp.tanh(y * (c0 + (c0 * 0.044715) * (y * y)))


def _gelu_kernel(h_ref, w_ref, o_ref):
    half = h_ref.shape[0] // 2
    halves = (slice(0, half), slice(half, 2 * half))
    ys = [jnp.dot(h_ref[rs, :], w_ref[...], preferred_element_type=F32) for rs in halves]
    for rs, y in zip(halves, ys):
        o_ref[rs, :] = _gelu_tanh(y).astype(BF16)


def _gmlp_in(h, w, jobs, tn=2048):
    rows = h.shape[0]
    tm = rows // IN_ROW_TILES
    n = w.shape[1]
    return _hosted_call(
        _gelu_kernel, (h, w),
        [pl.BlockSpec((tm, D_MODEL), lambda j, i: (i, 0)),
         pl.BlockSpec((D_MODEL, tn), lambda j, i: (0, j))],
        pl.BlockSpec((tm, tn), lambda j, i: (i, j)),
        jax.ShapeDtypeStruct((rows, n), BF16),
        (n // tn, rows // tm), jobs, "gmlp_in")


def _qkv_rope_kernel(h_ref, w_ref, cos_ref, sin_ref, swap_ref, o_ref, *, n_rope_tiles, tn, col_major):
    j = pl.program_id(1)

    def store(q0, val):
        if col_major:
            o_ref[q0 // MXU_WIDTH] = val
        else:
            o_ref[:, q0:q0 + MXU_WIDTH] = val

    @pl.when(j < n_rope_tiles)
    def _():
        h = h_ref[...]
        c = cos_ref[...]
        s = sin_ref[...]
        swap = swap_ref[...]
        y = jnp.dot(h, w_ref[...], preferred_element_type=F32)
        hi = y.astype(BF16)
        lo = (y - hi.astype(F32)).astype(BF16)
        for q0 in range(0, tn, MXU_WIDTH):
            cs = slice(q0, q0 + MXU_WIDTH)
            partner = (jnp.dot(hi[:, cs], swap, preferred_element_type=F32)
                       + jnp.dot(lo[:, cs], swap, preferred_element_type=F32))
            store(q0, (y[:, cs] * c + partner * s).astype(BF16))

    @pl.when(j >= n_rope_tiles)
    def _():
        y = jnp.dot(h_ref[...], w_ref[...], preferred_element_type=F32).astype(BF16)
        for q0 in range(0, tn, MXU_WIDTH):
            store(q0, y[:, q0:q0 + MXU_WIDTH])


def _qkv_rope(h, w, cos_t, sin_t, rope_width, jobs, col_major, tn=512):
    rows = h.shape[0]
    tm = rows // IN_ROW_TILES
    n = w.shape[1]
    kern = functools.partial(_qkv_rope_kernel, n_rope_tiles=rope_width // tn, tn=tn, col_major=col_major)
    if col_major:
        out_spec = pl.BlockSpec((tn // MXU_WIDTH, tm, MXU_WIDTH), lambda i, j: (j, i, 0))
        out_shape = jax.ShapeDtypeStruct((n // MXU_WIDTH, rows, MXU_WIDTH), BF16)
    else:
        out_spec = pl.BlockSpec((tm, tn), lambda i, j: (i, j))
        out_shape = jax.ShapeDtypeStruct((rows, n), BF16)
    lane = jnp.arange(MXU_WIDTH)
    swap = (lane[:, None] == (lane[None, :] ^ 1)).astype(BF16)
    return _hosted_call(
        kern, (h, w, cos_t, sin_t, swap),
        [pl.BlockSpec((tm, D_MODEL), lambda i, j: (i, 0)),
         pl.BlockSpec((D_MODEL, tn), lambda i, j: (0, j)),
         pl.BlockSpec((tm, MXU_WIDTH), lambda i, j: (i, 0)),
         pl.BlockSpec((tm, MXU_WIDTH), lambda i, j: (i, 0)),
         pl.BlockSpec((MXU_WIDTH, MXU_WIDTH), lambda i, j: (0, 0))],
        out_spec, out_shape, (rows // tm, n // tn), jobs, "qkv_rope")


def _mm_out_kernel(a_ref, w_ref, p_ref, *refs, coef, n_x, want_h):
    x_refs, xo_ref, maybe_h_ref = refs[:n_x], refs[n_x], refs[n_x + 1:]
    post = coef * (p_ref[P_GATE:P_GATE + 1, :] * p_ref[P_POST_G:P_POST_G + 1, :])
    half = a_ref.shape[0] // 2
    for rs in (slice(0, half), slice(half, 2 * half)):
        y = jnp.dot(a_ref[rs, :], w_ref[...], preferred_element_type=F32)
        xn = _stream_tile(x_refs, rs) + _rms(y) * post
        xo_ref[rs, :] = xn
        if want_h:
            maybe_h_ref[0][rs, :] = _pre(xn, p_ref).astype(BF16)


def _mm_out(a, w, x, pp, s, *, rows, coef, want_h):
    k = a.shape[1]

    n_x = len(x) if isinstance(x, tuple) else 1

    def vmem_bytes(tm):
        return k * D_MODEL * 2 + 2 * tm * (k * 2 + D_MODEL * (4 * n_x + 4 + 2)) + tm * D_MODEL * 4

    tm = 512 if vmem_bytes(512) <= VMEM_LIMIT - (2 << 20) else 256
    x_arrays, x_specs = _stream_specs(x, tm)
    out_shape = [jax.ShapeDtypeStruct((rows, D_MODEL), F32)]
    out_specs = [pl.BlockSpec((tm, D_MODEL), lambda i: (i, 0))]
    if want_h:
        out_shape.append(jax.ShapeDtypeStruct((rows, D_MODEL), BF16))
        out_specs.append(pl.BlockSpec((tm, D_MODEL), lambda i: (i, 0)))
    res = pl.pallas_call(
        functools.partial(_mm_out_kernel, coef=coef, n_x=len(x_arrays), want_h=want_h),
        grid=(rows // tm,),
        in_specs=[
            pl.BlockSpec((tm, k), lambda i: (i, 0)),
            pl.BlockSpec((k, D_MODEL), lambda i: (0, 0), pipeline_mode=pl.Buffered(1)),
            pl.BlockSpec((None, None, 8, D_MODEL), lambda i: (s, _group_of_block(i, tm), 0, 0)),
        ] + x_specs,
        out_specs=out_specs,
        out_shape=out_shape,
        compiler_params=_params(dimension_semantics=("arbitrary",)),
        name="mm_out",
    )(a, w, pp, *x_arrays)
    return (res[0], res[1]) if want_h else (res[0], None)


def _dot_t(a, b):
    return lax.dot_general(a, b, (((1,), (1,)), ((), ())), preferred_element_type=F32)


def _dot_tn(a, b):
    return lax.dot_general(a, b, (((0,), (0,)), ((), ())), preferred_element_type=F32)


def _win_attn_kernel(sink_ref, q_ref, kv_ref, kvc_ref, o_ref):
    n = pl.program_id(1)
    n_lat = SEQ // A_BLOCK
    kvw = A_KV_HEADS * HEAD_DIM
    scale = HEAD_DIM ** -0.5
    expo = scale * math.log2(math.e)
    gq = A_GROUP * A_BLOCK
    lane = lax.broadcasted_iota(jnp.int32, (1, gq), 1)
    qi = lax.broadcasted_iota(jnp.int32, (A_BLOCK, gq), 1) & (A_BLOCK - 1)
    mi = lax.broadcasted_iota(jnp.int32, (A_BLOCK, gq), 0)

    def heads(kk):
        q4 = jnp.concatenate(
            [q_ref[:, (kk * A_GROUP + g) * HEAD_DIM:(kk * A_GROUP + g + 1) * HEAD_DIM] for g in range(A_GROUP)],
            axis=0)
        sink = jnp.full((1, gq), sink_ref[kk * A_GROUP] / scale, F32)
        for g in range(1, A_GROUP):
            sink = jnp.where(lane >= g * A_BLOCK, sink_ref[kk * A_GROUP + g] / scale, sink)
        return q4, sink, slice(kk * HEAD_DIM, (kk + 1) * HEAD_DIM), slice(kvw + kk * HEAD_DIM, kvw + (kk + 1) * HEAD_DIM)

    def finish(kk, pieces, sink):
        m = sink
        for s, _ in pieces:
            m = jnp.maximum(m, jnp.max(s, axis=0, keepdims=True))
        den = jnp.exp2((sink - m) * expo)
        ot = None
        for s, v in pieces:
            p = jnp.exp2((s - m) * expo)
            den = den + jnp.sum(p, axis=0, keepdims=True)
            pv = _dot_tn(v, p.astype(BF16))
            ot = pv if ot is None else ot + pv
        ot = ot / den
        for g in range(A_GROUP):
            h = kk * A_GROUP + g
            o_ref[:, h * HEAD_DIM:(h + 1) * HEAD_DIM] = ot[:, g * A_BLOCK:(g + 1) * A_BLOCK].T.astype(BF16)

    @pl.when(n < n_lat)
    def _():
        off_l = jnp.where(n > 0, 0, 2 * A_BLOCK)
        off_r = jnp.where(n < n_lat - 1, 0, 2 * A_BLOCK)
        ok_l = mi >= qi + off_l
        ok_r = mi <= qi - off_r
        rows_l, rows_m, rows_r = [pl.ds(pl.multiple_of(jnp.clip(n + d, 0, n_lat - 1) * A_BLOCK, A_BLOCK), A_BLOCK)
                                  for d in (-1, 0, 1)]
        def scores(kk):
            q4, sink, kc, vc = heads(kk)
            s_l = jnp.where(ok_l, _dot_t(kv_ref[rows_l, kc], q4), NEG_INF)
            s_r = jnp.where(ok_r, _dot_t(kv_ref[rows_r, kc], q4), NEG_INF)
            return ([(_dot_t(kvc_ref[:, kc], q4), kvc_ref[:, vc]), (s_l, kv_ref[rows_l, vc]),
                     (_dot_t(kv_ref[rows_m, kc], q4), kv_ref[rows_m, vc]), (s_r, kv_ref[rows_r, vc])], sink)

        nxt = scores(0)
        for kk in range(A_KV_HEADS):
            cur = nxt
            if kk + 1 < A_KV_HEADS:
                nxt = scores(kk + 1)
            finish(kk, *cur)

    @pl.when(n >= n_lat)
    def _():
        for kk in range(A_KV_HEADS):
            q4, sink, kc, vc = heads(kk)
            finish(kk, [(_dot_t(kvc_ref[:, kc], q4), kvc_ref[:, vc])], sink)


def _win_attn(qkv, sink, with_ctx):
    n_lat = SEQ // A_BLOCK
    n_ctx = CTX_LEN // A_BLOCK
    steps = n_lat + (n_ctx if with_ctx else 0)
    qw = A_HEADS * HEAD_DIM
    kvw2 = 2 * A_KV_HEADS * HEAD_DIM
    ctx_blk0 = LAT_ROWS // A_BLOCK

    def qrow(b, n):
        return jnp.where(n < n_lat, b * n_lat + n, ctx_blk0 + b * n_ctx + (n - n_lat))

    rows = LAT_ROWS + (CTX_ROWS if with_ctx else 0)
    return pl.pallas_call(
        _win_attn_kernel,
        grid=(BATCH, steps),
        in_specs=[
            pl.BlockSpec(memory_space=pltpu.SMEM),
            pl.BlockSpec((A_BLOCK, qw), lambda b, n: (qrow(b, n), 0)),
            pl.BlockSpec((SEQ, kvw2), lambda b, n: (b, qw // kvw2)),
            pl.BlockSpec((CTX_LEN, kvw2), lambda b, n: (LAT_ROWS // CTX_LEN + b, qw // kvw2)),
        ],
        out_specs=pl.BlockSpec((A_BLOCK, D_MODEL), lambda b, n: (qrow(b, n), 0)),
        out_shape=jax.ShapeDtypeStruct((rows, D_MODEL), BF16),
        compiler_params=_params(dimension_semantics=("arbitrary", "arbitrary")),
        name="win_attn",
    )(sink, qkv, qkv, qkv)


def _gmlp_mid_kernel(u_ref, v_ref, g_ref, b_ref, ws_ref, bs_ref, o_ref, *, tm):
    v = v_ref[...].astype(F32)
    mu = jnp.mean(v, axis=-1, keepdims=True)
    vc = v - mu
    vn = vc * lax.rsqrt(jnp.mean(vc * vc, axis=-1, keepdims=True) + NORM_EPS)
    vn = (vn * g_ref[...] + b_ref[...]).astype(BF16)
    for g in range(B_GROUPS):
        cs = slice(g * B_GROUP_W, (g + 1) * B_GROUP_W)
        ws = ws_ref[g].astype(BF16)
        for c in range(tm // B_CHUNK):
            rs = slice(c * B_CHUNK, (c + 1) * B_CHUNK)
            mixed = jnp.dot(ws, vn[rs, cs], preferred_element_type=F32) + bs_ref[:, g:g + 1]
            o_ref[rs, cs] = (u_ref[rs, cs].astype(F32) * mixed).astype(BF16)


def _gmlp_mid(z, vn_g, vn_b, ws, bs_t, tm=256):
    rows = z.shape[0]
    return pl.pallas_call(
        functools.partial(_gmlp_mid_kernel, tm=tm),
        grid=(rows // tm,),
        in_specs=[
            pl.BlockSpec((tm, B_WIDTH), lambda i: (i, 0)),
            pl.BlockSpec((tm, B_WIDTH), lambda i: (i, 1)),
            pl.BlockSpec((1, B_WIDTH), lambda i: (0, 0)),
            pl.BlockSpec((1, B_WIDTH), lambda i: (0, 0)),
            pl.BlockSpec((B_GROUPS, B_CHUNK, B_CHUNK), lambda i: (0, 0, 0)),
            pl.BlockSpec((B_CHUNK, B_GROUPS), lambda i: (0, 0)),
        ],
        out_specs=pl.BlockSpec((tm, B_WIDTH), lambda i: (i, 0)),
        out_shape=jax.ShapeDtypeStruct((rows, B_WIDTH), BF16),
        compiler_params=_params(dimension_semantics=("arbitrary",)),
        name="gmlp_mid",
    )(z, z, vn_g, vn_b, ws, bs_t)


def _diff_attn_kernel(lq1_ref, lk1_ref, lq2_ref, lk2_ref, sg_ref, q_ref, k_ref, v_ref, kc_ref, vc_ref, o_ref, vt_ref,
                      *, lam_init, tq, tk):
    qi = pl.program_id(2)
    n_lat = SEQ // tq
    expo = (HEAD_DIM ** -0.5) * math.log2(math.e)
    lam = (jnp.exp(jnp.sum(lq1_ref[...] * lk1_ref[...], axis=-1, keepdims=True))
           - jnp.exp(jnp.sum(lq2_ref[...] * lk2_ref[...], axis=-1, keepdims=True)) + lam_init)

    @pl.when(qi == 0)
    def _():
        vt_ref[:, :CTX_LEN] = vc_ref[...].T
        for t in range(SEQ // tk):
            vt_ref[:, CTX_LEN + t * tk:CTX_LEN + (t + 1) * tk] = v_ref[t * tk:(t + 1) * tk, :].T

    def run(chunks):
        qs = [q_ref[:, c * HEAD_DIM:(c + 1) * HEAD_DIM] for c in range(2)]

        def scores(chunk):
            kref, lo, size, _ = chunk
            return [_dot_t(kref[lo:lo + size, c * HEAD_DIM:(c + 1) * HEAD_DIM], qs[c]) for c in range(2)]

        m = [None, None]
        l = [None, None]
        acc = [None, None]
        s_next = scores(chunks[0])
        for t, (_, _, size, vlo) in enumerate(chunks):
            s_cur = s_next
            if t + 1 < len(chunks):
                s_next = scores(chunks[t + 1])
            for c in range(2):
                st = s_cur[c]
                m_new = jnp.max(st, axis=0, keepdims=True)
                if m[c] is not None:
                    m_new = jnp.maximum(m[c], m_new)
                p = jnp.exp2((st - m_new) * expo)
                pv = jnp.dot(vt_ref[:, vlo:vlo + size], p.astype(BF16), preferred_element_type=F32)
                if m[c] is None:
                    l[c] = jnp.sum(p, axis=0, keepdims=True)
                    acc[c] = pv
                else:
                    alpha = jnp.exp2((m[c] - m_new) * expo)
                    l[c] = alpha * l[c] + jnp.sum(p, axis=0, keepdims=True)
                    acc[c] = alpha * acc[c] + pv
                m[c] = m_new
        o = acc[0] / l[0] - lam * (acc[1] / l[1])
        o = o * lax.rsqrt(jnp.mean(o * o, axis=0, keepdims=True) + NORM_EPS) * (sg_ref[...] * (1.0 - lam_init))
        o_ref[...] = o.T.astype(BF16)

    @pl.when(qi < n_lat)
    def _():
        run([(kc_ref, 0, CTX_LEN, 0)] + [(k_ref, t * tk, tk, CTX_LEN + t * tk) for t in range(SEQ // tk)])

    @pl.when(qi >= n_lat)
    def _():
        run([(kc_ref, 0, CTX_LEN, 0)])


def _diff_attn(qkv, lq1, lk1, lq2, lk2, subln_g_col, lam_init, tq=256, tk=1024):
    hw = 2 * HEAD_DIM
    n_lat = SEQ // tq
    n_ctx = CTX_LEN // tq
    kcol = D_MODEL // hw
    ctx_blk0 = LAT_ROWS // tq
    ctx_blk = LAT_ROWS // CTX_LEN

    def qrow(b, qi):
        return jnp.where(qi < n_lat, b * n_lat + qi, ctx_blk0 + b * n_ctx + (qi - n_lat))

    vec = pl.BlockSpec((1, HEAD_DIM), lambda b, h, qi: (0, 0))
    kern = functools.partial(_diff_attn_kernel, lam_init=lam_init, tq=tq, tk=tk)
    return pl.pallas_call(
        kern,
        grid=(BATCH, C_HEADS, n_lat + n_ctx),
        in_specs=[
            vec, vec, vec, vec,
            pl.BlockSpec((hw, 1), lambda b, h, qi: (0, 0)),
            pl.BlockSpec((None, tq, hw), lambda b, h, qi: (h, qrow(b, qi), 0)),
            pl.BlockSpec((None, SEQ, hw), lambda b, h, qi: (kcol + h, b, 0)),
            pl.BlockSpec((None, SEQ, hw), lambda b, h, qi: (2 * kcol + h, b, 0)),
            pl.BlockSpec((None, CTX_LEN, hw), lambda b, h, qi: (kcol + h, ctx_blk + b, 0)),
            pl.BlockSpec((None, CTX_LEN, hw), lambda b, h, qi: (2 * kcol + h, ctx_blk + b, 0)),
        ],
        out_specs=pl.BlockSpec((tq, hw), lambda b, h, qi: (qrow(b, qi), h)),
        out_shape=jax.ShapeDtypeStruct((ROWS, D_MODEL), BF16),
        scratch_shapes=[pltpu.VMEM((hw, CTX_LEN + SEQ), BF16)],
        compiler_params=_params(dimension_semantics=("arbitrary", "arbitrary", "arbitrary")),
        name="diff_attn",
    )(lq1, lk1, lq2, lk2, subln_g_col, qkv, qkv, qkv, qkv, qkv)


def _rope_tables():
    rows = SEQ // GRID_W
    row = jnp.repeat(jnp.arange(rows, dtype=F32), GRID_W)
    col = jnp.tile(jnp.arange(GRID_W, dtype=F32), rows)
    axis_dim = HEAD_DIM // 2
    inv = ROPE_THETA ** (-jnp.arange(0, axis_dim, 2, dtype=F32) / axis_dim)
    ang = jnp.concatenate([row[:, None] * inv, col[:, None] * inv], axis=-1)
    cos = jnp.repeat(jnp.cos(ang), 2, axis=-1)
    sin = jnp.repeat(jnp.sin(ang), 2, axis=-1) * jnp.tile(jnp.array([-1.0, 1.0], F32), HEAD_DIM // 2)
    heads_per_chunk = MXU_WIDTH // HEAD_DIM
    cos_t = jnp.concatenate([jnp.tile(cos, (BATCH, heads_per_chunk)), jnp.ones((CTX_ROWS, MXU_WIDTH), F32)], axis=0)
    sin_t = jnp.concatenate([jnp.tile(sin, (BATCH, heads_per_chunk)), jnp.zeros((CTX_ROWS, MXU_WIDTH), F32)], axis=0)
    return cos_t, sin_t


def _param_tiles(mods, norm_g):
    m = mods[:, :N_GROUPS].reshape(DEPTH, N_GROUPS, N_MOD, D_MODEL)
    zero = jnp.zeros((N_GROUPS, D_MODEL), F32)
    bcast = lambda v: jnp.broadcast_to(v, (N_GROUPS, D_MODEL))
    tiles = []
    for s in range(-1, 3 * DEPTH):
        rows = [zero] * 8
        if s >= 0:
            i, slot = divmod(s, 3)
            rows[P_POST_G] = bcast(norm_g[i, 2 * slot + 1])
            rows[P_GATE] = m[i, :, 3 * slot + 2]
        if s + 1 < 3 * DEPTH:
            i, slot = divmod(s + 1, 3)
            rows[P_PRE_G] = bcast(norm_g[i, 2 * slot])
            rows[P_SCALE] = m[i, :, 3 * slot + 1]
            rows[P_SHIFT] = m[i, :, 3 * slot]
        tiles.append(jnp.stack(rows, axis=1))
    return jnp.stack(tiles, axis=0)


def kernel(x, c, ctx, c_ctx, ada_w, ada_b, norm_g, ffn_w_in, ffn_w_out, a_w_in, a_w_out, a_sink, b_w_in, b_vnorm_g,
           b_vnorm_b, b_ws, b_bs, b_w_out, c_w_in, c_w_out, c_lq1, c_lk1, c_lq2, c_lk2, c_subln_g):
    xs = (x.reshape(LAT_ROWS, D_MODEL), ctx.reshape(CTX_ROWS, D_MODEL))
    cvec = jnp.concatenate([c, c_ctx[None, :], jnp.zeros((8 - BATCH - 1, D_MODEL), F32)], axis=0)
    pp = _param_tiles(_ada_mods(cvec, ada_w, ada_b), norm_g)
    cos_t, sin_t = _rope_tables()

    mixer_w = ((a_w_in, a_w_out), (b_w_in, b_w_out), (c_w_in, c_w_out))
    wgu = _convert(_ffn_split_job(ffn_w_in, (0, 0)))
    h = _prenorm(xs, pp, 0)
    for i in range(DEPTH):
        kind, j = i % N_MIXERS, i // N_MIXERS
        ctx_live = i < DEPTH - 1
        rows_out = ROWS if ctx_live else LAT_ROWS
        s0 = 3 * i + 1
        w_mix_in, w_mix_out = mixer_w[kind]

        ffn2_split = _ffn_split_job(ffn_w_in, (i, 1))
        jobs = [_cast_job(ffn_w_out, (i, 0), D_FF_PAD), _cast_job(w_mix_in, (j,))] + ([ffn2_split] if kind != 1 else [])
        a, conv = _ffn_in(h, wgu, jobs)
        xs, h = _mm_out(a, conv[0], xs, pp, s0, rows=ROWS, coef=0.5, want_h=True)
        w_i = conv[1]

        jobs = [_cast_job(w_mix_out, (j,))] + ([ffn2_split] if kind == 1 else [])
        if kind == 0:
            qkv, mix_conv = _qkv_rope(h, w_i, cos_t, sin_t, (A_HEADS + A_KV_HEADS) * HEAD_DIM, jobs, col_major=False)
            o = _win_attn(qkv, a_sink[j], ctx_live)
        elif kind == 1:
            z, mix_conv = _gmlp_in(h, w_i, jobs)
            o = _gmlp_mid(z, b_vnorm_g[j][None, :], b_vnorm_b[j][None, :], b_ws[j], b_bs[j].T)
        else:
            lam_init = 0.8 - 0.6 * math.exp(-0.3 * i)
            qkv, mix_conv = _qkv_rope(h, w_i, cos_t, sin_t, 2 * D_MODEL, jobs, col_major=True)
            o = _diff_attn(qkv, c_lq1[j][None, :], c_lk1[j][None, :], c_lq2[j][None, :], c_lk2[j][None, :],
                           c_subln_g[j][:, None], lam_init)
        w_o = mix_conv[0]
        wgu = mix_conv[1] if kind == 1 else conv[2]
        xs, h = _mm_out(o, w_o, xs, pp, s0 + 1, rows=rows_out, coef=1.0, want_h=True)

        jobs = [_cast_job(ffn_w_out, (i, 1), D_FF_PAD)]
        if i + 1 < DEPTH:
            jobs.append(_ffn_split_job(ffn_w_in, (i + 1, 0)))
        a, conv = _ffn_in(h, wgu, jobs)
        xs, h = _mm_out(a, conv[0], xs, pp, s0 + 2, rows=rows_out, coef=0.5, want_h=i + 1 < DEPTH)
        wgu = conv[1] if i + 1 < DEPTH else None
    return xs.reshape(BATCH, SEQ, D_MODEL)
```

```python
import functools
import math
from typing import Callable, NamedTuple

import jax
import jax.numpy as jnp
from jax import lax
from jax.experimental import pallas as pl
from jax.experimental.pallas import tpu as pltpu

D_MODEL = 2048
BATCH = 2
SEQ = 4096
DEPTH = 4
GRID_W = 64
CTX_LEN = 256
N_MIXERS = 3
N_MOD = 9
NORM_EPS = 1e-6
ROPE_THETA = 10000.0
NEG_INF = -1e30
D_FF = 5504
HEAD_DIM = 128
A_HEADS = 16
A_KV_HEADS = 4
A_GROUP = 4
A_BLOCK = 128
B_CHUNK = 128
B_WIDTH = 3 * D_MODEL
B_GROUPS = 8
B_GROUP_W = B_WIDTH // B_GROUPS
C_HEADS = 8

LAT_ROWS = BATCH * SEQ
CTX_ROWS = BATCH * CTX_LEN
ROWS = LAT_ROWS + CTX_ROWS
N_GROUPS = 3
LANES = 128
BF16_SUBLANES = 16
MXU_WIDTH = 256
IN_ROW_TILES = 8
QKV_ROW_TILES = 4
FF_TILE = 512
D_FF_PAD = -(-D_FF // FF_TILE) * FF_TILE
VMEM_LIMIT = 62 * 1024 * 1024

BF16 = jnp.bfloat16
F32 = jnp.float32

P_POST_G, P_GATE, P_PRE_G, P_SCALE, P_SHIFT = 0, 1, 2, 3, 4


def _params(**kw):
    return pltpu.CompilerParams(vmem_limit_bytes=VMEM_LIMIT, **kw)


def _group_of_block(i, tm):
    return jnp.minimum(i // (SEQ // tm), N_GROUPS - 1)


def _rms(x):
    return x * lax.rsqrt(jnp.mean(x * x, axis=-1, keepdims=True) + NORM_EPS)


def _ada_kernel(c_ref, w_ref, b_ref, o_ref):
    @pl.when(pl.program_id(1) == 0)
    def _():
        o_ref[...] = jnp.broadcast_to(b_ref[...], o_ref.shape)

    c = c_ref[...]
    a = (c * jax.nn.sigmoid(c)).astype(BF16)
    o_ref[...] += jnp.dot(a, w_ref[...].astype(BF16), preferred_element_type=F32)


def _ada_mods(cvec, ada_w, ada_b):
    kb = LANES
    n = N_MOD * D_MODEL
    return pl.pallas_call(
        _ada_kernel,
        grid=(DEPTH, D_MODEL // kb),
        in_specs=[
            pl.BlockSpec((8, kb), lambda l, k: (0, k)),
            pl.BlockSpec((None, kb, n), lambda l, k: (l, k, 0)),
            pl.BlockSpec((None, 1, n), lambda l, k: (l, 0, 0)),
        ],
        out_specs=pl.BlockSpec((None, 8, n), lambda l, k: (l, 0, 0)),
        out_shape=jax.ShapeDtypeStruct((DEPTH, 8, n), F32),
        compiler_params=_params(dimension_semantics=("arbitrary", "arbitrary")),
        name="ada_mods",
    )(cvec, ada_w, ada_b.reshape(DEPTH, 1, n))


def _pre(x, p_ref):
    gain = p_ref[P_PRE_G:P_PRE_G + 1, :] * (1.0 + p_ref[P_SCALE:P_SCALE + 1, :])
    return _rms(x) * gain + p_ref[P_SHIFT:P_SHIFT + 1, :]


def _stream_specs(x, tm):
    if not isinstance(x, tuple):
        return (x,), [pl.BlockSpec((tm, D_MODEL), lambda i: (i, 0))]
    n_lat = LAT_ROWS // tm
    return x, [pl.BlockSpec((tm, D_MODEL), lambda i: (jnp.minimum(i, n_lat - 1), 0)),
               pl.BlockSpec((tm, D_MODEL), lambda i: (jnp.maximum(i - n_lat, 0), 0))]


def _stream_tile(x_refs, rs=slice(None)):
    if len(x_refs) == 1:
        return x_refs[0][rs, :]
    n_lat = LAT_ROWS // x_refs[0].shape[0]
    return jnp.where(pl.program_id(0) < n_lat, x_refs[0][rs, :], x_refs[1][rs, :])


def _prenorm_kernel(*refs):
    *x_refs, p_ref, h_ref = refs
    h_ref[...] = _pre(_stream_tile(x_refs), p_ref).astype(BF16)


def _prenorm(x, pp, s, tm=512):
    x_arrays, x_specs = _stream_specs(x, tm)
    return pl.pallas_call(
        _prenorm_kernel,
        grid=(ROWS // tm,),
        in_specs=x_specs + [
            pl.BlockSpec((None, None, 8, D_MODEL), lambda i: (s, _group_of_block(i, tm), 0, 0)),
        ],
        out_specs=pl.BlockSpec((tm, D_MODEL), lambda i: (i, 0)),
        out_shape=jax.ShapeDtypeStruct((ROWS, D_MODEL), BF16),
        compiler_params=_params(dimension_semantics=("arbitrary",)),
        name="prenorm",
    )(*x_arrays, pp)


class _Job(NamedTuple):
    src: jax.Array
    in_spec: pl.BlockSpec
    out_spec: pl.BlockSpec
    out_shape: jax.ShapeDtypeStruct
    body: Callable
    n_blocks: int


def _row_block(rows, rows_out, steps):
    g = math.gcd(rows, rows_out)
    for rb in range(BF16_SUBLANES, g + 1, BF16_SUBLANES):
        if g % rb == 0 and rows_out // rb <= steps:
            return rb
    raise ValueError(f"no row block for {rows}->{rows_out} rows in {steps} steps")


def _cast_job(src, lead, rows_out=None):
    rows, cols = src.shape[-2:]
    rows_out = rows_out or rows

    def make(steps, lin):
        rb = _row_block(rows, rows_out, steps)
        nb_in, nb_out = rows // rb, rows_out // rb

        def body(src_ref, dst_ref, blk):
            v = src_ref[...].astype(BF16)
            if nb_out > nb_in:
                v = jnp.where(blk < nb_in, v, jnp.zeros_like(v))
            dst_ref[...] = v

        return _Job(
            src,
            pl.BlockSpec((None,) * len(lead) + (rb, cols),
                         lambda *g: lead + (jnp.minimum(lin(*g), nb_in - 1), 0)),
            pl.BlockSpec((rb, cols), lambda *g: (jnp.minimum(lin(*g), nb_out - 1), 0)),
            jax.ShapeDtypeStruct((rows_out, cols), BF16), body, nb_out)

    return make


def _ffn_split_job(ffn_w_in, lead):
    def make(steps, lin):
        rb = _row_block(D_MODEL, D_MODEL, steps)
        nb = D_MODEL // rb

        def body(src_ref, dst_ref, blk):
            del blk
            for part in range(2):
                dst_ref[part, :, :D_FF] = src_ref[:, part * D_FF:(part + 1) * D_FF].astype(BF16)
                dst_ref[part, :, D_FF:] = jnp.zeros((rb, D_FF_PAD - D_FF), BF16)

        return _Job(
            ffn_w_in,
            pl.BlockSpec((None,) * len(lead) + (rb, 2 * D_FF), lambda *g: lead + (jnp.minimum(lin(*g), nb - 1), 0)),
            pl.BlockSpec((2, rb, D_FF_PAD), lambda *g: (0, jnp.minimum(lin(*g), nb - 1), 0)),
            jax.ShapeDtypeStruct((2, D_MODEL, D_FF_PAD), BF16), body, nb)

    return make


def _hosted_call(main, main_args, main_in_specs, out_spec, out_shape, grid, job_makers, name):
    steps = math.prod(grid)
    strides = [math.prod(grid[d + 1:]) for d in range(len(grid))]
    lin = lambda *g: sum(gi * st for gi, st in zip(g, strides))
    jobs = [mk(steps, lin) for mk in job_makers]
    n_in, n_jobs = len(main_args), len(jobs)

    def kern(*refs):
        if main is not None:
            main(*refs[:n_in], refs[n_in + n_jobs])
        t = lin(*[pl.program_id(d) for d in range(len(grid))])
        n_main_out = 0 if main is None else 1
        for q, jb in enumerate(jobs):
            jb.body(refs[n_in + q], refs[n_in + n_jobs + n_main_out + q], jnp.minimum(t, jb.n_blocks - 1))

    main_out = [] if main is None else [(out_spec, out_shape)]
    res = pl.pallas_call(
        kern,
        grid=grid,
        in_specs=list(main_in_specs) + [jb.in_spec for jb in jobs],
        out_specs=[s for s, _ in main_out] + [jb.out_spec for jb in jobs],
        out_shape=[s for _, s in main_out] + [jb.out_shape for jb in jobs],
        compiler_params=_params(dimension_semantics=("arbitrary",) * len(grid)),
        name=name,
    )(*main_args, *[jb.src for jb in jobs])
    return (None, list(res)) if main is None else (res[0], list(res[1:]))


def _convert(job_maker, steps=32):
    return _hosted_call(None, (), (), None, None, (steps,), [job_maker], "convert")[1][0]


def _swiglu_kernel(h_ref, wg_ref, wu_ref, o_ref):
    h = h_ref[...]
    half = o_ref.shape[1] // 2
    parts = []
    for cs in (slice(0, half), slice(half, 2 * half)):
        parts.append((cs, jnp.dot(h, wg_ref[:, cs], preferred_element_type=F32),
                      jnp.dot(h, wu_ref[:, cs], preferred_element_type=F32)))
    for cs, g, u in parts:
        o_ref[:, cs] = (g * jax.nn.sigmoid(g) * u).astype(BF16)


def _ffn_in(h, wgu, jobs, tn=FF_TILE):
    rows = h.shape[0]
    tm = rows // IN_ROW_TILES
    return _hosted_call(
        _swiglu_kernel, (h, wgu, wgu),
        [pl.BlockSpec((tm, D_MODEL), lambda j, i: (i, 0)),
         pl.BlockSpec((None, D_MODEL, tn), lambda j, i: (0, 0, j)),
         pl.BlockSpec((None, D_MODEL, tn), lambda j, i: (1, 0, j))],
        pl.BlockSpec((tm, tn), lambda j, i: (i, j)),
        jax.ShapeDtypeStruct((rows, D_FF_PAD), BF16),
        (D_FF_PAD // tn, rows // tm), jobs, "ffn_in")


def _gelu_tanh(y):
    c0 = math.sqrt(2.0 / math.pi)
    half_y = 0.5 * y
    return half_y + half_y * jnp.tanh(y * (c0 + (c0 * 0.044715) * (y * y)))


def _gelu_kernel(h_ref, w_ref, o_ref):
    half = h_ref.shape[0] // 2
    halves = (slice(0, half), slice(half, 2 * half))
    ys = [jnp.dot(h_ref[rs, :], w_ref[...], preferred_element_type=F32) for rs in halves]
    for rs, y in zip(halves, ys):
        o_ref[rs, :] = _gelu_tanh(y).astype(BF16)


def _gmlp_in(h, w, jobs, tn=2048):
    rows = h.shape[0]
    tm = rows // IN_ROW_TILES
    n = w.shape[1]
    return _hosted_call(
        _gelu_kernel, (h, w),
        [pl.BlockSpec((tm, D_MODEL), lambda j, i: (i, 0)),
         pl.BlockSpec((D_MODEL, tn), lambda j, i: (0, j))],
        pl.BlockSpec((tm, tn), lambda j, i: (i, j)),
        jax.ShapeDtypeStruct((rows, n), BF16),
        (n // tn, rows // tm), jobs, "gmlp_in")


def _qkv_rope_kernel(h_ref, w_ref, cos_ref, sin_ref, swap_ref, o_ref, *, n_rope_tiles, tn, col_major):
    j = pl.program_id(1)

    def store(q0, val):
        if col_major:
            o_ref[q0 // MXU_WIDTH] = val
        else:
            o_ref[:, q0:q0 + MXU_WIDTH] = val

    @pl.when(j < n_rope_tiles)
    def _():
        h = h_ref[...]
        c = cos_ref[...]
        s = sin_ref[...]
        swap = swap_ref[...]
        y = jnp.dot(h, w_ref[...], preferred_element_type=F32)
        hi = y.astype(BF16)
        lo = (y - hi.astype(F32)).astype(BF16)
        for q0 in range(0, tn, MXU_WIDTH):
            cs = slice(q0, q0 + MXU_WIDTH)
            partner = (jnp.dot(hi[:, cs], swap, preferred_element_type=F32)
                       + jnp.dot(lo[:, cs], swap, preferred_element_type=F32))
            store(q0, (y[:, cs] * c + partner * s).astype(BF16))

    @pl.when(j >= n_rope_tiles)
    def _():
        y = jnp.dot(h_ref[...], w_ref[...], preferred_element_type=F32).astype(BF16)
        for q0 in range(0, tn, MXU_WIDTH):
            store(q0, y[:, q0:q0 + MXU_WIDTH])


def _qkv_rope(h, w, cos_t, sin_t, rope_width, jobs, col_major, tn=512):
    rows = h.shape[0]
    tm = rows // QKV_ROW_TILES
    n = w.shape[1]
    kern = functools.partial(_qkv_rope_kernel, n_rope_tiles=rope_width // tn, tn=tn, col_major=col_major)
    if col_major:
        out_spec = pl.BlockSpec((tn // MXU_WIDTH, tm, MXU_WIDTH), lambda i, j: (j, i, 0))
        out_shape = jax.ShapeDtypeStruct((n // MXU_WIDTH, rows, MXU_WIDTH), BF16)
    else:
        out_spec = pl.BlockSpec((tm, tn), lambda i, j: (i, j))
        out_shape = jax.ShapeDtypeStruct((rows, n), BF16)
    lane = jnp.arange(MXU_WIDTH)
    swap = (lane[:, None] == (lane[None, :] ^ 1)).astype(BF16)
    return _hosted_call(
        kern, (h, w, cos_t, sin_t, swap),
        [pl.BlockSpec((tm, D_MODEL), lambda i, j: (i, 0)),
         pl.BlockSpec((D_MODEL, tn), lambda i, j: (0, j)),
         pl.BlockSpec((tm, MXU_WIDTH), lambda i, j: (i, 0)),
         pl.BlockSpec((tm, MXU_WIDTH), lambda i, j: (i, 0)),
         pl.BlockSpec((MXU_WIDTH, MXU_WIDTH), lambda i, j: (0, 0))],
        out_spec, out_shape, (rows // tm, n // tn), jobs, "qkv_rope")


def _mm_out_kernel(a_ref, w_ref, p_ref, *refs, coef, n_x, want_h):
    x_refs, xo_ref, maybe_h_ref = refs[:n_x], refs[n_x], refs[n_x + 1:]
    post = coef * (p_ref[P_GATE:P_GATE + 1, :] * p_ref[P_POST_G:P_POST_G + 1, :])
    half = a_ref.shape[0] // 2
    for rs in (slice(0, half), slice(half, 2 * half)):
        y = jnp.dot(a_ref[rs, :], w_ref[...], preferred_element_type=F32)
        xn = _stream_tile(x_refs, rs) + _rms(y) * post
        xo_ref[rs, :] = xn
        if want_h:
            maybe_h_ref[0][rs, :] = _pre(xn, p_ref).astype(BF16)


def _mm_out(a, w, x, pp, s, *, rows, coef, want_h):
    k = a.shape[1]

    n_x = len(x) if isinstance(x, tuple) else 1

    def vmem_bytes(tm):
        return k * D_MODEL * 2 + 2 * tm * (k * 2 + D_MODEL * (4 * n_x + 4 + 2)) + tm * D_MODEL * 4

    tm = 512 if vmem_bytes(512) <= VMEM_LIMIT - (2 << 20) else 256
    x_arrays, x_specs = _stream_specs(x, tm)
    out_shape = [jax.ShapeDtypeStruct((rows, D_MODEL), F32)]
    out_specs = [pl.BlockSpec((tm, D_MODEL), lambda i: (i, 0))]
    if want_h:
        out_shape.append(jax.ShapeDtypeStruct((rows, D_MODEL), BF16))
        out_specs.append(pl.BlockSpec((tm, D_MODEL), lambda i: (i, 0)))
    res = pl.pallas_call(
        functools.partial(_mm_out_kernel, coef=coef, n_x=len(x_arrays), want_h=want_h),
        grid=(rows // tm,),
        in_specs=[
            pl.BlockSpec((tm, k), lambda i: (i, 0)),
            pl.BlockSpec((k, D_MODEL), lambda i: (0, 0), pipeline_mode=pl.Buffered(1)),
            pl.BlockSpec((None, None, 8, D_MODEL), lambda i: (s, _group_of_block(i, tm), 0, 0)),
        ] + x_specs,
        out_specs=out_specs,
        out_shape=out_shape,
        compiler_params=_params(dimension_semantics=("arbitrary",)),
        name="mm_out",
    )(a, w, pp, *x_arrays)
    return (res[0], res[1]) if want_h else (res[0], None)


def _dot_t(a, b):
    return lax.dot_general(a, b, (((1,), (1,)), ((), ())), preferred_element_type=F32)


def _dot_tn(a, b):
    return lax.dot_general(a, b, (((0,), (0,)), ((), ())), preferred_element_type=F32)


def _win_attn_kernel(sink_ref, q_ref, kv_ref, kvc_ref, o_ref):
    n = pl.program_id(1)
    n_lat = SEQ // A_BLOCK
    kvw = A_KV_HEADS * HEAD_DIM
    scale = HEAD_DIM ** -0.5
    expo = scale * math.log2(math.e)
    gq = A_GROUP * A_BLOCK
    lane = lax.broadcasted_iota(jnp.int32, (1, gq), 1)
    qi = lax.broadcasted_iota(jnp.int32, (A_BLOCK, gq), 1) & (A_BLOCK - 1)
    mi = lax.broadcasted_iota(jnp.int32, (A_BLOCK, gq), 0)

    def heads(kk):
        q4 = jnp.concatenate(
            [q_ref[:, (kk * A_GROUP + g) * HEAD_DIM:(kk * A_GROUP + g + 1) * HEAD_DIM] for g in range(A_GROUP)],
            axis=0)
        sink = jnp.full((1, gq), sink_ref[kk * A_GROUP] / scale, F32)
        for g in range(1, A_GROUP):
            sink = jnp.where(lane >= g * A_BLOCK, sink_ref[kk * A_GROUP + g] / scale, sink)
        return q4, sink, slice(kk * HEAD_DIM, (kk + 1) * HEAD_DIM), slice(kvw + kk * HEAD_DIM, kvw + (kk + 1) * HEAD_DIM)

    def finish(kk, pieces, sink):
        m = sink
        for s, _ in pieces:
            m = jnp.maximum(m, jnp.max(s, axis=0, keepdims=True))
        den = jnp.exp2((sink - m) * expo)
        ot = None
        for s, v in pieces:
            p = jnp.exp2((s - m) * expo)
            den = den + jnp.sum(p, axis=0, keepdims=True)
            pv = _dot_tn(v, p.astype(BF16))
            ot = pv if ot is None else ot + pv
        ot = ot / den
        for g in range(A_GROUP):
            h = kk * A_GROUP + g
            o_ref[:, h * HEAD_DIM:(h + 1) * HEAD_DIM] = ot[:, g * A_BLOCK:(g + 1) * A_BLOCK].T.astype(BF16)

    @pl.when(n < n_lat)
    def _():
        off_l = jnp.where(n > 0, 0, 2 * A_BLOCK)
        off_r = jnp.where(n < n_lat - 1, 0, 2 * A_BLOCK)
        ok_l = mi >= qi + off_l
        ok_r = mi <= qi - off_r
        rows_l, rows_m, rows_r = [pl.ds(pl.multiple_of(jnp.clip(n + d, 0, n_lat - 1) * A_BLOCK, A_BLOCK), A_BLOCK)
                                  for d in (-1, 0, 1)]
        def scores(kk):
            q4, sink, kc, vc = heads(kk)
            s_l = jnp.where(ok_l, _dot_t(kv_ref[rows_l, kc], q4), NEG_INF)
            s_r = jnp.where(ok_r, _dot_t(kv_ref[rows_r, kc], q4), NEG_INF)
            return ([(_dot_t(kvc_ref[:, kc], q4), kvc_ref[:, vc]), (s_l, kv_ref[rows_l, vc]),
                     (_dot_t(kv_ref[rows_m, kc], q4), kv_ref[rows_m, vc]), (s_r, kv_ref[rows_r, vc])], sink)

        nxt = scores(0)
        for kk in range(A_KV_HEADS):
            cur = nxt
            if kk + 1 < A_KV_HEADS:
                nxt = scores(kk + 1)
            finish(kk, *cur)

    @pl.when(n >= n_lat)
    def _():
        for kk in range(A_KV_HEADS):
            q4, sink, kc, vc = heads(kk)
            finish(kk, [(_dot_t(kvc_ref[:, kc], q4), kvc_ref[:, vc])], sink)


def _win_attn(qkv, sink, with_ctx):
    n_lat = SEQ // A_BLOCK
    n_ctx = CTX_LEN // A_BLOCK
    steps = n_lat + (n_ctx if with_ctx else 0)
    qw = A_HEADS * HEAD_DIM
    kvw2 = 2 * A_KV_HEADS * HEAD_DIM
    ctx_blk0 = LAT_ROWS // A_BLOCK

    def qrow(b, n):
        return jnp.where(n < n_lat, b * n_lat + n, ctx_blk0 + b * n_ctx + (n - n_lat))

    rows = LAT_ROWS + (CTX_ROWS if with_ctx else 0)
    return pl.pallas_call(
        _win_attn_kernel,
        grid=(BATCH, steps),
        in_specs=[
            pl.BlockSpec(memory_space=pltpu.SMEM),
            pl.BlockSpec((A_BLOCK, qw), lambda b, n: (qrow(b, n), 0)),
            pl.BlockSpec((SEQ, kvw2), lambda b, n: (b, qw // kvw2)),
            pl.BlockSpec((CTX_LEN, kvw2), lambda b, n: (LAT_ROWS // CTX_LEN + b, qw // kvw2)),
        ],
        out_specs=pl.BlockSpec((A_BLOCK, D_MODEL), lambda b, n: (qrow(b, n), 0)),
        out_shape=jax.ShapeDtypeStruct((rows, D_MODEL), BF16),
        compiler_params=_params(dimension_semantics=("arbitrary", "arbitrary")),
        name="win_attn",
    )(sink, qkv, qkv, qkv)


def _gmlp_mid_kernel(u_ref, v_ref, g_ref, b_ref, ws_ref, bs_ref, o_ref, *, tm):
    v = v_ref[...].astype(F32)
    mu = jnp.mean(v, axis=-1, keepdims=True)
    vc = v - mu
    vn = vc * lax.rsqrt(jnp.mean(vc * vc, axis=-1, keepdims=True) + NORM_EPS)
    vn = (vn * g_ref[...] + b_ref[...]).astype(BF16)
    for g in range(B_GROUPS):
        cs = slice(g * B_GROUP_W, (g + 1) * B_GROUP_W)
        ws = ws_ref[g].astype(BF16)
        for c in range(tm // B_CHUNK):
            rs = slice(c * B_CHUNK, (c + 1) * B_CHUNK)
            mixed = jnp.dot(ws, vn[rs, cs], preferred_element_type=F32) + bs_ref[:, g:g + 1]
            o_ref[rs, cs] = (u_ref[rs, cs].astype(F32) * mixed).astype(BF16)


def _gmlp_mid(z, vn_g, vn_b, ws, bs_t, tm=256):
    rows = z.shape[0]
    return pl.pallas_call(
        functools.partial(_gmlp_mid_kernel, tm=tm),
        grid=(rows // tm,),
        in_specs=[
            pl.BlockSpec((tm, B_WIDTH), lambda i: (i, 0)),
            pl.BlockSpec((tm, B_WIDTH), lambda i: (i, 1)),
            pl.BlockSpec((1, B_WIDTH), lambda i: (0, 0)),
            pl.BlockSpec((1, B_WIDTH), lambda i: (0, 0)),
            pl.BlockSpec((B_GROUPS, B_CHUNK, B_CHUNK), lambda i: (0, 0, 0)),
            pl.BlockSpec((B_CHUNK, B_GROUPS), lambda i: (0, 0)),
        ],
        out_specs=pl.BlockSpec((tm, B_WIDTH), lambda i: (i, 0)),
        out_shape=jax.ShapeDtypeStruct((rows, B_WIDTH), BF16),
        compiler_params=_params(dimension_semantics=("arbitrary",)),
        name="gmlp_mid",
    )(z, z, vn_g, vn_b, ws, bs_t)


def _diff_attn_kernel(lq1_ref, lk1_ref, lq2_ref, lk2_ref, sg_ref, q_ref, k_ref, v_ref, kc_ref, vc_ref, o_ref, vt_ref,
                      *, lam_init, tq, tk):
    qi = pl.program_id(2)
    n_lat = SEQ // tq
    expo = (HEAD_DIM ** -0.5) * math.log2(math.e)
    lam = (jnp.exp(jnp.sum(lq1_ref[...] * lk1_ref[...], axis=-1, keepdims=True))
           - jnp.exp(jnp.sum(lq2_ref[...] * lk2_ref[...], axis=-1, keepdims=True)) + lam_init)

    @pl.when(qi == 0)
    def _():
        vt_ref[:, :CTX_LEN] = vc_ref[...].T
        for t in range(SEQ // tk):
            vt_ref[:, CTX_LEN + t * tk:CTX_LEN + (t + 1) * tk] = v_ref[t * tk:(t + 1) * tk, :].T

    def run(chunks):
        qs = [q_ref[:, c * HEAD_DIM:(c + 1) * HEAD_DIM] for c in range(2)]

        def scores(chunk):
            kref, lo, size, _ = chunk
            return [_dot_t(kref[lo:lo + size, c * HEAD_DIM:(c + 1) * HEAD_DIM], qs[c]) for c in range(2)]

        m = [None, None]
        l = [None, None]
        acc = [None, None]
        s_next = scores(chunks[0])
        for t, (_, _, size, vlo) in enumerate(chunks):
            s_cur = s_next
            if t + 1 < len(chunks):
                s_next = scores(chunks[t + 1])
            for c in range(2):
                st = s_cur[c]
                m_new = jnp.max(st, axis=0, keepdims=True)
                if m[c] is not None:
                    m_new = jnp.maximum(m[c], m_new)
                p = jnp.exp2((st - m_new) * expo)
                pv = jnp.dot(vt_ref[:, vlo:vlo + size], p.astype(BF16), preferred_element_type=F32)
                if m[c] is None:
                    l[c] = jnp.sum(p, axis=0, keepdims=True)
                    acc[c] = pv
                else:
                    alpha = jnp.exp2((m[c] - m_new) * expo)
                    l[c] = alpha * l[c] + jnp.sum(p, axis=0, keepdims=True)
                    acc[c] = alpha * acc[c] + pv
                m[c] = m_new
        o = acc[0] / l[0] - lam * (acc[1] / l[1])
        o = o * lax.rsqrt(jnp.mean(o * o, axis=0, keepdims=True) + NORM_EPS) * (sg_ref[...] * (1.0 - lam_init))
        o_ref[...] = o.T.astype(BF16)

    @pl.when(qi < n_lat)
    def _():
        run([(kc_ref, 0, CTX_LEN, 0)] + [(k_ref, t * tk, tk, CTX_LEN + t * tk) for t in range(SEQ // tk)])

    @pl.when(qi >= n_lat)
    def _():
        run([(kc_ref, 0, CTX_LEN, 0)])


def _diff_attn(qkv, lq1, lk1, lq2, lk2, subln_g_col, lam_init, tq=256, tk=1024):
    hw = 2 * HEAD_DIM
    n_lat = SEQ // tq
    n_ctx = CTX_LEN // tq
    kcol = D_MODEL // hw
    ctx_blk0 = LAT_ROWS // tq
    ctx_blk = LAT_ROWS // CTX_LEN

    def qrow(b, qi):
        return jnp.where(qi < n_lat, b * n_lat + qi, ctx_blk0 + b * n_ctx + (qi - n_lat))

    vec = pl.BlockSpec((1, HEAD_DIM), lambda b, h, qi: (0, 0))
    kern = functools.partial(_diff_attn_kernel, lam_init=lam_init, tq=tq, tk=tk)
    return pl.pallas_call(
        kern,
        grid=(BATCH, C_HEADS, n_lat + n_ctx),
        in_specs=[
            vec, vec, vec, vec,
            pl.BlockSpec((hw, 1), lambda b, h, qi: (0, 0)),
            pl.BlockSpec((None, tq, hw), lambda b, h, qi: (h, qrow(b, qi), 0)),
            pl.BlockSpec((None, SEQ, hw), lambda b, h, qi: (kcol + h, b, 0)),
            pl.BlockSpec((None, SEQ, hw), lambda b, h, qi: (2 * kcol + h, b, 0)),
            pl.BlockSpec((None, CTX_LEN, hw), lambda b, h, qi: (kcol + h, ctx_blk + b, 0)),
            pl.BlockSpec((None, CTX_LEN, hw), lambda b, h, qi: (2 * kcol + h, ctx_blk + b, 0)),
        ],
        out_specs=pl.BlockSpec((tq, hw), lambda b, h, qi: (qrow(b, qi), h)),
        out_shape=jax.ShapeDtypeStruct((ROWS, D_MODEL), BF16),
        scratch_shapes=[pltpu.VMEM((hw, CTX_LEN + SEQ), BF16)],
        compiler_params=_params(dimension_semantics=("arbitrary", "arbitrary", "arbitrary")),
        name="diff_attn",
    )(lq1, lk1, lq2, lk2, subln_g_col, qkv, qkv, qkv, qkv, qkv)


def _rope_tables():
    rows = SEQ // GRID_W
    row = jnp.repeat(jnp.arange(rows, dtype=F32), GRID_W)
    col = jnp.tile(jnp.arange(GRID_W, dtype=F32), rows)
    axis_dim = HEAD_DIM // 2
    inv = ROPE_THETA ** (-jnp.arange(0, axis_dim, 2, dtype=F32) / axis_dim)
    ang = jnp.concatenate([row[:, None] * inv, col[:, None] * inv], axis=-1)
    cos = jnp.repeat(jnp.cos(ang), 2, axis=-1)
    sin = jnp.repeat(jnp.sin(ang), 2, axis=-1) * jnp.tile(jnp.array([-1.0, 1.0], F32), HEAD_DIM // 2)
    heads_per_chunk = MXU_WIDTH // HEAD_DIM
    cos_t = jnp.concatenate([jnp.tile(cos, (BATCH, heads_per_chunk)), jnp.ones((CTX_ROWS, MXU_WIDTH), F32)], axis=0)
    sin_t = jnp.concatenate([jnp.tile(sin, (BATCH, heads_per_chunk)), jnp.zeros((CTX_ROWS, MXU_WIDTH), F32)], axis=0)
    return cos_t, sin_t


def _param_tiles(mods, norm_g):
    m = mods[:, :N_GROUPS].reshape(DEPTH, N_GROUPS, N_MOD, D_MODEL)
    zero = jnp.zeros((N_GROUPS, D_MODEL), F32)
    bcast = lambda v: jnp.broadcast_to(v, (N_GROUPS, D_MODEL))
    tiles = []
    for s in range(-1, 3 * DEPTH):
        rows = [zero] * 8
        if s >= 0:
            i, slot = divmod(s, 3)
            rows[P_POST_G] = bcast(norm_g[i, 2 * slot + 1])
            rows[P_GATE] = m[i, :, 3 * slot + 2]
        if s + 1 < 3 * DEPTH:
            i, slot = divmod(s + 1, 3)
            rows[P_PRE_G] = bcast(norm_g[i, 2 * slot])
            rows[P_SCALE] = m[i, :, 3 * slot + 1]
            rows[P_SHIFT] = m[i, :, 3 * slot]
        tiles.append(jnp.stack(rows, axis=1))
    return jnp.stack(tiles, axis=0)


def kernel(x, c, ctx, c_ctx, ada_w, ada_b, norm_g, ffn_w_in, ffn_w_out, a_w_in, a_w_out, a_sink, b_w_in, b_vnorm_g,
           b_vnorm_b, b_ws, b_bs, b_w_out, c_w_in, c_w_out, c_lq1, c_lk1, c_lq2, c_lk2, c_subln_g):
    xs = (x.reshape(LAT_ROWS, D_MODEL), ctx.reshape(CTX_ROWS, D_MODEL))
    cvec = jnp.concatenate([c, c_ctx[None, :], jnp.zeros((8 - BATCH - 1, D_MODEL), F32)], axis=0)
    pp = _param_tiles(_ada_mods(cvec, ada_w, ada_b), norm_g)
    cos_t, sin_t = _rope_tables()

    mixer_w = ((a_w_in, a_w_out), (b_w_in, b_w_out), (c_w_in, c_w_out))
    wgu = _convert(_ffn_split_job(ffn_w_in, (0, 0)))
    h = _prenorm(xs, pp, 0)
    for i in range(DEPTH):
        kind, j = i % N_MIXERS, i // N_MIXERS
        ctx_live = i < DEPTH - 1
        rows_out = ROWS if ctx_live else LAT_ROWS
        s0 = 3 * i + 1
        w_mix_in, w_mix_out = mixer_w[kind]

        ffn2_split = _ffn_split_job(ffn_w_in, (i, 1))
        jobs = [_cast_job(ffn_w_out, (i, 0), D_FF_PAD), _cast_job(w_mix_in, (j,))] + ([ffn2_split] if kind != 1 else [])
        a, conv = _ffn_in(h, wgu, jobs)
        xs, h = _mm_out(a, conv[0], xs, pp, s0, rows=ROWS, coef=0.5, want_h=True)
        w_i = conv[1]

        jobs = [_cast_job(w_mix_out, (j,))] + ([ffn2_split] if kind == 1 else [])
        if kind == 0:
            qkv, mix_conv = _qkv_rope(h, w_i, cos_t, sin_t, (A_HEADS + A_KV_HEADS) * HEAD_DIM, jobs, col_major=False)
            o = _win_attn(qkv, a_sink[j], ctx_live)
        elif kind == 1:
            z, mix_conv = _gmlp_in(h, w_i, jobs)
            o = _gmlp_mid(z, b_vnorm_g[j][None, :], b_vnorm_b[j][None, :], b_ws[j], b_bs[j].T)
        else:
            lam_init = 0.8 - 0.6 * math.exp(-0.3 * i)
            qkv, mix_conv = _qkv_rope(h, w_i, cos_t, sin_t, 2 * D_MODEL, jobs, col_major=True)
            o = _diff_attn(qkv, c_lq1[j][None, :], c_lk1[j][None, :], c_lq2[j][None, :], c_lk2[j][None, :],
                           c_subln_g[j][:, None], lam_init)
        w_o = mix_conv[0]
        wgu = mix_conv[1] if kind == 1 else conv[2]
        xs, h = _mm_out(o, w_o, xs, pp, s0 + 1, rows=rows_out, coef=1.0, want_h=True)

        jobs = [_cast_job(ffn_w_out, (i, 1), D_FF_PAD)]
        if i + 1 < DEPTH:
            jobs.append(_ffn_split_job(ffn_w_in, (i + 1, 0)))
        a, conv = _ffn_in(h, wgu, jobs)
        xs, h = _mm_out(a, conv[0], xs, pp, s0 + 2, rows=rows_out, coef=0.5, want_h=i + 1 < DEPTH)
        wgu = conv[1] if i + 1 < DEPTH else None
    return xs.reshape(BATCH, SEQ, D_MODEL)
```

```python
import functools
import math
from typing import Callable, NamedTuple

import jax
import jax.numpy as jnp
from jax import lax
from jax.experimental import pallas as pl
from jax.experimental.pallas import tpu as pltpu

D_MODEL = 2048
BATCH = 2
SEQ = 4096
DEPTH = 4
GRID_W = 64
CTX_LEN = 256
N_MIXERS = 3
N_MOD = 9
NORM_EPS = 1e-6
ROPE_THETA = 10000.0
NEG_INF = -1e30
D_FF = 5504
HEAD_DIM = 128
A_HEADS = 16
A_KV_HEADS = 4
A_GROUP = 4
A_BLOCK = 128
B_CHUNK = 128
B_WIDTH = 3 * D_MODEL
B_GROUPS = 8
B_GROUP_W = B_WIDTH // B_GROUPS
C_HEADS = 8

LAT_ROWS = BATCH * SEQ
CTX_ROWS = BATCH * CTX_LEN
ROWS = LAT_ROWS + CTX_ROWS
N_GROUPS = 3
LANES = 128
BF16_SUBLANES = 16
MXU_WIDTH = 256
IN_ROW_TILES = 4
GMLP_ROW_TILES = 8
FF_TILE = 512
D_FF_PAD = -(-D_FF // FF_TILE) * FF_TILE
VMEM_LIMIT = 62 * 1024 * 1024

BF16 = jnp.bfloat16
F32 = jnp.float32

P_POST_G, P_GATE, P_PRE_G, P_SCALE, P_SHIFT = 0, 1, 2, 3, 4


def _params(**kw):
    return pltpu.CompilerParams(vmem_limit_bytes=VMEM_LIMIT, **kw)


def _group_of_block(i, tm):
    return jnp.minimum(i // (SEQ // tm), N_GROUPS - 1)


def _rms(x):
    return x * lax.rsqrt(jnp.mean(x * x, axis=-1, keepdims=True) + NORM_EPS)


def _ada_kernel(c_ref, w_ref, b_ref, o_ref):
    @pl.when(pl.program_id(1) == 0)
    def _():
        o_ref[...] = jnp.broadcast_to(b_ref[...], o_ref.shape)

    c = c_ref[...]
    a = (c * jax.nn.sigmoid(c)).astype(BF16)
    o_ref[...] += jnp.dot(a, w_ref[...].astype(BF16), preferred_element_type=F32)


def _ada_mods(cvec, ada_w, ada_b):
    kb = LANES
    n = N_MOD * D_MODEL
    return pl.pallas_call(
        _ada_kernel,
        grid=(DEPTH, D_MODEL // kb),
        in_specs=[
            pl.BlockSpec((8, kb), lambda l, k: (0, k)),
            pl.BlockSpec((None, kb, n), lambda l, k: (l, k, 0)),
            pl.BlockSpec((None, 1, n), lambda l, k: (l, 0, 0)),
        ],
        out_specs=pl.BlockSpec((None, 8, n), lambda l, k: (l, 0, 0)),
        out_shape=jax.ShapeDtypeStruct((DEPTH, 8, n), F32),
        compiler_params=_params(dimension_semantics=("arbitrary", "arbitrary")),
        name="ada_mods",
    )(cvec, ada_w, ada_b.reshape(DEPTH, 1, n))


def _pre(x, p_ref):
    gain = p_ref[P_PRE_G:P_PRE_G + 1, :] * (1.0 + p_ref[P_SCALE:P_SCALE + 1, :])
    return _rms(x) * gain + p_ref[P_SHIFT:P_SHIFT + 1, :]


def _stream_specs(x, tm):
    if not isinstance(x, tuple):
        return (x,), [pl.BlockSpec((tm, D_MODEL), lambda i: (i, 0))]
    n_lat = LAT_ROWS // tm
    return x, [pl.BlockSpec((tm, D_MODEL), lambda i: (jnp.minimum(i, n_lat - 1), 0)),
               pl.BlockSpec((tm, D_MODEL), lambda i: (jnp.maximum(i - n_lat, 0), 0))]


def _stream_tile(x_refs, rs=slice(None)):
    if len(x_refs) == 1:
        return x_refs[0][rs, :]
    n_lat = LAT_ROWS // x_refs[0].shape[0]
    return jnp.where(pl.program_id(0) < n_lat, x_refs[0][rs, :], x_refs[1][rs, :])


def _prenorm_kernel(*refs):
    *x_refs, p_ref, h_ref = refs
    h_ref[...] = _pre(_stream_tile(x_refs), p_ref).astype(BF16)


def _prenorm(x, pp, s, tm=512):
    x_arrays, x_specs = _stream_specs(x, tm)
    return pl.pallas_call(
        _prenorm_kernel,
        grid=(ROWS // tm,),
        in_specs=x_specs + [
            pl.BlockSpec((None, None, 8, D_MODEL), lambda i: (s, _group_of_block(i, tm), 0, 0)),
        ],
        out_specs=pl.BlockSpec((tm, D_MODEL), lambda i: (i, 0)),
        out_shape=jax.ShapeDtypeStruct((ROWS, D_MODEL), BF16),
        compiler_params=_params(dimension_semantics=("arbitrary",)),
        name="prenorm",
    )(*x_arrays, pp)


class _Job(NamedTuple):
    src: jax.Array
    in_spec: pl.BlockSpec
    out_spec: pl.BlockSpec
    out_shape: jax.ShapeDtypeStruct
    body: Callable
    n_blocks: int


def _row_block(rows, rows_out, steps):
    g = math.gcd(rows, rows_out)
    for rb in range(BF16_SUBLANES, g + 1, BF16_SUBLANES):
        if g % rb == 0 and rows_out // rb <= steps:
            return rb
    raise ValueError(f"no row block for {rows}->{rows_out} rows in {steps} steps")


def _cast_job(src, lead, rows_out=None):
    rows, cols = src.shape[-2:]
    rows_out = rows_out or rows

    def make(steps, lin):
        rb = _row_block(rows, rows_out, steps)
        nb_in, nb_out = rows // rb, rows_out // rb

        def body(src_ref, dst_ref, blk):
            v = src_ref[...].astype(BF16)
            if nb_out > nb_in:
                v = jnp.where(blk < nb_in, v, jnp.zeros_like(v))
            dst_ref[...] = v

        return _Job(
            src,
            pl.BlockSpec((None,) * len(lead) + (rb, cols),
                         lambda *g: lead + (jnp.minimum(lin(*g), nb_in - 1), 0)),
            pl.BlockSpec((rb, cols), lambda *g: (jnp.minimum(lin(*g), nb_out - 1), 0)),
            jax.ShapeDtypeStruct((rows_out, cols), BF16), body, nb_out)

    return make


def _ffn_split_job(ffn_w_in, lead):
    def make(steps, lin):
        rb = _row_block(D_MODEL, D_MODEL, steps)
        nb = D_MODEL // rb

        def body(src_ref, dst_ref, blk):
            del blk
            for part in range(2):
                dst_ref[part, :, :D_FF] = src_ref[:, part * D_FF:(part + 1) * D_FF].astype(BF16)
                dst_ref[part, :, D_FF:] = jnp.zeros((rb, D_FF_PAD - D_FF), BF16)

        return _Job(
            ffn_w_in,
            pl.BlockSpec((None,) * len(lead) + (rb, 2 * D_FF), lambda *g: lead + (jnp.minimum(lin(*g), nb - 1), 0)),
            pl.BlockSpec((2, rb, D_FF_PAD), lambda *g: (0, jnp.minimum(lin(*g), nb - 1), 0)),
            jax.ShapeDtypeStruct((2, D_MODEL, D_FF_PAD), BF16), body, nb)

    return make


def _hosted_call(main, main_args, main_in_specs, out_spec, out_shape, grid, job_makers, name):
    steps = math.prod(grid)
    strides = [math.prod(grid[d + 1:]) for d in range(len(grid))]
    lin = lambda *g: sum(gi * st for gi, st in zip(g, strides))
    jobs = [mk(steps, lin) for mk in job_makers]
    n_in, n_jobs = len(main_args), len(jobs)

    def kern(*refs):
        if main is not None:
            main(*refs[:n_in], refs[n_in + n_jobs])
        t = lin(*[pl.program_id(d) for d in range(len(grid))])
        n_main_out = 0 if main is None else 1
        for q, jb in enumerate(jobs):
            jb.body(refs[n_in + q], refs[n_in + n_jobs + n_main_out + q], jnp.minimum(t, jb.n_blocks - 1))

    main_out = [] if main is None else [(out_spec, out_shape)]
    res = pl.pallas_call(
        kern,
        grid=grid,
        in_specs=list(main_in_specs) + [jb.in_spec for jb in jobs],
        out_specs=[s for s, _ in main_out] + [jb.out_spec for jb in jobs],
        out_shape=[s for _, s in main_out] + [jb.out_shape for jb in jobs],
        compiler_params=_params(dimension_semantics=("arbitrary",) * len(grid)),
        name=name,
    )(*main_args, *[jb.src for jb in jobs])
    return (None, list(res)) if main is None else (res[0], list(res[1:]))


def _convert(job_maker, steps=32):
    return _hosted_call(None, (), (), None, None, (steps,), [job_maker], "convert")[1][0]


def _swiglu_kernel(h_ref, wg_ref, wu_ref, o_ref):
    h = h_ref[...]
    half = o_ref.shape[1] // 2
    parts = []
    for cs in (slice(0, half), slice(half, 2 * half)):
        parts.append((cs, jnp.dot(h, wg_ref[:, cs], preferred_element_type=F32),
                      jnp.dot(h, wu_ref[:, cs], preferred_element_type=F32)))
    for cs, g, u in parts:
        o_ref[:, cs] = (g * jax.nn.sigmoid(g) * u).astype(BF16)


def _ffn_in(h, wgu, jobs, tn=FF_TILE):
    rows = h.shape[0]
    tm = rows // IN_ROW_TILES
    return _hosted_call(
        _swiglu_kernel, (h, wgu, wgu),
        [pl.BlockSpec((tm, D_MODEL), lambda j, i: (i, 0)),
         pl.BlockSpec((None, D_MODEL, tn), lambda j, i: (0, 0, j)),
         pl.BlockSpec((None, D_MODEL, tn), lambda j, i: (1, 0, j))],
        pl.BlockSpec((tm, tn), lambda j, i: (i, j)),
        jax.ShapeDtypeStruct((rows, D_FF_PAD), BF16),
        (D_FF_PAD // tn, rows // tm), jobs, "ffn_in")


def _gelu_tanh(y):
    c0 = math.sqrt(2.0 / math.pi)
    half_y = 0.5 * y
    return half_y + half_y * jnp.tanh(y * (c0 + (c0 * 0.044715) * (y * y)))


def _gelu_kernel(h_ref, w_ref, o_ref):
    half = h_ref.shape[0] // 2
    halves = (slice(0, half), slice(half, 2 * half))
    ys = [jnp.dot(h_ref[rs, :], w_ref[...], preferred_element_type=F32) for rs in halves]
    for rs, y in zip(halves, ys):
        o_ref[rs, :] = _gelu_tanh(y).astype(BF16)


def _gmlp_in(h, w, jobs, tn=2048):
    rows = h.shape[0]
    tm = rows // GMLP_ROW_TILES
    n = w.shape[1]
    return _hosted_call(
        _gelu_kernel, (h, w),
        [pl.BlockSpec((tm, D_MODEL), lambda j, i: (i, 0)),
         pl.BlockSpec((D_MODEL, tn), lambda j, i: (0, j))],
        pl.BlockSpec((tm, tn), lambda j, i: (i, j)),
        jax.ShapeDtypeStruct((rows, n), BF16),
        (n // tn, rows // tm), jobs, "gmlp_in")


def _qkv_rope_kernel(h_ref, w_ref, cos_ref, sin_ref, swap_ref, o_ref, *, n_rope_tiles, tn, col_major):
    j = pl.program_id(1)

    def store(q0, val):
        if col_major:
            o_ref[q0 // MXU_WIDTH] = val
        else:
            o_ref[:, q0:q0 + MXU_WIDTH] = val

    @pl.when(j < n_rope_tiles)
    def _():
        h = h_ref[...]
        c = cos_ref[...]
        s = sin_ref[...]
        swap = swap_ref[...]
        y = jnp.dot(h, w_ref[...], preferred_element_type=F32)
        hi = y.astype(BF16)
        lo = (y - hi.astype(F32)).astype(BF16)
        for q0 in range(0, tn, MXU_WIDTH):
            cs = slice(q0, q0 + MXU_WIDTH)
            partner = (jnp.dot(hi[:, cs], swap, preferred_element_type=F32)
                       + jnp.dot(lo[:, cs], swap, preferred_element_type=F32))
            store(q0, (y[:, cs] * c + partner * s).astype(BF16))

    @pl.when(j >= n_rope_tiles)
    def _():
        y = jnp.dot(h_ref[...], w_ref[...], preferred_element_type=F32).astype(BF16)
        for q0 in range(0, tn, MXU_WIDTH):
            store(q0, y[:, q0:q0 + MXU_WIDTH])


def _qkv_rope(h, w, cos_t, sin_t, rope_width, jobs, col_major, tn=512):
    rows = h.shape[0]
    tm = rows // IN_ROW_TILES
    n = w.shape[1]
    kern = functools.partial(_qkv_rope_kernel, n_rope_tiles=rope_width // tn, tn=tn, col_major=col_major)
    if col_major:
        out_spec = pl.BlockSpec((tn // MXU_WIDTH, tm, MXU_WIDTH), lambda i, j: (j, i, 0))
        out_shape = jax.ShapeDtypeStruct((n // MXU_WIDTH, rows, MXU_WIDTH), BF16)
    else:
        out_spec = pl.BlockSpec((tm, tn), lambda i, j: (i, j))
        out_shape = jax.ShapeDtypeStruct((rows, n), BF16)
    lane = jnp.arange(MXU_WIDTH)
    swap = (lane[:, None] == (lane[None, :] ^ 1)).astype(BF16)
    return _hosted_call(
        kern, (h, w, cos_t, sin_t, swap),
        [pl.BlockSpec((tm, D_MODEL), lambda i, j: (i, 0)),
         pl.BlockSpec((D_MODEL, tn), lambda i, j: (0, j)),
         pl.BlockSpec((tm, MXU_WIDTH), lambda i, j: (i, 0)),
         pl.BlockSpec((tm, MXU_WIDTH), lambda i, j: (i, 0)),
         pl.BlockSpec((MXU_WIDTH, MXU_WIDTH), lambda i, j: (0, 0))],
        out_spec, out_shape, (rows // tm, n // tn), jobs, "qkv_rope")


def _mm_out_kernel(a_ref, w_ref, p_ref, *refs, coef, n_x, want_h):
    x_refs, xo_ref, maybe_h_ref = refs[:n_x], refs[n_x], refs[n_x + 1:]
    post = coef * (p_ref[P_GATE:P_GATE + 1, :] * p_ref[P_POST_G:P_POST_G + 1, :])
    half = a_ref.shape[0] // 2
    for rs in (slice(0, half), slice(half, 2 * half)):
        y = jnp.dot(a_ref[rs, :], w_ref[...], preferred_element_type=F32)
        xn = _stream_tile(x_refs, rs) + _rms(y) * post
        xo_ref[rs, :] = xn
        if want_h:
            maybe_h_ref[0][rs, :] = _pre(xn, p_ref).astype(BF16)


def _mm_out(a, w, x, pp, s, *, rows, coef, want_h):
    k = a.shape[1]

    n_x = len(x) if isinstance(x, tuple) else 1

    def vmem_bytes(tm):
        return k * D_MODEL * 2 + 2 * tm * (k * 2 + D_MODEL * (4 * n_x + 4 + 2)) + tm * D_MODEL * 4

    tm = 512 if vmem_bytes(512) <= VMEM_LIMIT - (2 << 20) else 256
    x_arrays, x_specs = _stream_specs(x, tm)
    out_shape = [jax.ShapeDtypeStruct((rows, D_MODEL), F32)]
    out_specs = [pl.BlockSpec((tm, D_MODEL), lambda i: (i, 0))]
    if want_h:
        out_shape.append(jax.ShapeDtypeStruct((rows, D_MODEL), BF16))
        out_specs.append(pl.BlockSpec((tm, D_MODEL), lambda i: (i, 0)))
    res = pl.pallas_call(
        functools.partial(_mm_out_kernel, coef=coef, n_x=len(x_arrays), want_h=want_h),
        grid=(rows // tm,),
        in_specs=[
            pl.BlockSpec((tm, k), lambda i: (i, 0)),
            pl.BlockSpec((k, D_MODEL), lambda i: (0, 0), pipeline_mode=pl.Buffered(1)),
            pl.BlockSpec((None, None, 8, D_MODEL), lambda i: (s, _group_of_block(i, tm), 0, 0)),
        ] + x_specs,
        out_specs=out_specs,
        out_shape=out_shape,
        compiler_params=_params(dimension_semantics=("arbitrary",)),
        name="mm_out",
    )(a, w, pp, *x_arrays)
    return (res[0], res[1]) if want_h else (res[0], None)


def _dot_t(a, b):
    return lax.dot_general(a, b, (((1,), (1,)), ((), ())), preferred_element_type=F32)


def _dot_tn(a, b):
    return lax.dot_general(a, b, (((0,), (0,)), ((), ())), preferred_element_type=F32)


def _win_attn_kernel(sink_ref, q_ref, kv_ref, kvc_ref, o_ref):
    n = pl.program_id(1)
    n_lat = SEQ // A_BLOCK
    kvw = A_KV_HEADS * HEAD_DIM
    scale = HEAD_DIM ** -0.5
    expo = scale * math.log2(math.e)
    gq = A_GROUP * A_BLOCK
    lane = lax.broadcasted_iota(jnp.int32, (1, gq), 1)
    qi = lax.broadcasted_iota(jnp.int32, (A_BLOCK, gq), 1) & (A_BLOCK - 1)
    mi = lax.broadcasted_iota(jnp.int32, (A_BLOCK, gq), 0)

    def heads(kk):
        q4 = jnp.concatenate(
            [q_ref[:, (kk * A_GROUP + g) * HEAD_DIM:(kk * A_GROUP + g + 1) * HEAD_DIM] for g in range(A_GROUP)],
            axis=0)
        sink = jnp.full((1, gq), sink_ref[kk * A_GROUP] / scale, F32)
        for g in range(1, A_GROUP):
            sink = jnp.where(lane >= g * A_BLOCK, sink_ref[kk * A_GROUP + g] / scale, sink)
        return q4, sink, slice(kk * HEAD_DIM, (kk + 1) * HEAD_DIM), slice(kvw + kk * HEAD_DIM, kvw + (kk + 1) * HEAD_DIM)

    def finish(kk, pieces, sink):
        m = sink
        for s, _ in pieces:
            m = jnp.maximum(m, jnp.max(s, axis=0, keepdims=True))
        den = jnp.exp2((sink - m) * expo)
        ot = None
        for s, v in pieces:
            p = jnp.exp2((s - m) * expo)
            den = den + jnp.sum(p, axis=0, keepdims=True)
            pv = _dot_tn(v, p.astype(BF16))
            ot = pv if ot is None else ot + pv
        ot = ot / den
        for g in range(A_GROUP):
            h = kk * A_GROUP + g
            o_ref[:, h * HEAD_DIM:(h + 1) * HEAD_DIM] = ot[:, g * A_BLOCK:(g + 1) * A_BLOCK].T.astype(BF16)

    @pl.when(n < n_lat)
    def _():
        off_l = jnp.where(n > 0, 0, 2 * A_BLOCK)
        off_r = jnp.where(n < n_lat - 1, 0, 2 * A_BLOCK)
        ok_l = mi >= qi + off_l
        ok_r = mi <= qi - off_r
        rows_l, rows_m, rows_r = [pl.ds(pl.multiple_of(jnp.clip(n + d, 0, n_lat - 1) * A_BLOCK, A_BLOCK), A_BLOCK)
                                  for d in (-1, 0, 1)]
        def scores(kk):
            q4, sink, kc, vc = heads(kk)
            s_l = jnp.where(ok_l, _dot_t(kv_ref[rows_l, kc], q4), NEG_INF)
            s_r = jnp.where(ok_r, _dot_t(kv_ref[rows_r, kc], q4), NEG_INF)
            return ([(_dot_t(kvc_ref[:, kc], q4), kvc_ref[:, vc]), (s_l, kv_ref[rows_l, vc]),
                     (_dot_t(kv_ref[rows_m, kc], q4), kv_ref[rows_m, vc]), (s_r, kv_ref[rows_r, vc])], sink)

        nxt = scores(0)
        for kk in range(A_KV_HEADS):
            cur = nxt
            if kk + 1 < A_KV_HEADS:
                nxt = scores(kk + 1)
            finish(kk, *cur)

    @pl.when(n >= n_lat)
    def _():
        for kk in range(A_KV_HEADS):
            q4, sink, kc, vc = heads(kk)
            finish(kk, [(_dot_t(kvc_ref[:, kc], q4), kvc_ref[:, vc])], sink)


def _win_attn(qkv, sink, with_ctx):
    n_lat = SEQ // A_BLOCK
    n_ctx = CTX_LEN // A_BLOCK
    steps = n_lat + (n_ctx if with_ctx else 0)
    qw = A_HEADS * HEAD_DIM
    kvw2 = 2 * A_KV_HEADS * HEAD_DIM
    ctx_blk0 = LAT_ROWS // A_BLOCK

    def qrow(b, n):
        return jnp.where(n < n_lat, b * n_lat + n, ctx_blk0 + b * n_ctx + (n - n_lat))

    rows = LAT_ROWS + (CTX_ROWS if with_ctx else 0)
    return pl.pallas_call(
        _win_attn_kernel,
        grid=(BATCH, steps),
        in_specs=[
            pl.BlockSpec(memory_space=pltpu.SMEM),
            pl.BlockSpec((A_BLOCK, qw), lambda b, n: (qrow(b, n), 0)),
            pl.BlockSpec((SEQ, kvw2), lambda b, n: (b, qw // kvw2)),
            pl.BlockSpec((CTX_LEN, kvw2), lambda b, n: (LAT_ROWS // CTX_LEN + b, qw // kvw2)),
        ],
        out_specs=pl.BlockSpec((A_BLOCK, D_MODEL), lambda b, n: (qrow(b, n), 0)),
        out_shape=jax.ShapeDtypeStruct((rows, D_MODEL), BF16),
        compiler_params=_params(dimension_semantics=("arbitrary", "arbitrary")),
        name="win_attn",
    )(sink, qkv, qkv, qkv)


def _gmlp_mid_kernel(u_ref, v_ref, g_ref, b_ref, ws_ref, bs_ref, o_ref, *, tm):
    v = v_ref[...].astype(F32)
    mu = jnp.mean(v, axis=-1, keepdims=True)
    vc = v - mu
    vn = vc * lax.rsqrt(jnp.mean(vc * vc, axis=-1, keepdims=True) + NORM_EPS)
    vn = (vn * g_ref[...] + b_ref[...]).astype(BF16)
    for g in range(B_GROUPS):
        cs = slice(g * B_GROUP_W, (g + 1) * B_GROUP_W)
        ws = ws_ref[g].astype(BF16)
        for c in range(tm // B_CHUNK):
            rs = slice(c * B_CHUNK, (c + 1) * B_CHUNK)
            mixed = jnp.dot(ws, vn[rs, cs], preferred_element_type=F32) + bs_ref[:, g:g + 1]
            o_ref[rs, cs] = (u_ref[rs, cs].astype(F32) * mixed).astype(BF16)


def _gmlp_mid(z, vn_g, vn_b, ws, bs_t, tm=256):
    rows = z.shape[0]
    return pl.pallas_call(
        functools.partial(_gmlp_mid_kernel, tm=tm),
        grid=(rows // tm,),
        in_specs=[
            pl.BlockSpec((tm, B_WIDTH), lambda i: (i, 0)),
            pl.BlockSpec((tm, B_WIDTH), lambda i: (i, 1)),
            pl.BlockSpec((1, B_WIDTH), lambda i: (0, 0)),
            pl.BlockSpec((1, B_WIDTH), lambda i: (0, 0)),
            pl.BlockSpec((B_GROUPS, B_CHUNK, B_CHUNK), lambda i: (0, 0, 0)),
            pl.BlockSpec((B_CHUNK, B_GROUPS), lambda i: (0, 0)),
        ],
        out_specs=pl.BlockSpec((tm, B_WIDTH), lambda i: (i, 0)),
        out_shape=jax.ShapeDtypeStruct((rows, B_WIDTH), BF16),
        compiler_params=_params(dimension_semantics=("arbitrary",)),
        name="gmlp_mid",
    )(z, z, vn_g, vn_b, ws, bs_t)


def _diff_attn_kernel(lq1_ref, lk1_ref, lq2_ref, lk2_ref, sg_ref, q_ref, k_ref, v_ref, kc_ref, vc_ref, o_ref, vt_ref,
                      *, lam_init, tq, tk):
    qi = pl.program_id(2)
    n_lat = SEQ // tq
    expo = (HEAD_DIM ** -0.5) * math.log2(math.e)
    lam = (jnp.exp(jnp.sum(lq1_ref[...] * lk1_ref[...], axis=-1, keepdims=True))
           - jnp.exp(jnp.sum(lq2_ref[...] * lk2_ref[...], axis=-1, keepdims=True)) + lam_init)

    @pl.when(qi == 0)
    def _():
        vt_ref[:, :CTX_LEN] = vc_ref[...].T
        for t in range(SEQ // tk):
            vt_ref[:, CTX_LEN + t * tk:CTX_LEN + (t + 1) * tk] = v_ref[t * tk:(t + 1) * tk, :].T

    def run(chunks):
        qs = [q_ref[:, c * HEAD_DIM:(c + 1) * HEAD_DIM] for c in range(2)]

        def scores(chunk):
            kref, lo, size, _ = chunk
            return [_dot_t(kref[lo:lo + size, c * HEAD_DIM:(c + 1) * HEAD_DIM], qs[c]) for c in range(2)]

        m = [None, None]
        l = [None, None]
        acc = [None, None]
        s_next = scores(chunks[0])
        for t, (_, _, size, vlo) in enumerate(chunks):
            s_cur = s_next
            if t + 1 < len(chunks):
                s_next = scores(chunks[t + 1])
            for c in range(2):
                st = s_cur[c]
                m_new = jnp.max(st, axis=0, keepdims=True)
                if m[c] is not None:
                    m_new = jnp.maximum(m[c], m_new)
                p = jnp.exp2((st - m_new) * expo)
                pv = jnp.dot(vt_ref[:, vlo:vlo + size], p.astype(BF16), preferred_element_type=F32)
                if m[c] is None:
                    l[c] = jnp.sum(p, axis=0, keepdims=True)
                    acc[c] = pv
                else:
                    alpha = jnp.exp2((m[c] - m_new) * expo)
                    l[c] = alpha * l[c] + jnp.sum(p, axis=0, keepdims=True)
                    acc[c] = alpha * acc[c] + pv
                m[c] = m_new
        o = acc[0] / l[0] - lam * (acc[1] / l[1])
        o = o * lax.rsqrt(jnp.mean(o * o, axis=0, keepdims=True) + NORM_EPS) * (sg_ref[...] * (1.0 - lam_init))
        o_ref[...] = o.T.astype(BF16)

    @pl.when(qi < n_lat)
    def _():
        run([(kc_ref, 0, CTX_LEN, 0)] + [(k_ref, t * tk, tk, CTX_LEN + t * tk) for t in range(SEQ // tk)])

    @pl.when(qi >= n_lat)
    def _():
        run([(kc_ref, 0, CTX_LEN, 0)])


def _diff_attn(qkv, lq1, lk1, lq2, lk2, subln_g_col, lam_init, tq=256, tk=1024):
    hw = 2 * HEAD_DIM
    n_lat = SEQ // tq
    n_ctx = CTX_LEN // tq
    kcol = D_MODEL // hw
    ctx_blk0 = LAT_ROWS // tq
    ctx_blk = LAT_ROWS // CTX_LEN

    def qrow(b, qi):
        return jnp.where(qi < n_lat, b * n_lat + qi, ctx_blk0 + b * n_ctx + (qi - n_lat))

    vec = pl.BlockSpec((1, HEAD_DIM), lambda b, h, qi: (0, 0))
    kern = functools.partial(_diff_attn_kernel, lam_init=lam_init, tq=tq, tk=tk)
    return pl.pallas_call(
        kern,
        grid=(BATCH, C_HEADS, n_lat + n_ctx),
        in_specs=[
            vec, vec, vec, vec,
            pl.BlockSpec((hw, 1), lambda b, h, qi: (0, 0)),
            pl.BlockSpec((None, tq, hw), lambda b, h, qi: (h, qrow(b, qi), 0)),
            pl.BlockSpec((None, SEQ, hw), lambda b, h, qi: (kcol + h, b, 0)),
            pl.BlockSpec((None, SEQ, hw), lambda b, h, qi: (2 * kcol + h, b, 0)),
            pl.BlockSpec((None, CTX_LEN, hw), lambda b, h, qi: (kcol + h, ctx_blk + b, 0)),
            pl.BlockSpec((None, CTX_LEN, hw), lambda b, h, qi: (2 * kcol + h, ctx_blk + b, 0)),
        ],
        out_specs=pl.BlockSpec((tq, hw), lambda b, h, qi: (qrow(b, qi), h)),
        out_shape=jax.ShapeDtypeStruct((ROWS, D_MODEL), BF16),
        scratch_shapes=[pltpu.VMEM((hw, CTX_LEN + SEQ), BF16)],
        compiler_params=_params(dimension_semantics=("arbitrary", "arbitrary", "arbitrary")),
        name="diff_attn",
    )(lq1, lk1, lq2, lk2, subln_g_col, qkv, qkv, qkv, qkv, qkv)


def _rope_tables():
    rows = SEQ // GRID_W
    row = jnp.repeat(jnp.arange(rows, dtype=F32), GRID_W)
    col = jnp.tile(jnp.arange(GRID_W, dtype=F32), rows)
    axis_dim = HEAD_DIM // 2
    inv = ROPE_THETA ** (-jnp.arange(0, axis_dim, 2, dtype=F32) / axis_dim)
    ang = jnp.concatenate([row[:, None] * inv, col[:, None] * inv], axis=-1)
    cos = jnp.repeat(jnp.cos(ang), 2, axis=-1)
    sin = jnp.repeat(jnp.sin(ang), 2, axis=-1) * jnp.tile(jnp.array([-1.0, 1.0], F32), HEAD_DIM // 2)
    heads_per_chunk = MXU_WIDTH // HEAD_DIM
    cos_t = jnp.concatenate([jnp.tile(cos, (BATCH, heads_per_chunk)), jnp.ones((CTX_ROWS, MXU_WIDTH), F32)], axis=0)
    sin_t = jnp.concatenate([jnp.tile(sin, (BATCH, heads_per_chunk)), jnp.zeros((CTX_ROWS, MXU_WIDTH), F32)], axis=0)
    return cos_t, sin_t


def _param_tiles(mods, norm_g):
    m = mods[:, :N_GROUPS].reshape(DEPTH, N_GROUPS, N_MOD, D_MODEL)
    zero = jnp.zeros((N_GROUPS, D_MODEL), F32)
    bcast = lambda v: jnp.broadcast_to(v, (N_GROUPS, D_MODEL))
    tiles = []
    for s in range(-1, 3 * DEPTH):
        rows = [zero] * 8
        if s >= 0:
            i, slot = divmod(s, 3)
            rows[P_POST_G] = bcast(norm_g[i, 2 * slot + 1])
            rows[P_GATE] = m[i, :, 3 * slot + 2]
        if s + 1 < 3 * DEPTH:
            i, slot = divmod(s + 1, 3)
            rows[P_PRE_G] = bcast(norm_g[i, 2 * slot])
            rows[P_SCALE] = m[i, :, 3 * slot + 1]
            rows[P_SHIFT] = m[i, :, 3 * slot]
        tiles.append(jnp.stack(rows, axis=1))
    return jnp.stack(tiles, axis=0)


def kernel(x, c, ctx, c_ctx, ada_w, ada_b, norm_g, ffn_w_in, ffn_w_out, a_w_in, a_w_out, a_sink, b_w_in, b_vnorm_g,
           b_vnorm_b, b_ws, b_bs, b_w_out, c_w_in, c_w_out, c_lq1, c_lk1, c_lq2, c_lk2, c_subln_g):
    xs = (x.reshape(LAT_ROWS, D_MODEL), ctx.reshape(CTX_ROWS, D_MODEL))
    cvec = jnp.concatenate([c, c_ctx[None, :], jnp.zeros((8 - BATCH - 1, D_MODEL), F32)], axis=0)
    pp = _param_tiles(_ada_mods(cvec, ada_w, ada_b), norm_g)
    cos_t, sin_t = _rope_tables()

    mixer_w = ((a_w_in, a_w_out), (b_w_in, b_w_out), (c_w_in, c_w_out))
    wgu = _convert(_ffn_split_job(ffn_w_in, (0, 0)))
    h = _prenorm(xs, pp, 0)
    for i in range(DEPTH):
        kind, j = i % N_MIXERS, i // N_MIXERS
        ctx_live = i < DEPTH - 1
        rows_out = ROWS if ctx_live else LAT_ROWS
        s0 = 3 * i + 1
        w_mix_in, w_mix_out = mixer_w[kind]

        ffn2_split = _ffn_split_job(ffn_w_in, (i, 1))
        jobs = [_cast_job(ffn_w_out, (i, 0), D_FF_PAD), _cast_job(w_mix_in, (j,))] + ([ffn2_split] if kind != 1 else [])
        a, conv = _ffn_in(h, wgu, jobs)
        xs, h = _mm_out(a, conv[0], xs, pp, s0, rows=ROWS, coef=0.5, want_h=True)
        w_i = conv[1]

        jobs = [_cast_job(w_mix_out, (j,))] + ([ffn2_split] if kind == 1 else [])
        if kind == 0:
            qkv, mix_conv = _qkv_rope(h, w_i, cos_t, sin_t, (A_HEADS + A_KV_HEADS) * HEAD_DIM, jobs, col_major=False)
            o = _win_attn(qkv, a_sink[j], ctx_live)
        elif kind == 1:
            z, mix_conv = _gmlp_in(h, w_i, jobs)
            o = _gmlp_mid(z, b_vnorm_g[j][None, :], b_vnorm_b[j][None, :], b_ws[j], b_bs[j].T)
        else:
            lam_init = 0.8 - 0.6 * math.exp(-0.3 * i)
            qkv, mix_conv = _qkv_rope(h, w_i, cos_t, sin_t, 2 * D_MODEL, jobs, col_major=True)
            o = _diff_attn(qkv, c_lq1[j][None, :], c_lk1[j][None, :], c_lq2[j][None, :], c_lk2[j][None, :],
                           c_subln_g[j][:, None], lam_init)
        w_o = mix_conv[0]
        wgu = mix_conv[1] if kind == 1 else conv[2]
        xs, h = _mm_out(o, w_o, xs, pp, s0 + 1, rows=rows_out, coef=1.0, want_h=True)

        jobs = [_cast_job(ffn_w_out, (i, 1), D_FF_PAD)]
        if i + 1 < DEPTH:
            jobs.append(_ffn_split_job(ffn_w_in, (i + 1, 0)))
        a, conv = _ffn_in(h, wgu, jobs)
        xs, h = _mm_out(a, conv[0], xs, pp, s0 + 2, rows=rows_out, coef=0.5, want_h=i + 1 < DEPTH)
        wgu = conv[1] if i + 1 < DEPTH else None
    return xs.reshape(BATCH, SEQ, D_MODEL)
```

```python
import functools
import math
from typing import Callable, NamedTuple

import jax
import jax.numpy as jnp
from jax import lax
from jax.experimental import pallas as pl
from jax.experimental.pallas import tpu as pltpu

D_MODEL = 2048
BATCH = 2
SEQ = 4096
DEPTH = 4
GRID_W = 64
CTX_LEN = 256
N_MIXERS = 3
N_MOD = 9
NORM_EPS = 1e-6
ROPE_THETA = 10000.0
NEG_INF = -1e30
D_FF = 5504
HEAD_DIM = 128
A_HEADS = 16
A_KV_HEADS = 4
A_GROUP = 4
A_BLOCK = 128
B_CHUNK = 128
B_WIDTH = 3 * D_MODEL
B_GROUPS = 8
B_GROUP_W = B_WIDTH // B_GROUPS
C_HEADS = 8

LAT_ROWS = BATCH * SEQ
CTX_ROWS = BATCH * CTX_LEN
ROWS = LAT_ROWS + CTX_ROWS
N_GROUPS = 3
LANES = 128
BF16_SUBLANES = 16
MXU_WIDTH = 256
IN_ROW_TILES = 4
GMLP_ROW_TILES = 8
FF_TILE = 512
D_FF_PAD = -(-D_FF // FF_TILE) * FF_TILE
VMEM_LIMIT = 62 * 1024 * 1024

BF16 = jnp.bfloat16
F32 = jnp.float32

P_POST_G, P_GATE, P_PRE_G, P_SCALE, P_SHIFT = 0, 1, 2, 3, 4


def _params(**kw):
    return pltpu.CompilerParams(vmem_limit_bytes=VMEM_LIMIT, **kw)


def _group_of_block(i, tm):
    return jnp.minimum(i // (SEQ // tm), N_GROUPS - 1)


def _rms(x):
    return x * lax.rsqrt(jnp.mean(x * x, axis=-1, keepdims=True) + NORM_EPS)


def _ada_kernel(c_ref, w_ref, b_ref, o_ref):
    @pl.when(pl.program_id(1) == 0)
    def _():
        o_ref[...] = jnp.broadcast_to(b_ref[...], o_ref.shape)

    c = c_ref[...]
    a = (c * jax.nn.sigmoid(c)).astype(BF16)
    o_ref[...] += jnp.dot(a, w_ref[...].astype(BF16), preferred_element_type=F32)


def _ada_mods(cvec, ada_w, ada_b):
    kb = LANES
    n = N_MOD * D_MODEL
    return pl.pallas_call(
        _ada_kernel,
        grid=(DEPTH, D_MODEL // kb),
        in_specs=[
            pl.BlockSpec((8, kb), lambda l, k: (0, k)),
            pl.BlockSpec((None, kb, n), lambda l, k: (l, k, 0)),
            pl.BlockSpec((None, 1, n), lambda l, k: (l, 0, 0)),
        ],
        out_specs=pl.BlockSpec((None, 8, n), lambda l, k: (l, 0, 0)),
        out_shape=jax.ShapeDtypeStruct((DEPTH, 8, n), F32),
        compiler_params=_params(dimension_semantics=("arbitrary", "arbitrary")),
        name="ada_mods",
    )(cvec, ada_w, ada_b.reshape(DEPTH, 1, n))


def _pre(x, p_ref):
    gain = p_ref[P_PRE_G:P_PRE_G + 1, :] * (1.0 + p_ref[P_SCALE:P_SCALE + 1, :])
    return _rms(x) * gain + p_ref[P_SHIFT:P_SHIFT + 1, :]


def _stream_specs(x, tm):
    if not isinstance(x, tuple):
        return (x,), [pl.BlockSpec((tm, D_MODEL), lambda i: (i, 0))]
    n_lat = LAT_ROWS // tm
    return x, [pl.BlockSpec((tm, D_MODEL), lambda i: (jnp.minimum(i, n_lat - 1), 0)),
               pl.BlockSpec((tm, D_MODEL), lambda i: (jnp.maximum(i - n_lat, 0), 0))]


def _stream_tile(x_refs, rs=slice(None)):
    if len(x_refs) == 1:
        return x_refs[0][rs, :]
    n_lat = LAT_ROWS // x_refs[0].shape[0]
    return jnp.where(pl.program_id(0) < n_lat, x_refs[0][rs, :], x_refs[1][rs, :])


def _prenorm_kernel(*refs):
    *x_refs, p_ref, h_ref = refs
    h_ref[...] = _pre(_stream_tile(x_refs), p_ref).astype(BF16)


def _prenorm(x, pp, s, jobs, tm=512):
    x_arrays, x_specs = _stream_specs(x, tm)
    return _hosted_call(
        _prenorm_kernel, (*x_arrays, pp),
        x_specs + [pl.BlockSpec((None, None, 8, D_MODEL), lambda i: (s, _group_of_block(i, tm), 0, 0))],
        pl.BlockSpec((tm, D_MODEL), lambda i: (i, 0)),
        jax.ShapeDtypeStruct((ROWS, D_MODEL), BF16),
        (ROWS // tm,), jobs, "prenorm")


class _Job(NamedTuple):
    src: jax.Array
    in_spec: pl.BlockSpec
    out_spec: pl.BlockSpec
    out_shape: jax.ShapeDtypeStruct
    body: Callable
    n_blocks: int


def _row_block(rows, rows_out, steps):
    g = math.gcd(rows, rows_out)
    for rb in range(BF16_SUBLANES, g + 1, BF16_SUBLANES):
        if g % rb == 0 and rows_out // rb <= steps:
            return rb
    raise ValueError(f"no row block for {rows}->{rows_out} rows in {steps} steps")


def _cast_job(src, lead, rows_out=None):
    rows, cols = src.shape[-2:]
    rows_out = rows_out or rows

    def make(steps, lin):
        rb = _row_block(rows, rows_out, steps)
        nb_in, nb_out = rows // rb, rows_out // rb

        def body(src_ref, dst_ref, blk):
            v = src_ref[...].astype(BF16)
            if nb_out > nb_in:
                v = jnp.where(blk < nb_in, v, jnp.zeros_like(v))
            dst_ref[...] = v

        return _Job(
            src,
            pl.BlockSpec((None,) * len(lead) + (rb, cols),
                         lambda *g: lead + (jnp.minimum(lin(*g), nb_in - 1), 0)),
            pl.BlockSpec((rb, cols), lambda *g: (jnp.minimum(lin(*g), nb_out - 1), 0)),
            jax.ShapeDtypeStruct((rows_out, cols), BF16), body, nb_out)

    return make


def _ffn_split_job(ffn_w_in, lead):
    def make(steps, lin):
        rb = _row_block(D_MODEL, D_MODEL, steps)
        nb = D_MODEL // rb

        def body(src_ref, dst_ref, blk):
            del blk
            for part in range(2):
                dst_ref[part, :, :D_FF] = src_ref[:, part * D_FF:(part + 1) * D_FF].astype(BF16)
                dst_ref[part, :, D_FF:] = jnp.zeros((rb, D_FF_PAD - D_FF), BF16)

        return _Job(
            ffn_w_in,
            pl.BlockSpec((None,) * len(lead) + (rb, 2 * D_FF), lambda *g: lead + (jnp.minimum(lin(*g), nb - 1), 0)),
            pl.BlockSpec((2, rb, D_FF_PAD), lambda *g: (0, jnp.minimum(lin(*g), nb - 1), 0)),
            jax.ShapeDtypeStruct((2, D_MODEL, D_FF_PAD), BF16), body, nb)

    return make


def _hosted_call(main, main_args, main_in_specs, out_spec, out_shape, grid, job_makers, name):
    steps = math.prod(grid)
    strides = [math.prod(grid[d + 1:]) for d in range(len(grid))]
    lin = lambda *g: sum(gi * st for gi, st in zip(g, strides))
    jobs = [mk(steps, lin) for mk in job_makers]
    n_in, n_jobs = len(main_args), len(jobs)

    def kern(*refs):
        main(*refs[:n_in], refs[n_in + n_jobs])
        t = lin(*[pl.program_id(d) for d in range(len(grid))])
        for q, jb in enumerate(jobs):
            jb.body(refs[n_in + q], refs[n_in + n_jobs + 1 + q], jnp.minimum(t, jb.n_blocks - 1))

    res = pl.pallas_call(
        kern,
        grid=grid,
        in_specs=list(main_in_specs) + [jb.in_spec for jb in jobs],
        out_specs=[out_spec] + [jb.out_spec for jb in jobs],
        out_shape=[out_shape] + [jb.out_shape for jb in jobs],
        compiler_params=_params(dimension_semantics=("arbitrary",) * len(grid)),
        name=name,
    )(*main_args, *[jb.src for jb in jobs])
    return res[0], list(res[1:])


def _swiglu_kernel(h_ref, wg_ref, wu_ref, o_ref):
    h = h_ref[...]
    half = o_ref.shape[1] // 2
    parts = []
    for cs in (slice(0, half), slice(half, 2 * half)):
        parts.append((cs, jnp.dot(h, wg_ref[:, cs], preferred_element_type=F32),
                      jnp.dot(h, wu_ref[:, cs], preferred_element_type=F32)))
    for cs, g, u in parts:
        o_ref[:, cs] = (g * jax.nn.sigmoid(g) * u).astype(BF16)


def _ffn_in(h, wgu, jobs, tn=FF_TILE):
    rows = h.shape[0]
    tm = rows // IN_ROW_TILES
    return _hosted_call(
        _swiglu_kernel, (h, wgu, wgu),
        [pl.BlockSpec((tm, D_MODEL), lambda j, i: (i, 0)),
         pl.BlockSpec((None, D_MODEL, tn), lambda j, i: (0, 0, j)),
         pl.BlockSpec((None, D_MODEL, tn), lambda j, i: (1, 0, j))],
        pl.BlockSpec((tm, tn), lambda j, i: (i, j)),
        jax.ShapeDtypeStruct((rows, D_FF_PAD), BF16),
        (D_FF_PAD // tn, rows // tm), jobs, "ffn_in")


def _gelu_tanh(y):
    c0 = math.sqrt(2.0 / math.pi)
    half_y = 0.5 * y
    return half_y + half_y * jnp.tanh(y * (c0 + (c0 * 0.044715) * (y * y)))


def _gelu_kernel(h_ref, w_ref, o_ref):
    half = h_ref.shape[0] // 2
    halves = (slice(0, half), slice(half, 2 * half))
    ys = [jnp.dot(h_ref[rs, :], w_ref[...], preferred_element_type=F32) for rs in halves]
    for rs, y in zip(halves, ys):
        o_ref[rs, :] = _gelu_tanh(y).astype(BF16)


def _gmlp_in(h, w, jobs, tn=2048):
    rows = h.shape[0]
    tm = rows // GMLP_ROW_TILES
    n = w.shape[1]
    return _hosted_call(
        _gelu_kernel, (h, w),
        [pl.BlockSpec((tm, D_MODEL), lambda j, i: (i, 0)),
         pl.BlockSpec((D_MODEL, tn), lambda j, i: (0, j))],
        pl.BlockSpec((tm, tn), lambda j, i: (i, j)),
        jax.ShapeDtypeStruct((rows, n), BF16),
        (n // tn, rows // tm), jobs, "gmlp_in")


def _qkv_rope_kernel(h_ref, w_ref, cos_ref, sin_ref, swap_ref, o_ref, *, n_rope_tiles, tn, col_major):
    j = pl.program_id(1)

    def store(q0, val):
        if col_major:
            o_ref[q0 // MXU_WIDTH] = val
        else:
            o_ref[:, q0:q0 + MXU_WIDTH] = val

    @pl.when(j < n_rope_tiles)
    def _():
        h = h_ref[...]
        c = cos_ref[...]
        s = sin_ref[...]
        swap = swap_ref[...]
        y = jnp.dot(h, w_ref[...], preferred_element_type=F32)
        hi = y.astype(BF16)
        lo = (y - hi.astype(F32)).astype(BF16)
        for q0 in range(0, tn, MXU_WIDTH):
            cs = slice(q0, q0 + MXU_WIDTH)
            partner = (jnp.dot(hi[:, cs], swap, preferred_element_type=F32)
                       + jnp.dot(lo[:, cs], swap, preferred_element_type=F32))
            store(q0, (y[:, cs] * c + partner * s).astype(BF16))

    @pl.when(j >= n_rope_tiles)
    def _():
        y = jnp.dot(h_ref[...], w_ref[...], preferred_element_type=F32).astype(BF16)
        for q0 in range(0, tn, MXU_WIDTH):
            store(q0, y[:, q0:q0 + MXU_WIDTH])


def _qkv_rope(h, w, cos_t, sin_t, rope_width, jobs, col_major, tn=512):
    rows = h.shape[0]
    tm = rows // IN_ROW_TILES
    n = w.shape[1]
    kern = functools.partial(_qkv_rope_kernel, n_rope_tiles=rope_width // tn, tn=tn, col_major=col_major)
    if col_major:
        out_spec = pl.BlockSpec((tn // MXU_WIDTH, tm, MXU_WIDTH), lambda i, j: (j, i, 0))
        out_shape = jax.ShapeDtypeStruct((n // MXU_WIDTH, rows, MXU_WIDTH), BF16)
    else:
        out_spec = pl.BlockSpec((tm, tn), lambda i, j: (i, j))
        out_shape = jax.ShapeDtypeStruct((rows, n), BF16)
    lane = jnp.arange(MXU_WIDTH)
    swap = (lane[:, None] == (lane[None, :] ^ 1)).astype(BF16)
    return _hosted_call(
        kern, (h, w, cos_t, sin_t, swap),
        [pl.BlockSpec((tm, D_MODEL), lambda i, j: (i, 0)),
         pl.BlockSpec((D_MODEL, tn), lambda i, j: (0, j)),
         pl.BlockSpec((tm, MXU_WIDTH), lambda i, j: (i, 0)),
         pl.BlockSpec((tm, MXU_WIDTH), lambda i, j: (i, 0)),
         pl.BlockSpec((MXU_WIDTH, MXU_WIDTH), lambda i, j: (0, 0))],
        out_spec, out_shape, (rows // tm, n // tn), jobs, "qkv_rope")


def _mm_out_kernel(a_ref, w_ref, p_ref, *refs, coef, n_x, want_h):
    x_refs, xo_ref, maybe_h_ref = refs[:n_x], refs[n_x], refs[n_x + 1:]
    post = coef * (p_ref[P_GATE:P_GATE + 1, :] * p_ref[P_POST_G:P_POST_G + 1, :])
    half = a_ref.shape[0] // 2
    for rs in (slice(0, half), slice(half, 2 * half)):
        y = jnp.dot(a_ref[rs, :], w_ref[...], preferred_element_type=F32)
        xn = _stream_tile(x_refs, rs) + _rms(y) * post
        xo_ref[rs, :] = xn
        if want_h:
            maybe_h_ref[0][rs, :] = _pre(xn, p_ref).astype(BF16)


def _mm_out(a, w, x, pp, s, *, rows, coef, want_h):
    k = a.shape[1]

    n_x = len(x) if isinstance(x, tuple) else 1

    def vmem_bytes(tm):
        return k * D_MODEL * 2 + 2 * tm * (k * 2 + D_MODEL * (4 * n_x + 4 + 2)) + tm * D_MODEL * 4

    tm = 512 if vmem_bytes(512) <= VMEM_LIMIT - (2 << 20) else 256
    x_arrays, x_specs = _stream_specs(x, tm)
    out_shape = [jax.ShapeDtypeStruct((rows, D_MODEL), F32)]
    out_specs = [pl.BlockSpec((tm, D_MODEL), lambda i: (i, 0))]
    if want_h:
        out_shape.append(jax.ShapeDtypeStruct((rows, D_MODEL), BF16))
        out_specs.append(pl.BlockSpec((tm, D_MODEL), lambda i: (i, 0)))
    res = pl.pallas_call(
        functools.partial(_mm_out_kernel, coef=coef, n_x=len(x_arrays), want_h=want_h),
        grid=(rows // tm,),
        in_specs=[
            pl.BlockSpec((tm, k), lambda i: (i, 0)),
            pl.BlockSpec((k, D_MODEL), lambda i: (0, 0), pipeline_mode=pl.Buffered(1)),
            pl.BlockSpec((None, None, 8, D_MODEL), lambda i: (s, _group_of_block(i, tm), 0, 0)),
        ] + x_specs,
        out_specs=out_specs,
        out_shape=out_shape,
        compiler_params=_params(dimension_semantics=("arbitrary",)),
        name="mm_out",
    )(a, w, pp, *x_arrays)
    return (res[0], res[1]) if want_h else (res[0], None)


def _dot_t(a, b):
    return lax.dot_general(a, b, (((1,), (1,)), ((), ())), preferred_element_type=F32)


def _dot_tn(a, b):
    return lax.dot_general(a, b, (((0,), (0,)), ((), ())), preferred_element_type=F32)


def _win_attn_kernel(sink_ref, q_ref, kv_ref, kvc_ref, o_ref):
    n = pl.program_id(1)
    n_lat = SEQ // A_BLOCK
    kvw = A_KV_HEADS * HEAD_DIM
    scale = HEAD_DIM ** -0.5
    expo = scale * math.log2(math.e)
    gq = A_GROUP * A_BLOCK
    lane = lax.broadcasted_iota(jnp.int32, (1, gq), 1)
    qi = lax.broadcasted_iota(jnp.int32, (A_BLOCK, gq), 1) & (A_BLOCK - 1)
    mi = lax.broadcasted_iota(jnp.int32, (A_BLOCK, gq), 0)

    def heads(kk):
        q4 = jnp.concatenate(
            [q_ref[:, (kk * A_GROUP + g) * HEAD_DIM:(kk * A_GROUP + g + 1) * HEAD_DIM] for g in range(A_GROUP)],
            axis=0)
        sink = jnp.full((1, gq), sink_ref[kk * A_GROUP] / scale, F32)
        for g in range(1, A_GROUP):
            sink = jnp.where(lane >= g * A_BLOCK, sink_ref[kk * A_GROUP + g] / scale, sink)
        return q4, sink, slice(kk * HEAD_DIM, (kk + 1) * HEAD_DIM), slice(kvw + kk * HEAD_DIM, kvw + (kk + 1) * HEAD_DIM)

    def finish(kk, pieces, sink):
        m = sink
        for s, _ in pieces:
            m = jnp.maximum(m, jnp.max(s, axis=0, keepdims=True))
        den = jnp.exp2((sink - m) * expo)
        ot = None
        for s, v in pieces:
            p = jnp.exp2((s - m) * expo)
            den = den + jnp.sum(p, axis=0, keepdims=True)
            pv = _dot_tn(v, p.astype(BF16))
            ot = pv if ot is None else ot + pv
        ot = ot / den
        for g in range(A_GROUP):
            h = kk * A_GROUP + g
            o_ref[:, h * HEAD_DIM:(h + 1) * HEAD_DIM] = ot[:, g * A_BLOCK:(g + 1) * A_BLOCK].T.astype(BF16)

    @pl.when(n < n_lat)
    def _():
        off_l = jnp.where(n > 0, 0, 2 * A_BLOCK)
        off_r = jnp.where(n < n_lat - 1, 0, 2 * A_BLOCK)
        ok_l = mi >= qi + off_l
        ok_r = mi <= qi - off_r
        rows_l, rows_m, rows_r = [pl.ds(pl.multiple_of(jnp.clip(n + d, 0, n_lat - 1) * A_BLOCK, A_BLOCK), A_BLOCK)
                                  for d in (-1, 0, 1)]
        def scores(kk):
            q4, sink, kc, vc = heads(kk)
            s_l = jnp.where(ok_l, _dot_t(kv_ref[rows_l, kc], q4), NEG_INF)
            s_r = jnp.where(ok_r, _dot_t(kv_ref[rows_r, kc], q4), NEG_INF)
            return ([(_dot_t(kvc_ref[:, kc], q4), kvc_ref[:, vc]), (s_l, kv_ref[rows_l, vc]),
                     (_dot_t(kv_ref[rows_m, kc], q4), kv_ref[rows_m, vc]), (s_r, kv_ref[rows_r, vc])], sink)

        nxt = scores(0)
        for kk in range(A_KV_HEADS):
            cur = nxt
            if kk + 1 < A_KV_HEADS:
                nxt = scores(kk + 1)
            finish(kk, *cur)

    @pl.when(n >= n_lat)
    def _():
        for kk in range(A_KV_HEADS):
            q4, sink, kc, vc = heads(kk)
            finish(kk, [(_dot_t(kvc_ref[:, kc], q4), kvc_ref[:, vc])], sink)


def _win_attn(qkv, sink, with_ctx):
    n_lat = SEQ // A_BLOCK
    n_ctx = CTX_LEN // A_BLOCK
    steps = n_lat + (n_ctx if with_ctx else 0)
    qw = A_HEADS * HEAD_DIM
    kvw2 = 2 * A_KV_HEADS * HEAD_DIM
    ctx_blk0 = LAT_ROWS // A_BLOCK

    def qrow(b, n):
        return jnp.where(n < n_lat, b * n_lat + n, ctx_blk0 + b * n_ctx + (n - n_lat))

    rows = LAT_ROWS + (CTX_ROWS if with_ctx else 0)
    return pl.pallas_call(
        _win_attn_kernel,
        grid=(BATCH, steps),
        in_specs=[
            pl.BlockSpec(memory_space=pltpu.SMEM),
            pl.BlockSpec((A_BLOCK, qw), lambda b, n: (qrow(b, n), 0)),
            pl.BlockSpec((SEQ, kvw2), lambda b, n: (b, qw // kvw2)),
            pl.BlockSpec((CTX_LEN, kvw2), lambda b, n: (LAT_ROWS // CTX_LEN + b, qw // kvw2)),
        ],
        out_specs=pl.BlockSpec((A_BLOCK, D_MODEL), lambda b, n: (qrow(b, n), 0)),
        out_shape=jax.ShapeDtypeStruct((rows, D_MODEL), BF16),
        compiler_params=_params(dimension_semantics=("arbitrary", "arbitrary")),
        name="win_attn",
    )(sink, qkv, qkv, qkv)


def _gmlp_mid_kernel(u_ref, v_ref, g_ref, b_ref, ws_ref, bs_ref, o_ref, *, tm):
    v = v_ref[...].astype(F32)
    mu = jnp.mean(v, axis=-1, keepdims=True)
    vc = v - mu
    vn = vc * lax.rsqrt(jnp.mean(vc * vc, axis=-1, keepdims=True) + NORM_EPS)
    vn = (vn * g_ref[...] + b_ref[...]).astype(BF16)
    for g in range(B_GROUPS):
        cs = slice(g * B_GROUP_W, (g + 1) * B_GROUP_W)
        ws = ws_ref[g].astype(BF16)
        for c in range(tm // B_CHUNK):
            rs = slice(c * B_CHUNK, (c + 1) * B_CHUNK)
            mixed = jnp.dot(ws, vn[rs, cs], preferred_element_type=F32) + bs_ref[:, g:g + 1]
            o_ref[rs, cs] = (u_ref[rs, cs].astype(F32) * mixed).astype(BF16)


def _gmlp_mid(z, vn_g, vn_b, ws, bs_t, tm=256):
    rows = z.shape[0]
    return pl.pallas_call(
        functools.partial(_gmlp_mid_kernel, tm=tm),
        grid=(rows // tm,),
        in_specs=[
            pl.BlockSpec((tm, B_WIDTH), lambda i: (i, 0)),
            pl.BlockSpec((tm, B_WIDTH), lambda i: (i, 1)),
            pl.BlockSpec((1, B_WIDTH), lambda i: (0, 0)),
            pl.BlockSpec((1, B_WIDTH), lambda i: (0, 0)),
            pl.BlockSpec((B_GROUPS, B_CHUNK, B_CHUNK), lambda i: (0, 0, 0)),
            pl.BlockSpec((B_CHUNK, B_GROUPS), lambda i: (0, 0)),
        ],
        out_specs=pl.BlockSpec((tm, B_WIDTH), lambda i: (i, 0)),
        out_shape=jax.ShapeDtypeStruct((rows, B_WIDTH), BF16),
        compiler_params=_params(dimension_semantics=("arbitrary",)),
        name="gmlp_mid",
    )(z, z, vn_g, vn_b, ws, bs_t)


def _diff_attn_kernel(lq1_ref, lk1_ref, lq2_ref, lk2_ref, sg_ref, q_ref, k_ref, v_ref, kc_ref, vc_ref, o_ref, vt_ref,
                      *, lam_init, tq, tk):
    qi = pl.program_id(2)
    n_lat = SEQ // tq
    expo = (HEAD_DIM ** -0.5) * math.log2(math.e)
    lam = (jnp.exp(jnp.sum(lq1_ref[...] * lk1_ref[...], axis=-1, keepdims=True))
           - jnp.exp(jnp.sum(lq2_ref[...] * lk2_ref[...], axis=-1, keepdims=True)) + lam_init)

    @pl.when(qi == 0)
    def _():
        vt_ref[:, :CTX_LEN] = vc_ref[...].T
        for t in range(SEQ // tk):
            vt_ref[:, CTX_LEN + t * tk:CTX_LEN + (t + 1) * tk] = v_ref[t * tk:(t + 1) * tk, :].T

    def run(chunks):
        qs = [q_ref[:, c * HEAD_DIM:(c + 1) * HEAD_DIM] for c in range(2)]

        def scores(chunk):
            kref, lo, size, _ = chunk
            return [_dot_t(kref[lo:lo + size, c * HEAD_DIM:(c + 1) * HEAD_DIM], qs[c]) for c in range(2)]

        m = [None, None]
        l = [None, None]
        acc = [None, None]
        s_next = scores(chunks[0])
        for t, (_, _, size, vlo) in enumerate(chunks):
            s_cur = s_next
            if t + 1 < len(chunks):
                s_next = scores(chunks[t + 1])
            for c in range(2):
                st = s_cur[c]
                m_new = jnp.max(st, axis=0, keepdims=True)
                if m[c] is not None:
                    m_new = jnp.maximum(m[c], m_new)
                p = jnp.exp2((st - m_new) * expo)
                pv = jnp.dot(vt_ref[:, vlo:vlo + size], p.astype(BF16), preferred_element_type=F32)
                if m[c] is None:
                    l[c] = jnp.sum(p, axis=0, keepdims=True)
                    acc[c] = pv
                else:
                    alpha = jnp.exp2((m[c] - m_new) * expo)
                    l[c] = alpha * l[c] + jnp.sum(p, axis=0, keepdims=True)
                    acc[c] = alpha * acc[c] + pv
                m[c] = m_new
        o = acc[0] / l[0] - lam * (acc[1] / l[1])
        o = o * lax.rsqrt(jnp.mean(o * o, axis=0, keepdims=True) + NORM_EPS) * (sg_ref[...] * (1.0 - lam_init))
        o_ref[...] = o.T.astype(BF16)

    @pl.when(qi < n_lat)
    def _():
        run([(kc_ref, 0, CTX_LEN, 0)] + [(k_ref, t * tk, tk, CTX_LEN + t * tk) for t in range(SEQ // tk)])

    @pl.when(qi >= n_lat)
    def _():
        run([(kc_ref, 0, CTX_LEN, 0)])


def _diff_attn(qkv, lq1, lk1, lq2, lk2, subln_g_col, lam_init, tq=256, tk=1024):
    hw = 2 * HEAD_DIM
    n_lat = SEQ // tq
    n_ctx = CTX_LEN // tq
    kcol = D_MODEL // hw
    ctx_blk0 = LAT_ROWS // tq
    ctx_blk = LAT_ROWS // CTX_LEN

    def qrow(b, qi):
        return jnp.where(qi < n_lat, b * n_lat + qi, ctx_blk0 + b * n_ctx + (qi - n_lat))

    vec = pl.BlockSpec((1, HEAD_DIM), lambda b, h, qi: (0, 0))
    kern = functools.partial(_diff_attn_kernel, lam_init=lam_init, tq=tq, tk=tk)
    return pl.pallas_call(
        kern,
        grid=(BATCH, C_HEADS, n_lat + n_ctx),
        in_specs=[
            vec, vec, vec, vec,
            pl.BlockSpec((hw, 1), lambda b, h, qi: (0, 0)),
            pl.BlockSpec((None, tq, hw), lambda b, h, qi: (h, qrow(b, qi), 0)),
            pl.BlockSpec((None, SEQ, hw), lambda b, h, qi: (kcol + h, b, 0)),
            pl.BlockSpec((None, SEQ, hw), lambda b, h, qi: (2 * kcol + h, b, 0)),
            pl.BlockSpec((None, CTX_LEN, hw), lambda b, h, qi: (kcol + h, ctx_blk + b, 0)),
            pl.BlockSpec((None, CTX_LEN, hw), lambda b, h, qi: (2 * kcol + h, ctx_blk + b, 0)),
        ],
        out_specs=pl.BlockSpec((tq, hw), lambda b, h, qi: (qrow(b, qi), h)),
        out_shape=jax.ShapeDtypeStruct((ROWS, D_MODEL), BF16),
        scratch_shapes=[pltpu.VMEM((hw, CTX_LEN + SEQ), BF16)],
        compiler_params=_params(dimension_semantics=("arbitrary", "arbitrary", "arbitrary")),
        name="diff_attn",
    )(lq1, lk1, lq2, lk2, subln_g_col, qkv, qkv, qkv, qkv, qkv)


def _rope_tables():
    rows = SEQ // GRID_W
    row = jnp.repeat(jnp.arange(rows, dtype=F32), GRID_W)
    col = jnp.tile(jnp.arange(GRID_W, dtype=F32), rows)
    axis_dim = HEAD_DIM // 2
    inv = ROPE_THETA ** (-jnp.arange(0, axis_dim, 2, dtype=F32) / axis_dim)
    ang = jnp.concatenate([row[:, None] * inv, col[:, None] * inv], axis=-1)
    cos = jnp.repeat(jnp.cos(ang), 2, axis=-1)
    sin = jnp.repeat(jnp.sin(ang), 2, axis=-1) * jnp.tile(jnp.array([-1.0, 1.0], F32), HEAD_DIM // 2)
    heads_per_chunk = MXU_WIDTH // HEAD_DIM
    cos_t = jnp.concatenate([jnp.tile(cos, (BATCH, heads_per_chunk)), jnp.ones((CTX_ROWS, MXU_WIDTH), F32)], axis=0)
    sin_t = jnp.concatenate([jnp.tile(sin, (BATCH, heads_per_chunk)), jnp.zeros((CTX_ROWS, MXU_WIDTH), F32)], axis=0)
    return cos_t, sin_t


def _param_tiles(mods, norm_g):
    m = mods[:, :N_GROUPS].reshape(DEPTH, N_GROUPS, N_MOD, D_MODEL)
    zero = jnp.zeros((N_GROUPS, D_MODEL), F32)
    bcast = lambda v: jnp.broadcast_to(v, (N_GROUPS, D_MODEL))
    tiles = []
    for s in range(-1, 3 * DEPTH):
        rows = [zero] * 8
        if s >= 0:
            i, slot = divmod(s, 3)
            rows[P_POST_G] = bcast(norm_g[i, 2 * slot + 1])
            rows[P_GATE] = m[i, :, 3 * slot + 2]
        if s + 1 < 3 * DEPTH:
            i, slot = divmod(s + 1, 3)
            rows[P_PRE_G] = bcast(norm_g[i, 2 * slot])
            rows[P_SCALE] = m[i, :, 3 * slot + 1]
            rows[P_SHIFT] = m[i, :, 3 * slot]
        tiles.append(jnp.stack(rows, axis=1))
    return jnp.stack(tiles, axis=0)


def kernel(x, c, ctx, c_ctx, ada_w, ada_b, norm_g, ffn_w_in, ffn_w_out, a_w_in, a_w_out, a_sink, b_w_in, b_vnorm_g,
           b_vnorm_b, b_ws, b_bs, b_w_out, c_w_in, c_w_out, c_lq1, c_lk1, c_lq2, c_lk2, c_subln_g):
    xs = (x.reshape(LAT_ROWS, D_MODEL), ctx.reshape(CTX_ROWS, D_MODEL))
    cvec = jnp.concatenate([c, c_ctx[None, :], jnp.zeros((8 - BATCH - 1, D_MODEL), F32)], axis=0)
    pp = _param_tiles(_ada_mods(cvec, ada_w, ada_b), norm_g)
    cos_t, sin_t = _rope_tables()

    mixer_w = ((a_w_in, a_w_out), (b_w_in, b_w_out), (c_w_in, c_w_out))
    h, (wgu,) = _prenorm(xs, pp, 0, [_ffn_split_job(ffn_w_in, (0, 0))])
    for i in range(DEPTH):
        kind, j = i % N_MIXERS, i // N_MIXERS
        ctx_live = i < DEPTH - 1
        rows_out = ROWS if ctx_live else LAT_ROWS
        s0 = 3 * i + 1
        w_mix_in, w_mix_out = mixer_w[kind]

        ffn2_split = _ffn_split_job(ffn_w_in, (i, 1))
        jobs = [_cast_job(ffn_w_out, (i, 0), D_FF_PAD), _cast_job(w_mix_in, (j,))] + ([ffn2_split] if kind != 1 else [])
        a, conv = _ffn_in(h, wgu, jobs)
        xs, h = _mm_out(a, conv[0], xs, pp, s0, rows=ROWS, coef=0.5, want_h=True)
        w_i = conv[1]

        jobs = [_cast_job(w_mix_out, (j,))] + ([ffn2_split] if kind == 1 else [])
        if kind == 0:
            qkv, mix_conv = _qkv_rope(h, w_i, cos_t, sin_t, (A_HEADS + A_KV_HEADS) * HEAD_DIM, jobs, col_major=False)
            o = _win_attn(qkv, a_sink[j], ctx_live)
        elif kind == 1:
            z, mix_conv = _gmlp_in(h, w_i, jobs)
            o = _gmlp_mid(z, b_vnorm_g[j][None, :], b_vnorm_b[j][None, :], b_ws[j], b_bs[j].T)
        else:
            lam_init = 0.8 - 0.6 * math.exp(-0.3 * i)
            qkv, mix_conv = _qkv_rope(h, w_i, cos_t, sin_t, 2 * D_MODEL, jobs, col_major=True)
            o = _diff_attn(qkv, c_lq1[j][None, :], c_lk1[j][None, :], c_lq2[j][None, :], c_lk2[j][None, :],
                           c_subln_g[j][:, None], lam_init)
        w_o = mix_conv[0]
        wgu = mix_conv[1] if kind == 1 else conv[2]
        xs, h = _mm_out(o, w_o, xs, pp, s0 + 1, rows=rows_out, coef=1.0, want_h=True)

        jobs = [_cast_job(ffn_w_out, (i, 1), D_FF_PAD)]
        if i + 1 < DEPTH:
            jobs.append(_ffn_split_job(ffn_w_in, (i + 1, 0)))
        a, conv = _ffn_in(h, wgu, jobs)
        xs, h = _mm_out(a, conv[0], xs, pp, s0 + 2, rows=rows_out, coef=0.5, want_h=i + 1 < DEPTH)
        wgu = conv[1] if i + 1 < DEPTH else None
    return xs.reshape(BATCH, SEQ, D_MODEL)
```
